```python
import jax, jax.numpy as jnp
from jax import lax
import numpy as np

D_MODEL = 4096
BATCH = 4
SEQ = 4096
DEPTH = 1

HEAD_DIM = 128
SWA_Q_HEADS = D_MODEL // (2 * HEAD_DIM)
SWA_KV_HEADS = SWA_Q_HEADS // 4
SWA_GROUP = SWA_Q_HEADS // SWA_KV_HEADS
SWA_WINDOW = 128
SB_HEADS = D_MODEL // (2 * HEAD_DIM)
BLOCK = 128
D_FF = 4 * D_MODEL
ROPE_THETA = 10000.0
NORM_EPS = 1e-6
N_BRANCHES = 2
N_MOD = 6

SWA_Q_W = SWA_Q_HEADS * HEAD_DIM
SWA_KV_W = SWA_KV_HEADS * HEAD_DIM
SB_W = SB_HEADS * HEAD_DIM
D_IN = SWA_Q_W + 2 * SWA_KV_W + 3 * SB_W
IN_SPLITS = tuple(int(v) for v in np.cumsum(
    [SWA_Q_W, SWA_KV_W, SWA_KV_W, SB_W, SB_W]))

kernel_name = "hybrid_swa_stickbreaking_gated_block"


def rms_norm(x, g):
    xf = x.astype(jnp.float32)
    y = xf * lax.rsqrt(jnp.mean(xf * xf, axis=-1, keepdims=True) + NORM_EPS)
    return (y * g.astype(jnp.float32)).astype(x.dtype)


def rope_tables(seq_len):
    inv_freq = 1.0 / (ROPE_THETA ** (jnp.arange(0, HEAD_DIM, 2, dtype=jnp.float32) / HEAD_DIM))
    ang = jnp.arange(seq_len, dtype=jnp.float32)[:, None] * inv_freq[None, :]
    return jnp.cos(ang), jnp.sin(ang)


def apply_rope(x, cos, sin):
    xf = x.astype(jnp.float32)
    x1, x2 = jnp.split(xf, 2, axis=-1)
    c = cos[None, :, None, :]
    s = sin[None, :, None, :]
    return jnp.concatenate([x1 * c - x2 * s, x2 * c + x1 * s], axis=-1).astype(x.dtype)


def sliding_window_gqa(q, k, v, sinks):
    b, s = q.shape[0], q.shape[1]
    nb = s // BLOCK
    qb = q.reshape(b, nb, BLOCK, SWA_KV_HEADS, SWA_GROUP, HEAD_DIM)

    def with_prev(t):
        tb = t.reshape(b, nb, BLOCK, SWA_KV_HEADS, HEAD_DIM)
        prev = jnp.pad(tb[:, :-1], ((0, 0), (1, 0), (0, 0), (0, 0), (0, 0)))
        return jnp.concatenate([prev, tb], axis=2)

    kw, vw = with_prev(k), with_prev(v)
    scores = jnp.einsum('bnqhgd,bnkhd->bnhgqk', qb, kw).astype(jnp.float32) * (HEAD_DIM ** -0.5)
    qi = jnp.arange(BLOCK)[:, None]
    kj = jnp.arange(2 * BLOCK)[None, :]
    diff = qi + BLOCK - kj
    band = (diff >= 0) & (diff < SWA_WINDOW)
    blk = jnp.arange(nb)[:, None, None]
    valid = band[None] & ((blk > 0) | (kj >= BLOCK)[None])
    scores = jnp.where(valid[None, :, None, None], scores, -jnp.inf)
    sink = sinks.astype(jnp.float32).reshape(SWA_KV_HEADS, SWA_GROUP)[None, None, :, :, None, None]
    m = jnp.maximum(jnp.max(scores, axis=-1, keepdims=True), sink)
    p = jnp.exp(scores - m)
    denom = jnp.sum(p, axis=-1, keepdims=True) + jnp.exp(sink - m)
    probs = (p / denom).astype(v.dtype)
    out = jnp.einsum('bnhgqk,bnkhd->bnqhgd', probs, vw)
    return out.reshape(b, s, SWA_Q_HEADS * HEAD_DIM)


def stick_breaking_attention(q, k, v):
    b, s = q.shape[0], q.shape[1]
    nb = s // BLOCK
    qb = q.reshape(b, nb, BLOCK, SB_HEADS, HEAD_DIM).transpose(1, 0, 2, 3, 4)
    key_pos = jnp.arange(s)

    def one_block(args):
        q_blk, n = args
        z = jnp.einsum('bqhd,bshd->bhqs', q_blk, k).astype(jnp.float32) * (HEAD_DIM ** -0.5)
        t = n * BLOCK + jnp.arange(BLOCK)
        valid = key_pos[None, :] < t[:, None]
        log_beta = jax.nn.log_sigmoid(z)
        log_one_minus = jnp.where(valid, log_beta - z, 0.0)
        suffix = lax.cumsum(log_one_minus, axis=3, reverse=True) - log_one_minus
        att = jnp.where(valid, jnp.exp(log_beta + suffix), 0.0)
        return jnp.einsum('bhqs,bshd->bqhd', att.astype(v.dtype), v)

    out = lax.map(one_block, (qb, jnp.arange(nb)))
    return out.transpose(1, 0, 2, 3, 4).reshape(b, s, SB_HEADS * HEAD_DIM)


def setup_inputs(seed: int = 0) -> dict:
    key = jax.random.key(seed)
    ks = jax.random.split(key, 20)
    f32 = jnp.float32

    def nrm(k, shape, fan_in):
        return jax.random.normal(k, shape, f32) * (fan_in ** -0.5)

    def gain(k):
        return 1.0 + 0.05 * jax.random.normal(k, (DEPTH, D_MODEL), f32)

    return {
        "x": jax.random.normal(ks[0], (BATCH, SEQ, D_MODEL), f32),
        "c": jax.random.normal(ks[1], (BATCH, D_MODEL), f32),
        "w_ada": nrm(ks[2], (DEPTH, D_MODEL, N_MOD * D_MODEL), D_MODEL) * 0.5,
        "b_ada": 0.02 * jax.random.normal(ks[3], (DEPTH, N_MOD * D_MODEL), f32),
        "g_pre_mix": gain(ks[4]),
        "g_post_mix": gain(ks[5]),
        "g_pre_mlp": gain(ks[6]),
        "g_post_mlp": gain(ks[7]),
        "w_in": nrm(ks[8], (DEPTH, D_MODEL, D_IN), D_MODEL),
        "attn_sinks": 0.5 * jax.random.normal(ks[9], (DEPTH, SWA_Q_HEADS), f32),
        "w_branch_gate": nrm(ks[10], (DEPTH, D_MODEL, N_BRANCHES * D_MODEL), D_MODEL),
        "b_branch_gate": 0.02 * jax.random.normal(ks[11], (DEPTH, N_BRANCHES * D_MODEL), f32),
        "w_proj_swa": nrm(ks[12], (DEPTH, SWA_Q_W, D_MODEL), SWA_Q_W),
        "w_proj_sb": nrm(ks[13], (DEPTH, SB_W, D_MODEL), SB_W),
        "w_out": nrm(ks[14], (DEPTH, D_MODEL, D_MODEL), D_MODEL),
        "w_ff_in": nrm(ks[15], (DEPTH, D_MODEL, D_FF), D_MODEL),
        "w_ff_out": nrm(ks[16], (DEPTH, D_FF, D_MODEL), D_FF),
    }


def reference(x, c, w_ada, b_ada, g_pre_mix, g_post_mix, g_pre_mlp, g_post_mlp,
              w_in, attn_sinks, w_branch_gate, b_branch_gate, w_proj_swa, w_proj_sb,
              w_out, w_ff_in, w_ff_out):
    b, s = x.shape[0], x.shape[1]
    cos, sin = rope_tables(s)
    c_act = jax.nn.silu(c)
    for l in range(DEPTH):
        mod = c_act @ w_ada[l] + b_ada[l]
        shift1, scale1, gate1, shift2, scale2, gate2 = [m[:, None, :] for m in jnp.split(mod, N_MOD, axis=-1)]

        h = rms_norm(x, g_pre_mix[l]) * (1.0 + scale1) + shift1
        proj = h @ w_in[l]
        qa, ka, va, qb_, kb_, vb_ = jnp.split(proj, IN_SPLITS, axis=-1)
        qa = apply_rope(qa.reshape(b, s, SWA_Q_HEADS, HEAD_DIM), cos, sin)
        ka = apply_rope(ka.reshape(b, s, SWA_KV_HEADS, HEAD_DIM), cos, sin)
        va = va.reshape(b, s, SWA_KV_HEADS, HEAD_DIM)
        y_swa = sliding_window_gqa(qa, ka, va, attn_sinks[l]) @ w_proj_swa[l]
        y_sb = stick_breaking_attention(
            qb_.reshape(b, s, SB_HEADS, HEAD_DIM),
            kb_.reshape(b, s, SB_HEADS, HEAD_DIM),
            vb_.reshape(b, s, SB_HEADS, HEAD_DIM)) @ w_proj_sb[l]
        gate_a, gate_b = jnp.split(jax.nn.sigmoid(h @ w_branch_gate[l] + b_branch_gate[l]), N_BRANCHES, axis=-1)
        mixed = (gate_a * y_swa + gate_b * y_sb) @ w_out[l]
        x = x + gate1 * rms_norm(mixed, g_post_mix[l])

        h = rms_norm(x, g_pre_mlp[l]) * (1.0 + scale2) + shift2
        f = jnp.square(jax.nn.relu(h @ w_ff_in[l])) @ w_ff_out[l]
        x = x + gate2 * rms_norm(f, g_post_mlp[l])
    return x
```

```python
import functools
import math

import jax
import jax.numpy as jnp
from jax import lax
from jax.experimental import pallas as pl
from jax.experimental.pallas import tpu as pltpu

F32 = jnp.float32
BF16 = jnp.bfloat16

HEAD_DIM = 128
LANES = 128
SWA_WINDOW = 128
SWA_GROUP = 4
ROPE_THETA = 10000.0
NORM_EPS = 1e-6
N_MOD = 6
MIB = 1024 * 1024
VMEM_LIMIT_BYTES = 56 * MIB


def _tile(dim, pref):
    t = min(dim, pref)
    while dim % t:
        t //= 2
    return t


def _cparams(semantics, vmem=VMEM_LIMIT_BYTES):
    return pltpu.CompilerParams(dimension_semantics=semantics, vmem_limit_bytes=vmem)


def _ada_kernel(c_ref, w_ref, b_ref, o_ref):
    c = c_ref[...]
    act = (c * jax.nn.sigmoid(c)).astype(BF16)
    o_ref[...] = jnp.dot(act, w_ref[...].astype(BF16), preferred_element_type=F32) + b_ref[...]


def _ada_mod(c_pad, w_ada, b_ada):
    rows, d = c_pad.shape
    n = w_ada.shape[1]
    tn = _tile(n, 1024)
    return pl.pallas_call(
        _ada_kernel,
        out_shape=jax.ShapeDtypeStruct((rows, n), F32),
        grid=(n // tn,),
        in_specs=[
            pl.BlockSpec((rows, d), lambda j: (0, 0)),
            pl.BlockSpec((d, tn), lambda j: (0, j)),
            pl.BlockSpec((1, tn), lambda j: (0, j)),
        ],
        out_specs=pl.BlockSpec((rows, tn), lambda j: (0, j)),
        compiler_params=_cparams(("arbitrary",)),
        name="ada_mod",
    )(c_pad, w_ada, b_ada.reshape(1, n))


def _rms(x, g):
    y = x * lax.rsqrt(jnp.mean(x * x, axis=-1, keepdims=True) + NORM_EPS)
    return y * g


def _prenorm_kernel(x_ref, g_ref, scale_ref, shift_ref, h_ref):
    h = _rms(x_ref[0], g_ref[...]) * (1.0 + scale_ref[0]) + shift_ref[0]
    h_ref[0] = h.astype(h_ref.dtype)


def _mod_spec(d, k):
    return pl.BlockSpec((1, 1, d), lambda b, i: (b * N_MOD + k, 0, 0))


def _prenorm(x, g, mod3, k_shift, k_scale):
    b, s, d = x.shape
    ts = _tile(s, 256)
    row = pl.BlockSpec((1, ts, d), lambda bb, i: (bb, i, 0))
    return pl.pallas_call(
        _prenorm_kernel,
        out_shape=jax.ShapeDtypeStruct((b, s, d), BF16),
        grid=(b, s // ts),
        in_specs=[row, pl.BlockSpec((1, d), lambda bb, i: (0, 0)),
                  _mod_spec(d, k_scale), _mod_spec(d, k_shift)],
        out_specs=row,
        compiler_params=_cparams(("arbitrary", "arbitrary")),
        name="prenorm",
    )(x, g.reshape(1, d), mod3, mod3)


def _post_pre_kernel(y_ref, x_ref, gpost_ref, gate_ref, gpre_ref, scale_ref, shift_ref,
                     x1_ref, h_ref):
    x1 = x_ref[0] + gate_ref[0] * _rms(y_ref[0], gpost_ref[...])
    x1_ref[0] = x1
    h = _rms(x1, gpre_ref[...]) * (1.0 + scale_ref[0]) + shift_ref[0]
    h_ref[0] = h.astype(h_ref.dtype)


def _post_pre(y, x, gpost, gpre, mod3, k_gate, k_shift, k_scale):
    b, s, d = x.shape
    ts = _tile(s, 256)
    row = pl.BlockSpec((1, ts, d), lambda bb, i: (bb, i, 0))
    vec = pl.BlockSpec((1, d), lambda bb, i: (0, 0))
    return pl.pallas_call(
        _post_pre_kernel,
        out_shape=(jax.ShapeDtypeStruct((b, s, d), F32), jax.ShapeDtypeStruct((b, s, d), BF16)),
        grid=(b, s // ts),
        in_specs=[row, row, vec, _mod_spec(d, k_gate), vec, _mod_spec(d, k_scale),
                  _mod_spec(d, k_shift)],
        out_specs=(row, row),
        compiler_params=_cparams(("arbitrary", "arbitrary")),
        name="post_pre_norm",
    )(y, x, gpost.reshape(1, d), mod3, gpre.reshape(1, d), mod3, mod3)


def _post_kernel(y_ref, x_ref, gpost_ref, gate_ref, o_ref):
    o_ref[0] = x_ref[0] + gate_ref[0] * _rms(y_ref[0], gpost_ref[...])


def _post(y, x, gpost, mod3, k_gate):
    b, s, d = x.shape
    ts = _tile(s, 256)
    row = pl.BlockSpec((1, ts, d), lambda bb, i: (bb, i, 0))
    return pl.pallas_call(
        _post_kernel,
        out_shape=jax.ShapeDtypeStruct((b, s, d), F32),
        grid=(b, s // ts),
        in_specs=[row, row, pl.BlockSpec((1, d), lambda bb, i: (0, 0)), _mod_spec(d, k_gate)],
        out_specs=row,
        compiler_params=_cparams(("arbitrary", "arbitrary")),
        name="post_norm",
    )(y, x, gpost.reshape(1, d), mod3)


def _in_proj_kernel(a_ref, w_ref, cos_ref, sin_ref, bias_ref, o_ref, *, groups):
    j = pl.program_id(1)
    acc = jnp.dot(a_ref[...], w_ref[...], preferred_element_type=F32)
    for lo, hi, kinds in groups:
        @pl.when(jnp.logical_and(j >= lo, j < hi))
        def _():
            for c, kind in enumerate(kinds):
                cols = slice(c * LANES, (c + 1) * LANES)
                blk = acc[:, cols]
                if kind == "rope":
                    blk = blk * cos_ref[...] + pltpu.roll(blk, HEAD_DIM // 2, 1) * sin_ref[...]
                elif kind == "gate":
                    blk = jax.nn.sigmoid(blk + bias_ref[:, cols])
                o_ref[:, cols] = blk.astype(o_ref.dtype)


def _in_proj(h, w_cat, cos_full, sin_signed, bias_cat, *, n_rope, n_plain, seq):
    t, d = h.shape
    n = w_cat.shape[1]
    tm = _tile(seq, 1024)
    tn = _tile(n, 1024)
    assert n_rope % LANES == 0 and n_plain % LANES == 0 and tn % LANES == 0
    chunks = tn // LANES

    def kind_of(col):
        return "rope" if col < n_rope else ("plain" if col < n_rope + n_plain else "gate")

    patterns = [tuple(kind_of(jj * tn + c * LANES) for c in range(chunks)) for jj in range(n // tn)]
    groups = []
    for jj, pat in enumerate(patterns):
        if groups and groups[-1][2] == pat:
            groups[-1] = (groups[-1][0], jj + 1, pat)
        else:
            groups.append((jj, jj + 1, pat))
    n_seq_tiles = seq // tm
    return pl.pallas_call(
        functools.partial(_in_proj_kernel, groups=tuple(groups)),
        out_shape=jax.ShapeDtypeStruct((t, n), BF16),
        grid=(t // tm, n // tn),
        in_specs=[
            pl.BlockSpec((tm, d), lambda i, j: (i, 0)),
            pl.BlockSpec((d, tn), lambda i, j: (0, j)),
            pl.BlockSpec((tm, HEAD_DIM), lambda i, j: (i % n_seq_tiles, 0)),
            pl.BlockSpec((tm, HEAD_DIM), lambda i, j: (i % n_seq_tiles, 0)),
            pl.BlockSpec((1, tn), lambda i, j: (0, j)),
        ],
        out_specs=pl.BlockSpec((tm, tn), lambda i, j: (i, j)),
        compiler_params=_cparams(("arbitrary", "arbitrary")),
        name="in_proj",
    )(h, w_cat, cos_full, sin_signed, bias_cat)


def _swa_kernel(sink_ref, q_ref, kp_ref, k_ref, vp_ref, v_ref, o_ref, *, tq):
    i = pl.program_id(1)
    kvh = pl.program_id(2)
    w = SWA_WINDOW
    rows = SWA_GROUP * w
    qi = lax.broadcasted_iota(jnp.int32, (rows, 2 * w), 0) % w
    kj = lax.broadcasted_iota(jnp.int32, (rows, 2 * w), 1)
    band = jnp.logical_and(kj > qi, kj <= qi + w)
    first = jnp.logical_and(band, jnp.logical_or(kj >= w, i > 0))
    sink = jnp.concatenate(
        [jnp.full((w, 1), sink_ref[kvh * SWA_GROUP + g], F32) for g in range(SWA_GROUP)], axis=0)
    scale = HEAD_DIM ** -0.5
    for s in range(tq // w):
        q = jnp.concatenate(
            [q_ref[s * w:(s + 1) * w, g * HEAD_DIM:(g + 1) * HEAD_DIM] for g in range(SWA_GROUP)],
            axis=0)
        if s == 0:
            k = jnp.concatenate([kp_ref[...], k_ref[0:w, :]], axis=0)
            v = jnp.concatenate([vp_ref[...], v_ref[0:w, :]], axis=0)
            valid = first
        else:
            k = k_ref[(s - 1) * w:(s + 1) * w, :]
            v = v_ref[(s - 1) * w:(s + 1) * w, :]
            valid = band
        sc = lax.dot_general(q, k, (((1,), (1,)), ((), ())), preferred_element_type=F32) * scale
        sc = jnp.where(valid, sc, -jnp.inf)
        m = jnp.maximum(jnp.max(sc, axis=-1, keepdims=True), sink)
        p = jnp.exp(sc - m)
        denom = jnp.sum(p, axis=-1, keepdims=True) + jnp.exp(sink - m)
        probs = (p / denom).astype(BF16)
        o = jnp.dot(probs, v, preferred_element_type=F32)
        for g in range(SWA_GROUP):
            o_ref[s * w:(s + 1) * w, g * HEAD_DIM:(g + 1) * HEAD_DIM] = (
                o[g * w:(g + 1) * w, :].astype(o_ref.dtype))


def _swa(proj, sinks, *, batch, seq, q_heads, k_col, v_col):
    kv_heads = q_heads // SWA_GROUP
    w = SWA_WINDOW
    tq = _tile(seq, 512)
    nq = seq // tq
    sub = tq // w
    gw = SWA_GROUP * HEAD_DIM
    kb, vb = k_col // HEAD_DIM, v_col // HEAD_DIM

    def prev_rows(b, i):
        return b * (seq // w) + jnp.maximum(i * sub - 1, 0)

    grid_spec = pltpu.PrefetchScalarGridSpec(
        num_scalar_prefetch=1,
        grid=(batch, nq, kv_heads),
        in_specs=[
            pl.BlockSpec((tq, gw), lambda b, i, h, s: (b * nq + i, h)),
            pl.BlockSpec((w, HEAD_DIM), lambda b, i, h, s: (prev_rows(b, i), kb + h)),
            pl.BlockSpec((tq, HEAD_DIM), lambda b, i, h, s: (b * nq + i, kb + h)),
            pl.BlockSpec((w, HEAD_DIM), lambda b, i, h, s: (prev_rows(b, i), vb + h)),
            pl.BlockSpec((tq, HEAD_DIM), lambda b, i, h, s: (b * nq + i, vb + h)),
        ],
        out_specs=pl.BlockSpec((tq, gw), lambda b, i, h, s: (b * nq + i, h)),
    )
    return pl.pallas_call(
        functools.partial(_swa_kernel, tq=tq),
        out_shape=jax.ShapeDtypeStruct((batch * seq, q_heads * HEAD_DIM), BF16),
        grid_spec=grid_spec,
        compiler_params=_cparams(("arbitrary", "arbitrary", "arbitrary")),
        name="swa_attention",
    )(sinks, proj, proj, proj, proj, proj)


def _sb_kernel(q_ref, k_ref, v_ref, o_ref, *, tq):
    i = pl.program_id(2)
    q = q_ref[...]
    row = lax.broadcasted_iota(jnp.int32, (tq, tq), 0)
    col = lax.broadcasted_iota(jnp.int32, (tq, tq), 1)
    strict_lower = col < row
    later = (row > col).astype(BF16)
    scale = HEAD_DIM ** -0.5

    def block(j, r_sum, acc, diag):
        start = pl.multiple_of(j * tq, tq)
        k = k_ref[pl.ds(start, tq), :]
        v = v_ref[pl.ds(start, tq), :]
        z = lax.dot_general(q, k, (((1,), (1,)), ((), ())), preferred_element_type=F32) * scale
        softplus = jnp.maximum(z, 0.0) + jnp.log(1.0 + jnp.exp(-jnp.abs(z)))
        log_beta = z - softplus
        log_rest = -softplus
        if diag:
            log_rest = jnp.where(strict_lower, log_rest, 0.0)
        hi = log_rest.astype(BF16)
        lo = (log_rest - hi.astype(F32)).astype(BF16)
        suffix = (jnp.dot(hi, later, preferred_element_type=F32)
                  + jnp.dot(lo, later, preferred_element_type=F32))
        att = jnp.exp(log_beta + suffix + r_sum)
        if diag:
            att = jnp.where(strict_lower, att, 0.0)
        acc = acc + jnp.dot(att.astype(BF16), v, preferred_element_type=F32)
        r_sum = r_sum + jnp.sum(log_rest, axis=-1, keepdims=True)
        return r_sum, acc

    r_sum, acc = block(i, jnp.zeros((tq, 1), F32), jnp.zeros((tq, HEAD_DIM), F32), True)

    def body(t, carry):
        return block(i - 1 - t, carry[0], carry[1], False)

    r_sum, acc = lax.fori_loop(0, i, body, (r_sum, acc))
    o_ref[...] = acc.astype(o_ref.dtype)


def _sb(proj, *, batch, seq, heads, q_col, k_col, v_col):
    tq = _tile(seq, 256)
    nq = seq // tq
    qb, kb, vb = q_col // HEAD_DIM, k_col // HEAD_DIM, v_col // HEAD_DIM
    return pl.pallas_call(
        functools.partial(_sb_kernel, tq=tq),
        out_shape=jax.ShapeDtypeStruct((batch * seq, heads * HEAD_DIM), BF16),
        grid=(batch, heads, nq),
        in_specs=[
            pl.BlockSpec((tq, HEAD_DIM), lambda b, h, i: (b * nq + i, qb + h)),
            pl.BlockSpec((seq, HEAD_DIM), lambda b, h, i: (b, kb + h)),
            pl.BlockSpec((seq, HEAD_DIM), lambda b, h, i: (b, vb + h)),
        ],
        out_specs=pl.BlockSpec((tq, HEAD_DIM), lambda b, h, i: (b * nq + i, h)),
        compiler_params=_cparams(("arbitrary", "arbitrary", "arbitrary")),
        name="sb_attention",
    )(proj, proj, proj)


def _merge_kernel(a1_ref, a2_ref, w1_ref, w2_ref, ga_ref, gb_ref, o_ref):
    y1 = jnp.dot(a1_ref[...], w1_ref[...], preferred_element_type=F32)
    y2 = jnp.dot(a2_ref[...], w2_ref[...], preferred_element_type=F32)
    o = ga_ref[...].astype(F32) * y1 + gb_ref[...].astype(F32) * y2
    o_ref[...] = o.astype(o_ref.dtype)


def _merge(a_swa, a_sb, w_swa, w_sb, proj, *, gate_col):
    t, k1 = a_swa.shape
    k2 = a_sb.shape[1]
    d = w_swa.shape[1]
    tm = _tile(t, 1024)
    tn = _tile(math.gcd(d, gate_col), 512)
    ga, gb = gate_col // tn, (gate_col + d) // tn
    return pl.pallas_call(
        _merge_kernel,
        out_shape=jax.ShapeDtypeStruct((t, d), BF16),
        grid=(t // tm, d // tn),
        in_specs=[
            pl.BlockSpec((tm, k1), lambda i, j: (i, 0)),
            pl.BlockSpec((tm, k2), lambda i, j: (i, 0)),
            pl.BlockSpec((k1, tn), lambda i, j: (0, j)),
            pl.BlockSpec((k2, tn), lambda i, j: (0, j)),
            pl.BlockSpec((tm, tn), lambda i, j: (i, ga + j)),
            pl.BlockSpec((tm, tn), lambda i, j: (i, gb + j)),
        ],
        out_specs=pl.BlockSpec((tm, tn), lambda i, j: (i, j)),
        compiler_params=_cparams(("arbitrary", "arbitrary")),
        name="branch_merge",
    )(a_swa, a_sb, w_swa, w_sb, proj, proj)


def _mm_kernel(a_ref, w_ref, o_ref, *, relu_sq):
    acc = jnp.dot(a_ref[...], w_ref[...], preferred_element_type=F32)
    if relu_sq:
        acc = jnp.square(jnp.maximum(acc, 0.0))
    o_ref[...] = acc.astype(o_ref.dtype)


def _mm(a, w, out_dtype, *, relu_sq=False, name):
    t, k = a.shape
    n = w.shape[1]
    tm = _tile(t, 1024)
    tn = _tile(n, 1024)
    return pl.pallas_call(
        functools.partial(_mm_kernel, relu_sq=relu_sq),
        out_shape=jax.ShapeDtypeStruct((t, n), out_dtype),
        grid=(t // tm, n // tn),
        in_specs=[pl.BlockSpec((tm, k), lambda i, j: (i, 0)),
                  pl.BlockSpec((k, tn), lambda i, j: (0, j))],
        out_specs=pl.BlockSpec((tm, tn), lambda i, j: (i, j)),
        compiler_params=_cparams(("arbitrary", "arbitrary")),
        name=name,
    )(a, w)


def _mm_ksplit_kernel(a_ref, w_ref, o_ref, acc_ref):
    kk = pl.program_id(2)
    part = jnp.dot(a_ref[...], w_ref[...], preferred_element_type=F32)

    @pl.when(kk == 0)
    def _():
        acc_ref[...] = part

    @pl.when(kk > 0)
    def _():
        acc_ref[...] += part

    @pl.when(kk == pl.num_programs(2) - 1)
    def _():
        o_ref[...] = acc_ref[...].astype(o_ref.dtype)


def _mm_ksplit(a, w, out_dtype, *, name):
    t, k = a.shape
    n = w.shape[1]
    tm = _tile(t, 1024)
    tn = _tile(n, 1024)
    tk = _tile(k, 4096)
    return pl.pallas_call(
        _mm_ksplit_kernel,
        out_shape=jax.ShapeDtypeStruct((t, n), out_dtype),
        grid=(t // tm, n // tn, k // tk),
        in_specs=[pl.BlockSpec((tm, tk), lambda i, j, kk: (i, kk)),
                  pl.BlockSpec((tk, tn), lambda i, j, kk: (kk, j))],
        out_specs=pl.BlockSpec((tm, tn), lambda i, j, kk: (i, j)),
        scratch_shapes=[pltpu.VMEM((tm, tn), F32)],
        compiler_params=_cparams(("arbitrary", "arbitrary", "arbitrary")),
        name=name,
    )(a, w)


def _rope_tables(seq):
    inv_freq = 1.0 / (ROPE_THETA ** (jnp.arange(0, HEAD_DIM, 2, dtype=F32) / HEAD_DIM))
    ang = jnp.arange(seq, dtype=F32)[:, None] * inv_freq[None, :]
    cos, sin = jnp.cos(ang), jnp.sin(ang)
    return jnp.concatenate([cos, cos], axis=-1), jnp.concatenate([-sin, sin], axis=-1)


def kernel(x, c, w_ada, b_ada, g_pre_mix, g_post_mix, g_pre_mlp, g_post_mlp, w_in, attn_sinks,
           w_branch_gate, b_branch_gate, w_proj_swa, w_proj_sb, w_out, w_ff_in, w_ff_out):
    batch, seq, d = x.shape
    depth = w_in.shape[0]
    d_in = w_in.shape[2]
    swa_q_heads = attn_sinks.shape[1]
    swa_q_w = w_proj_swa.shape[1]
    sb_w = w_proj_sb.shape[1]
    swa_kv_w = (d_in - swa_q_w - 3 * sb_w) // 2
    sb_heads = sb_w // HEAD_DIM
    assert swa_q_w == swa_q_heads * HEAD_DIM and swa_kv_w * SWA_GROUP == swa_q_w
    assert seq % SWA_WINDOW == 0
    t = batch * seq

    cos_full, sin_signed = _rope_tables(seq)
    pad_rows = (-batch) % 8
    c_pad = jnp.pad(c, ((0, pad_rows), (0, 0)))

    n_rope = swa_q_w + swa_kv_w
    n_plain = swa_kv_w + 3 * sb_w
    k_col = swa_q_w
    v_col = swa_q_w + swa_kv_w
    qb_col = v_col + swa_kv_w
    kb_col = qb_col + sb_w
    vb_col = kb_col + sb_w
    gate_col = d_in

    for l in range(depth):
        mod = _ada_mod(c_pad, w_ada[l], b_ada[l])[:batch]
        mod3 = mod.reshape(batch * N_MOD, 1, d)

        w_cat = jnp.concatenate([w_in[l], w_branch_gate[l]], axis=1).astype(BF16)
        bias_cat = jnp.concatenate([jnp.zeros((d_in,), F32), b_branch_gate[l]]).reshape(1, -1)

        h = _prenorm(x, g_pre_mix[l], mod3, 0, 1).reshape(t, d)
        proj = _in_proj(h, w_cat, cos_full, sin_signed, bias_cat,
                        n_rope=n_rope, n_plain=n_plain, seq=seq)
        a_swa = _swa(proj, attn_sinks[l], batch=batch, seq=seq, q_heads=swa_q_heads,
                     k_col=k_col, v_col=v_col)
        a_sb = _sb(proj, batch=batch, seq=seq, heads=sb_heads,
                   q_col=qb_col, k_col=kb_col, v_col=vb_col)
        merged = _merge(a_swa, a_sb, w_proj_swa[l].astype(BF16), w_proj_sb[l].astype(BF16), proj,
                        gate_col=gate_col)
        mixed = _mm(merged, w_out[l].astype(BF16), F32, name="out_proj").reshape(batch, seq, d)
        x1, h2 = _post_pre(mixed, x, g_post_mix[l], g_pre_mlp[l], mod3, 2, 3, 4)
        act = _mm(h2.reshape(t, d), w_ff_in[l].astype(BF16), BF16, relu_sq=True, name="ff_in")
        f = _mm_ksplit(act, w_ff_out[l].astype(BF16), F32, name="ff_out").reshape(batch, seq, d)
        x = _post(f, x1, g_post_mlp[l], mod3, 5)
    return x
```

```python
import functools
import math

import jax
import jax.numpy as jnp
from jax import lax
from jax.experimental import pallas as pl
from jax.experimental.pallas import tpu as pltpu

F32 = jnp.float32
BF16 = jnp.bfloat16

HEAD_DIM = 128
LANES = 128
SWA_WINDOW = 128
SWA_GROUP = 4
ROPE_THETA = 10000.0
NORM_EPS = 1e-6
LOG2E = 1.4426950408889634
SB_UNDERFLOW_LOG2 = -152.0
N_MOD = 6
MIB = 1024 * 1024
VMEM_LIMIT_BYTES = 56 * MIB


def _tile(dim, pref):
    t = min(dim, pref)
    while dim % t:
        t //= 2
    return t


def _cparams(semantics, vmem=VMEM_LIMIT_BYTES, flags=None):
    return pltpu.CompilerParams(dimension_semantics=semantics, vmem_limit_bytes=vmem, flags=flags)


def _ada_kernel(c_ref, w_ref, b_ref, o_ref):
    c = c_ref[...]
    act = (c * jax.nn.sigmoid(c)).astype(BF16)
    o_ref[...] = jnp.dot(act, w_ref[...].astype(BF16), preferred_element_type=F32) + b_ref[...]


def _ada_mod(c_pad, w_ada, b_ada):
    rows, d = c_pad.shape
    n = w_ada.shape[1]
    tn = _tile(n, 1024)
    return pl.pallas_call(
        _ada_kernel,
        out_shape=jax.ShapeDtypeStruct((rows, n), F32),
        grid=(n // tn,),
        in_specs=[
            pl.BlockSpec((rows, d), lambda j: (0, 0)),
            pl.BlockSpec((d, tn), lambda j: (0, j)),
            pl.BlockSpec((1, tn), lambda j: (0, j)),
        ],
        out_specs=pl.BlockSpec((rows, tn), lambda j: (0, j)),
        compiler_params=_cparams(("arbitrary",)),
        name="ada_mod",
    )(c_pad, w_ada, b_ada.reshape(1, n))


def _rms(x, g):
    y = x * lax.rsqrt(jnp.mean(x * x, axis=-1, keepdims=True) + NORM_EPS)
    return y * g


def _prenorm_kernel(x_ref, g_ref, scale_ref, shift_ref, h_ref):
    h = _rms(x_ref[0], g_ref[...]) * (1.0 + scale_ref[0]) + shift_ref[0]
    h_ref[0] = h.astype(h_ref.dtype)


def _mod_spec(d, k):
    return pl.BlockSpec((1, 1, d), lambda b, i: (b * N_MOD + k, 0, 0))


def _prenorm(x, g, mod3, k_shift, k_scale):
    b, s, d = x.shape
    ts = _tile(s, 256)
    row = pl.BlockSpec((1, ts, d), lambda bb, i: (bb, i, 0))
    return pl.pallas_call(
        _prenorm_kernel,
        out_shape=jax.ShapeDtypeStruct((b, s, d), BF16),
        grid=(b, s // ts),
        in_specs=[row, pl.BlockSpec((1, d), lambda bb, i: (0, 0)),
                  _mod_spec(d, k_scale), _mod_spec(d, k_shift)],
        out_specs=row,
        compiler_params=_cparams(("arbitrary", "arbitrary")),
        name="prenorm",
    )(x, g.reshape(1, d), mod3, mod3)


def _post_pre_kernel(y_ref, x_ref, gpost_ref, gate_ref, gpre_ref, scale_ref, shift_ref,
                     x1_ref, h_ref):
    x1 = x_ref[0] + gate_ref[0] * _rms(y_ref[0], gpost_ref[...])
    x1_ref[0] = x1
    h = _rms(x1, gpre_ref[...]) * (1.0 + scale_ref[0]) + shift_ref[0]
    h_ref[0] = h.astype(h_ref.dtype)


def _post_pre(y, x, gpost, gpre, mod3, k_gate, k_shift, k_scale):
    b, s, d = x.shape
    ts = _tile(s, 256)
    row = pl.BlockSpec((1, ts, d), lambda bb, i: (bb, i, 0))
    vec = pl.BlockSpec((1, d), lambda bb, i: (0, 0))
    return pl.pallas_call(
        _post_pre_kernel,
        out_shape=(jax.ShapeDtypeStruct((b, s, d), F32), jax.ShapeDtypeStruct((b, s, d), BF16)),
        grid=(b, s // ts),
        in_specs=[row, row, vec, _mod_spec(d, k_gate), vec, _mod_spec(d, k_scale),
                  _mod_spec(d, k_shift)],
        out_specs=(row, row),
        compiler_params=_cparams(("arbitrary", "arbitrary")),
        name="post_pre_norm",
    )(y, x, gpost.reshape(1, d), mod3, gpre.reshape(1, d), mod3, mod3)


def _post_kernel(y_ref, x_ref, gpost_ref, gate_ref, o_ref):
    o_ref[0] = x_ref[0] + gate_ref[0] * _rms(y_ref[0], gpost_ref[...])


def _post(y, x, gpost, mod3, k_gate):
    b, s, d = x.shape
    ts = _tile(s, 256)
    row = pl.BlockSpec((1, ts, d), lambda bb, i: (bb, i, 0))
    return pl.pallas_call(
        _post_kernel,
        out_shape=jax.ShapeDtypeStruct((b, s, d), F32),
        grid=(b, s // ts),
        in_specs=[row, row, pl.BlockSpec((1, d), lambda bb, i: (0, 0)), _mod_spec(d, k_gate)],
        out_specs=row,
        compiler_params=_cparams(("arbitrary", "arbitrary")),
        name="post_norm",
    )(y, x, gpost.reshape(1, d), mod3)


def _in_proj_kernel(a_ref, w_ref, cos_ref, sin_ref, bias_ref, o_ref, *, groups):
    j = pl.program_id(1)
    acc = jnp.dot(a_ref[...], w_ref[...], preferred_element_type=F32)
    for lo, hi, kinds in groups:
        @pl.when(jnp.logical_and(j >= lo, j < hi))
        def _():
            for c, kind in enumerate(kinds):
                cols = slice(c * LANES, (c + 1) * LANES)
                blk = acc[:, cols]
                if kind == "rope":
                    blk = blk * cos_ref[...] + pltpu.roll(blk, HEAD_DIM // 2, 1) * sin_ref[...]
                elif kind == "gate":
                    blk = jax.nn.sigmoid(blk + bias_ref[:, cols])
                o_ref[:, cols] = blk.astype(o_ref.dtype)


def _in_proj(h, w_cat, cos_full, sin_signed, bias_cat, *, n_rope, n_plain, seq):
    t, d = h.shape
    n = w_cat.shape[1]
    tm = _tile(seq, 1024)
    tn = _tile(n, 1024)
    assert n_rope % LANES == 0 and n_plain % LANES == 0 and tn % LANES == 0
    chunks = tn // LANES

    def kind_of(col):
        return "rope" if col < n_rope else ("plain" if col < n_rope + n_plain else "gate")

    patterns = [tuple(kind_of(jj * tn + c * LANES) for c in range(chunks)) for jj in range(n // tn)]
    groups = []
    for jj, pat in enumerate(patterns):
        if groups and groups[-1][2] == pat:
            groups[-1] = (groups[-1][0], jj + 1, pat)
        else:
            groups.append((jj, jj + 1, pat))
    n_seq_tiles = seq // tm
    return pl.pallas_call(
        functools.partial(_in_proj_kernel, groups=tuple(groups)),
        out_shape=jax.ShapeDtypeStruct((t, n), BF16),
        grid=(t // tm, n // tn),
        in_specs=[
            pl.BlockSpec((tm, d), lambda i, j: (i, 0)),
            pl.BlockSpec((d, tn), lambda i, j: (0, j)),
            pl.BlockSpec((tm, HEAD_DIM), lambda i, j: (i % n_seq_tiles, 0)),
            pl.BlockSpec((tm, HEAD_DIM), lambda i, j: (i % n_seq_tiles, 0)),
            pl.BlockSpec((1, tn), lambda i, j: (0, j)),
        ],
        out_specs=pl.BlockSpec((tm, tn), lambda i, j: (i, j)),
        compiler_params=_cparams(("arbitrary", "arbitrary")),
        name="in_proj",
    )(h, w_cat, cos_full, sin_signed, bias_cat)


def _swa_kernel(sink_ref, q_ref, kp_ref, k_ref, vp_ref, v_ref, o_ref, *, tq):
    i = pl.program_id(1)
    kvh = pl.program_id(2)
    w = SWA_WINDOW
    rows = SWA_GROUP * w
    qi = lax.broadcasted_iota(jnp.int32, (rows, 2 * w), 0) % w
    kj = lax.broadcasted_iota(jnp.int32, (rows, 2 * w), 1)
    band = jnp.logical_and(kj > qi, kj <= qi + w)
    first = jnp.logical_and(band, jnp.logical_or(kj >= w, i > 0))
    sink = jnp.concatenate(
        [jnp.full((w, 1), sink_ref[kvh * SWA_GROUP + g], F32) for g in range(SWA_GROUP)], axis=0)
    scale = HEAD_DIM ** -0.5
    for s in range(tq // w):
        q = jnp.concatenate(
            [q_ref[s * w:(s + 1) * w, g * HEAD_DIM:(g + 1) * HEAD_DIM] for g in range(SWA_GROUP)],
            axis=0)
        if s == 0:
            k = jnp.concatenate([kp_ref[...], k_ref[0:w, :]], axis=0)
            v = jnp.concatenate([vp_ref[...], v_ref[0:w, :]], axis=0)
            valid = first
        else:
            k = k_ref[(s - 1) * w:(s + 1) * w, :]
            v = v_ref[(s - 1) * w:(s + 1) * w, :]
            valid = band
        sc = lax.dot_general(q, k, (((1,), (1,)), ((), ())), preferred_element_type=F32) * scale
        sc = jnp.where(valid, sc, -jnp.inf)
        m = jnp.maximum(jnp.max(sc, axis=-1, keepdims=True), sink)
        p = jnp.exp(sc - m)
        denom = jnp.sum(p, axis=-1, keepdims=True) + jnp.exp(sink - m)
        probs = (p / denom).astype(BF16)
        o = jnp.dot(probs, v, preferred_element_type=F32)
        for g in range(SWA_GROUP):
            o_ref[s * w:(s + 1) * w, g * HEAD_DIM:(g + 1) * HEAD_DIM] = (
                o[g * w:(g + 1) * w, :].astype(o_ref.dtype))


def _swa(proj, sinks, *, batch, seq, q_heads, k_col, v_col):
    kv_heads = q_heads // SWA_GROUP
    w = SWA_WINDOW
    tq = _tile(seq, 512)
    nq = seq // tq
    sub = tq // w
    gw = SWA_GROUP * HEAD_DIM
    kb, vb = k_col // HEAD_DIM, v_col // HEAD_DIM

    def prev_rows(b, i):
        return b * (seq // w) + jnp.maximum(i * sub - 1, 0)

    grid_spec = pltpu.PrefetchScalarGridSpec(
        num_scalar_prefetch=1,
        grid=(batch, nq, kv_heads),
        in_specs=[
            pl.BlockSpec((tq, gw), lambda b, i, h, s: (b * nq + i, h)),
            pl.BlockSpec((w, HEAD_DIM), lambda b, i, h, s: (prev_rows(b, i), kb + h)),
            pl.BlockSpec((tq, HEAD_DIM), lambda b, i, h, s: (b * nq + i, kb + h)),
            pl.BlockSpec((w, HEAD_DIM), lambda b, i, h, s: (prev_rows(b, i), vb + h)),
            pl.BlockSpec((tq, HEAD_DIM), lambda b, i, h, s: (b * nq + i, vb + h)),
        ],
        out_specs=pl.BlockSpec((tq, gw), lambda b, i, h, s: (b * nq + i, h)),
    )
    return pl.pallas_call(
        functools.partial(_swa_kernel, tq=tq),
        out_shape=jax.ShapeDtypeStruct((batch * seq, q_heads * HEAD_DIM), BF16),
        grid_spec=grid_spec,
        compiler_params=_cparams(("arbitrary", "arbitrary", "arbitrary")),
        name="swa_attention",
    )(sinks, proj, proj, proj, proj, proj)


def _sb_kernel(q_ref, k_ref, v_ref, o_ref, *, tq, hp):
    i = pl.program_id(2)
    row = lax.broadcasted_iota(jnp.int32, (tq, tq), 0)
    col = lax.broadcasted_iota(jnp.int32, (tq, tq), 1)
    strict_lower = col < row
    neg_later = jnp.where(row > col, -1.0, 0.0).astype(BF16)
    neg_later = jnp.concatenate([neg_later, neg_later], axis=0)
    z_scale = HEAD_DIM ** -0.5 * LOG2E
    sign_bit = jnp.uint32(0x80000000)

    def head_block(h, start, r_sum, acc, diag):
        cols = slice(h * HEAD_DIM, (h + 1) * HEAD_DIM)
        q = q_ref[:, cols]
        k = k_ref[pl.ds(start, tq), cols]
        v = v_ref[pl.ds(start, tq), cols]
        z = lax.dot_general(q, k, (((1,), (1,)), ((), ())), preferred_element_type=F32) * z_scale
        neg_abs = lax.bitcast_convert_type(lax.bitcast_convert_type(z, jnp.uint32) | sign_bit, F32)
        softplus = jnp.maximum(z, 0.0) + jnp.log(1.0 + jnp.exp2(neg_abs)) * LOG2E
        log_beta = z - softplus
        if diag:
            softplus = jnp.where(strict_lower, softplus, 0.0)
        hi = softplus.astype(BF16)
        lo = (softplus - hi.astype(F32)).astype(BF16)
        suffix = jnp.dot(jnp.concatenate([hi, lo], axis=1), neg_later, preferred_element_type=F32)
        att = jnp.exp2(log_beta + suffix + r_sum)
        if diag:
            att = jnp.where(strict_lower, att, 0.0)
        acc = acc + jnp.dot(att.astype(BF16), v, preferred_element_type=F32)
        r_sum = r_sum - jnp.sum(softplus, axis=-1, keepdims=True)
        return r_sum, acc

    def block(j, carry, diag):
        start = pl.multiple_of(j * tq, tq)
        return tuple(head_block(h, start, carry[h][0], carry[h][1], diag) for h in range(hp))

    def any_row_alive(carry):
        top = functools.reduce(jnp.maximum, [jnp.max(c[0], axis=0, keepdims=True) for c in carry])
        return top[0, 0] > SB_UNDERFLOW_LOG2

    def cond(state):
        t, alive, _ = state
        return jnp.logical_and(t < i, alive)

    def body(state):
        t, _, carry = state
        carry = block(i - 1 - t, carry, False)
        return t + 1, any_row_alive(carry), carry

    init = tuple((jnp.zeros((tq, 1), F32), jnp.zeros((tq, HEAD_DIM), F32)) for _ in range(hp))
    carry = block(i, init, True)
    _, _, carry = lax.while_loop(cond, body, (jnp.int32(0), any_row_alive(carry), carry))
    for h in range(hp):
        o_ref[:, h * HEAD_DIM:(h + 1) * HEAD_DIM] = carry[h][1].astype(o_ref.dtype)


def _sb(proj, *, batch, seq, heads, q_col, k_col, v_col):
    tq = _tile(seq, 256)
    nq = seq // tq
    hp = 2 if heads % 2 == 0 else 1
    hw = hp * HEAD_DIM
    assert q_col % hw == 0 and k_col % hw == 0 and v_col % hw == 0
    qb, kb, vb = q_col // hw, k_col // hw, v_col // hw
    return pl.pallas_call(
        functools.partial(_sb_kernel, tq=tq, hp=hp),
        out_shape=jax.ShapeDtypeStruct((batch * seq, heads * HEAD_DIM), BF16),
        grid=(batch, heads // hp, nq),
        in_specs=[
            pl.BlockSpec((tq, hw), lambda b, h, i: (b * nq + i, qb + h)),
            pl.BlockSpec((seq, hw), lambda b, h, i: (b, kb + h)),
            pl.BlockSpec((seq, hw), lambda b, h, i: (b, vb + h)),
        ],
        out_specs=pl.BlockSpec((tq, hw), lambda b, h, i: (b * nq + i, h)),
        compiler_params=_cparams(("arbitrary", "arbitrary", "arbitrary")),
        name="sb_attention",
    )(proj, proj, proj)


def _merge_kernel(a1_ref, a2_ref, w1_ref, w2_ref, ga_ref, gb_ref, o_ref):
    y1 = jnp.dot(a1_ref[...], w1_ref[...], preferred_element_type=F32)
    y2 = jnp.dot(a2_ref[...], w2_ref[...], preferred_element_type=F32)
    o = ga_ref[...].astype(F32) * y1 + gb_ref[...].astype(F32) * y2
    o_ref[...] = o.astype(o_ref.dtype)


def _merge(a_swa, a_sb, w_swa, w_sb, proj, *, gate_col):
    t, k1 = a_swa.shape
    k2 = a_sb.shape[1]
    d = w_swa.shape[1]
    tm = _tile(t, 1024)
    tn = _tile(math.gcd(d, gate_col), 512)
    ga, gb = gate_col // tn, (gate_col + d) // tn
    return pl.pallas_call(
        _merge_kernel,
        out_shape=jax.ShapeDtypeStruct((t, d), BF16),
        grid=(t // tm, d // tn),
        in_specs=[
            pl.BlockSpec((tm, k1), lambda i, j: (i, 0)),
            pl.BlockSpec((tm, k2), lambda i, j: (i, 0)),
            pl.BlockSpec((k1, tn), lambda i, j: (0, j)),
            pl.BlockSpec((k2, tn), lambda i, j: (0, j)),
            pl.BlockSpec((tm, tn), lambda i, j: (i, ga + j)),
            pl.BlockSpec((tm, tn), lambda i, j: (i, gb + j)),
        ],
        out_specs=pl.BlockSpec((tm, tn), lambda i, j: (i, j)),
        compiler_params=_cparams(("arbitrary", "arbitrary")),
        name="branch_merge",
    )(a_swa, a_sb, w_swa, w_sb, proj, proj)


def _mm_kernel(a_ref, w_ref, o_ref, *, relu_sq):
    acc = jnp.dot(a_ref[...], w_ref[...], preferred_element_type=F32)
    if relu_sq:
        acc = jnp.square(jnp.maximum(acc, 0.0))
    o_ref[...] = acc.astype(o_ref.dtype)


def _mm(a, w, out_dtype, *, relu_sq=False, name):
    t, k = a.shape
    n = w.shape[1]
    tm = _tile(t, 1024)
    tn = _tile(n, 1024)
    return pl.pallas_call(
        functools.partial(_mm_kernel, relu_sq=relu_sq),
        out_shape=jax.ShapeDtypeStruct((t, n), out_dtype),
        grid=(t // tm, n // tn),
        in_specs=[pl.BlockSpec((tm, k), lambda i, j: (i, 0)),
                  pl.BlockSpec((k, tn), lambda i, j: (0, j))],
        out_specs=pl.BlockSpec((tm, tn), lambda i, j: (i, j)),
        compiler_params=_cparams(("arbitrary", "arbitrary")),
        name=name,
    )(a, w)


def _mm_ksplit_kernel(a_ref, w_ref, o_ref, acc_ref):
    kk = pl.program_id(2)
    part = jnp.dot(a_ref[...], w_ref[...], preferred_element_type=F32)

    @pl.when(kk == 0)
    def _():
        acc_ref[...] = part

    @pl.when(kk > 0)
    def _():
        acc_ref[...] += part

    @pl.when(kk == pl.num_programs(2) - 1)
    def _():
        o_ref[...] = acc_ref[...].astype(o_ref.dtype)


def _mm_ksplit(a, w, out_dtype, *, name):
    t, k = a.shape
    n = w.shape[1]
    tm = _tile(t, 1024)
    tn = _tile(n, 1024)
    tk = _tile(k, 4096)
    return pl.pallas_call(
        _mm_ksplit_kernel,
        out_shape=jax.ShapeDtypeStruct((t, n), out_dtype),
        grid=(t // tm, n // tn, k // tk),
        in_specs=[pl.BlockSpec((tm, tk), lambda i, j, kk: (i, kk)),
                  pl.BlockSpec((tk, tn), lambda i, j, kk: (kk, j))],
        out_specs=pl.BlockSpec((tm, tn), lambda i, j, kk: (i, j)),
        scratch_shapes=[pltpu.VMEM((tm, tn), F32)],
        compiler_params=_cparams(("arbitrary", "arbitrary", "arbitrary")),
        name=name,
    )(a, w)


def _rope_tables(seq):
    inv_freq = 1.0 / (ROPE_THETA ** (jnp.arange(0, HEAD_DIM, 2, dtype=F32) / HEAD_DIM))
    ang = jnp.arange(seq, dtype=F32)[:, None] * inv_freq[None, :]
    cos, sin = jnp.cos(ang), jnp.sin(ang)
    return jnp.concatenate([cos, cos], axis=-1), jnp.concatenate([-sin, sin], axis=-1)


def kernel(x, c, w_ada, b_ada, g_pre_mix, g_post_mix, g_pre_mlp, g_post_mlp, w_in, attn_sinks,
           w_branch_gate, b_branch_gate, w_proj_swa, w_proj_sb, w_out, w_ff_in, w_ff_out):
    batch, seq, d = x.shape
    depth = w_in.shape[0]
    d_in = w_in.shape[2]
    swa_q_heads = attn_sinks.shape[1]
    swa_q_w = w_proj_swa.shape[1]
    sb_w = w_proj_sb.shape[1]
    swa_kv_w = (d_in - swa_q_w - 3 * sb_w) // 2
    sb_heads = sb_w // HEAD_DIM
    assert swa_q_w == swa_q_heads * HEAD_DIM and swa_kv_w * SWA_GROUP == swa_q_w
    assert seq % SWA_WINDOW == 0
    t = batch * seq

    cos_full, sin_signed = _rope_tables(seq)
    pad_rows = (-batch) % 8
    c_pad = jnp.pad(c, ((0, pad_rows), (0, 0)))

    n_rope = swa_q_w + swa_kv_w
    n_plain = swa_kv_w + 3 * sb_w
    k_col = swa_q_w
    v_col = swa_q_w + swa_kv_w
    qb_col = v_col + swa_kv_w
    kb_col = qb_col + sb_w
    vb_col = kb_col + sb_w
    gate_col = d_in

    for l in range(depth):
        mod = _ada_mod(c_pad, w_ada[l], b_ada[l])[:batch]
        mod3 = mod.reshape(batch * N_MOD, 1, d)

        w_cat = jnp.concatenate([w_in[l], w_branch_gate[l]], axis=1).astype(BF16)
        bias_cat = jnp.concatenate([jnp.zeros((d_in,), F32), b_branch_gate[l]]).reshape(1, -1)

        h = _prenorm(x, g_pre_mix[l], mod3, 0, 1).reshape(t, d)
        proj = _in_proj(h, w_cat, cos_full, sin_signed, bias_cat,
                        n_rope=n_rope, n_plain=n_plain, seq=seq)
        a_swa = _swa(proj, attn_sinks[l], batch=batch, seq=seq, q_heads=swa_q_heads,
                     k_col=k_col, v_col=v_col)
        a_sb = _sb(proj, batch=batch, seq=seq, heads=sb_heads,
                   q_col=qb_col, k_col=kb_col, v_col=vb_col)
        merged = _merge(a_swa, a_sb, w_proj_swa[l].astype(BF16), w_proj_sb[l].astype(BF16), proj,
                        gate_col=gate_col)
        mixed = _mm(merged, w_out[l].astype(BF16), F32, name="out_proj").reshape(batch, seq, d)
        x1, h2 = _post_pre(mixed, x, g_post_mix[l], g_pre_mlp[l], mod3, 2, 3, 4)
        act = _mm(h2.reshape(t, d), w_ff_in[l].astype(BF16), BF16, relu_sq=True, name="ff_in")
        f = _mm_ksplit(act, w_ff_out[l].astype(BF16), F32, name="ff_out").reshape(batch, seq, d)
        x = _post(f, x1, g_post_mlp[l], mod3, 5)
    return x
```

```python
import functools
import math

import jax
import jax.numpy as jnp
from jax import lax
from jax.experimental import pallas as pl
from jax.experimental.pallas import tpu as pltpu

F32 = jnp.float32
BF16 = jnp.bfloat16

HEAD_DIM = 128
LANES = 128
SWA_WINDOW = 128
SWA_GROUP = 4
ROPE_THETA = 10000.0
NORM_EPS = 1e-6
LOG2E = 1.4426950408889634
SB_UNDERFLOW_LOG2 = -152.0
N_MOD = 6
MIB = 1024 * 1024
VMEM_LIMIT_BYTES = 56 * MIB


def _tile(dim, pref):
    t = min(dim, pref)
    while dim % t:
        t //= 2
    return t


def _cparams(semantics, vmem=VMEM_LIMIT_BYTES, flags=None):
    return pltpu.CompilerParams(dimension_semantics=semantics, vmem_limit_bytes=vmem, flags=flags)


def _ada_kernel(c_ref, w_ref, b_ref, o_ref):
    c = c_ref[...]
    act = (c * jax.nn.sigmoid(c)).astype(BF16)
    o_ref[...] = jnp.dot(act, w_ref[...].astype(BF16), preferred_element_type=F32) + b_ref[...]


def _ada_mod(c_pad, w_ada, b_ada):
    rows, d = c_pad.shape
    n = w_ada.shape[1]
    tn = _tile(n, 1024)
    return pl.pallas_call(
        _ada_kernel,
        out_shape=jax.ShapeDtypeStruct((rows, n), F32),
        grid=(n // tn,),
        in_specs=[
            pl.BlockSpec((rows, d), lambda j: (0, 0)),
            pl.BlockSpec((d, tn), lambda j: (0, j)),
            pl.BlockSpec((1, tn), lambda j: (0, j)),
        ],
        out_specs=pl.BlockSpec((rows, tn), lambda j: (0, j)),
        compiler_params=_cparams(("arbitrary",)),
        name="ada_mod",
    )(c_pad, w_ada, b_ada.reshape(1, n))


def _rms(x, g):
    y = x * lax.rsqrt(jnp.mean(x * x, axis=-1, keepdims=True) + NORM_EPS)
    return y * g


def _prenorm_kernel(x_ref, g_ref, scale_ref, shift_ref, h_ref):
    h = _rms(x_ref[0], g_ref[...]) * (1.0 + scale_ref[0]) + shift_ref[0]
    h_ref[0] = h.astype(h_ref.dtype)


def _mod_spec(d, k):
    return pl.BlockSpec((1, 1, d), lambda b, i: (b * N_MOD + k, 0, 0))


def _prenorm(x, g, mod3, k_shift, k_scale):
    b, s, d = x.shape
    ts = _tile(s, 256)
    row = pl.BlockSpec((1, ts, d), lambda bb, i: (bb, i, 0))
    return pl.pallas_call(
        _prenorm_kernel,
        out_shape=jax.ShapeDtypeStruct((b, s, d), BF16),
        grid=(b, s // ts),
        in_specs=[row, pl.BlockSpec((1, d), lambda bb, i: (0, 0)),
                  _mod_spec(d, k_scale), _mod_spec(d, k_shift)],
        out_specs=row,
        compiler_params=_cparams(("arbitrary", "arbitrary")),
        name="prenorm",
    )(x, g.reshape(1, d), mod3, mod3)


def _post_pre_kernel(y_ref, x_ref, gpost_ref, gate_ref, gpre_ref, scale_ref, shift_ref,
                     x1_ref, h_ref):
    x1 = x_ref[0] + gate_ref[0] * _rms(y_ref[0], gpost_ref[...])
    x1_ref[0] = x1
    h = _rms(x1, gpre_ref[...]) * (1.0 + scale_ref[0]) + shift_ref[0]
    h_ref[0] = h.astype(h_ref.dtype)


def _post_pre(y, x, gpost, gpre, mod3, k_gate, k_shift, k_scale):
    b, s, d = x.shape
    ts = _tile(s, 256)
    row = pl.BlockSpec((1, ts, d), lambda bb, i: (bb, i, 0))
    vec = pl.BlockSpec((1, d), lambda bb, i: (0, 0))
    return pl.pallas_call(
        _post_pre_kernel,
        out_shape=(jax.ShapeDtypeStruct((b, s, d), F32), jax.ShapeDtypeStruct((b, s, d), BF16)),
        grid=(b, s // ts),
        in_specs=[row, row, vec, _mod_spec(d, k_gate), vec, _mod_spec(d, k_scale),
                  _mod_spec(d, k_shift)],
        out_specs=(row, row),
        compiler_params=_cparams(("arbitrary", "arbitrary")),
        name="post_pre_norm",
    )(y, x, gpost.reshape(1, d), mod3, gpre.reshape(1, d), mod3, mod3)


def _post_kernel(y_ref, x_ref, gpost_ref, gate_ref, o_ref):
    o_ref[0] = x_ref[0] + gate_ref[0] * _rms(y_ref[0], gpost_ref[...])


def _post(y, x, gpost, mod3, k_gate):
    b, s, d = x.shape
    ts = _tile(s, 256)
    row = pl.BlockSpec((1, ts, d), lambda bb, i: (bb, i, 0))
    return pl.pallas_call(
        _post_kernel,
        out_shape=jax.ShapeDtypeStruct((b, s, d), F32),
        grid=(b, s // ts),
        in_specs=[row, row, pl.BlockSpec((1, d), lambda bb, i: (0, 0)), _mod_spec(d, k_gate)],
        out_specs=row,
        compiler_params=_cparams(("arbitrary", "arbitrary")),
        name="post_norm",
    )(y, x, gpost.reshape(1, d), mod3)


def _rope_proj_kernel(a_ref, w_ref, cos_ref, sin_ref, o_ref, *, full_rope_tiles, partial_chunks):
    j = pl.program_id(1)
    acc = jnp.dot(a_ref[...], w_ref[...], preferred_element_type=F32)
    cos, sin = cos_ref[...], sin_ref[...]
    head = j <= full_rope_tiles
    tail = j < full_rope_tiles
    tables = {True: (jnp.where(head, cos, 1.0), jnp.where(head, sin, 0.0)),
              False: (jnp.where(tail, cos, 1.0), jnp.where(tail, sin, 0.0))}
    for c in range(acc.shape[1] // LANES):
        cols = slice(c * LANES, (c + 1) * LANES)
        cos_c, sin_c = tables[c < partial_chunks]
        blk = acc[:, cols]
        o_ref[:, cols] = (blk * cos_c + pltpu.roll(blk, HEAD_DIM // 2, 1) * sin_c).astype(o_ref.dtype)


def _rope_proj(h, w, cos_full, sin_signed, *, n_rope, seq):
    t, d = h.shape
    n = w.shape[1]
    tm = _tile(seq, 1024)
    tn = _tile(n, 1024)
    assert n_rope % LANES == 0 and tn % LANES == 0
    n_seq_tiles = seq // tm
    kern = functools.partial(_rope_proj_kernel, full_rope_tiles=n_rope // tn,
                             partial_chunks=(n_rope % tn) // LANES)
    return pl.pallas_call(
        kern,
        out_shape=jax.ShapeDtypeStruct((t, n), BF16),
        grid=(t // tm, n // tn),
        in_specs=[
            pl.BlockSpec((tm, d), lambda i, j: (i, 0)),
            pl.BlockSpec((d, tn), lambda i, j: (0, j)),
            pl.BlockSpec((tm, HEAD_DIM), lambda i, j: (i % n_seq_tiles, 0)),
            pl.BlockSpec((tm, HEAD_DIM), lambda i, j: (i % n_seq_tiles, 0)),
        ],
        out_specs=pl.BlockSpec((tm, tn), lambda i, j: (i, j)),
        compiler_params=_cparams(("arbitrary", "arbitrary")),
        name="swa_proj",
    )(h, w, cos_full, sin_signed)


def _gate_proj_kernel(a_ref, w_ref, b_ref, o_ref):
    acc = jnp.dot(a_ref[...], w_ref[...], preferred_element_type=F32)
    o_ref[...] = jax.nn.sigmoid(acc + b_ref[...]).astype(o_ref.dtype)


def _gate_proj(h, w, bias):
    t, d = h.shape
    n = w.shape[1]
    tm = _tile(t, 1024)
    tn = _tile(n, 1024)
    return pl.pallas_call(
        _gate_proj_kernel,
        out_shape=jax.ShapeDtypeStruct((t, n), BF16),
        grid=(t // tm, n // tn),
        in_specs=[pl.BlockSpec((tm, d), lambda i, j: (i, 0)),
                  pl.BlockSpec((d, tn), lambda i, j: (0, j)),
                  pl.BlockSpec((1, tn), lambda i, j: (0, j))],
        out_specs=pl.BlockSpec((tm, tn), lambda i, j: (i, j)),
        compiler_params=_cparams(("arbitrary", "arbitrary")),
        name="gate_proj",
    )(h, w, bias.reshape(1, n))


def _swa_kernel(sink_ref, q_ref, kp_ref, k_ref, vp_ref, v_ref, o_ref, *, tq):
    i = pl.program_id(1)
    kvh = pl.program_id(2)
    w = SWA_WINDOW
    rows = SWA_GROUP * w
    qi = lax.broadcasted_iota(jnp.int32, (rows, 2 * w), 0) % w
    kj = lax.broadcasted_iota(jnp.int32, (rows, 2 * w), 1)
    band = jnp.logical_and(kj > qi, kj <= qi + w)
    first = jnp.logical_and(band, jnp.logical_or(kj >= w, i > 0))
    sink = jnp.concatenate(
        [jnp.full((w, 1), sink_ref[kvh * SWA_GROUP + g], F32) for g in range(SWA_GROUP)], axis=0)
    scale = HEAD_DIM ** -0.5
    for s in range(tq // w):
        q = jnp.concatenate(
            [q_ref[s * w:(s + 1) * w, g * HEAD_DIM:(g + 1) * HEAD_DIM] for g in range(SWA_GROUP)],
            axis=0)
        if s == 0:
            k = jnp.concatenate([kp_ref[...], k_ref[0:w, :]], axis=0)
            v = jnp.concatenate([vp_ref[...], v_ref[0:w, :]], axis=0)
            valid = first
        else:
            k = k_ref[(s - 1) * w:(s + 1) * w, :]
            v = v_ref[(s - 1) * w:(s + 1) * w, :]
            valid = band
        sc = lax.dot_general(q, k, (((1,), (1,)), ((), ())), preferred_element_type=F32) * scale
        sc = jnp.where(valid, sc, -jnp.inf)
        m = jnp.maximum(jnp.max(sc, axis=-1, keepdims=True), sink)
        p = jnp.exp(sc - m)
        denom = jnp.sum(p, axis=-1, keepdims=True) + jnp.exp(sink - m)
        probs = (p / denom).astype(BF16)
        o = jnp.dot(probs, v, preferred_element_type=F32)
        for g in range(SWA_GROUP):
            o_ref[s * w:(s + 1) * w, g * HEAD_DIM:(g + 1) * HEAD_DIM] = (
                o[g * w:(g + 1) * w, :].astype(o_ref.dtype))


def _swa(proj, sinks, *, batch, seq, q_heads, k_col, v_col):
    kv_heads = q_heads // SWA_GROUP
    w = SWA_WINDOW
    tq = _tile(seq, 512)
    nq = seq // tq
    sub = tq // w
    gw = SWA_GROUP * HEAD_DIM
    kb, vb = k_col // HEAD_DIM, v_col // HEAD_DIM

    def prev_rows(b, i):
        return b * (seq // w) + jnp.maximum(i * sub - 1, 0)

    grid_spec = pltpu.PrefetchScalarGridSpec(
        num_scalar_prefetch=1,
        grid=(batch, nq, kv_heads),
        in_specs=[
            pl.BlockSpec((tq, gw), lambda b, i, h, s: (b * nq + i, h)),
            pl.BlockSpec((w, HEAD_DIM), lambda b, i, h, s: (prev_rows(b, i), kb + h)),
            pl.BlockSpec((tq, HEAD_DIM), lambda b, i, h, s: (b * nq + i, kb + h)),
            pl.BlockSpec((w, HEAD_DIM), lambda b, i, h, s: (prev_rows(b, i), vb + h)),
            pl.BlockSpec((tq, HEAD_DIM), lambda b, i, h, s: (b * nq + i, vb + h)),
        ],
        out_specs=pl.BlockSpec((tq, gw), lambda b, i, h, s: (b * nq + i, h)),
    )
    return pl.pallas_call(
        functools.partial(_swa_kernel, tq=tq),
        out_shape=jax.ShapeDtypeStruct((batch * seq, q_heads * HEAD_DIM), BF16),
        grid_spec=grid_spec,
        compiler_params=_cparams(("arbitrary", "arbitrary", "arbitrary")),
        name="swa_attention",
    )(sinks, proj, proj, proj, proj, proj)


def _sb_kernel(q_ref, k_ref, v_ref, o_ref, hilo_ref, lb_ref, *, tq, hp):
    i = pl.program_id(2)
    row = lax.broadcasted_iota(jnp.int32, (tq, tq), 0)
    col = lax.broadcasted_iota(jnp.int32, (tq, tq), 1)
    strict_lower = col < row
    neg_later = jnp.where(row > col, -1.0, 0.0).astype(BF16)
    neg_later = jnp.concatenate([neg_later, neg_later], axis=0)
    z_scale = HEAD_DIM ** -0.5 * LOG2E
    sign_bit = jnp.uint32(0x80000000)

    def logits_stage(h, start, r_sum, diag):
        cols = slice(h * HEAD_DIM, (h + 1) * HEAD_DIM)
        q = q_ref[:, cols]
        k = k_ref[pl.ds(start, tq), cols]
        z = lax.dot_general(q, k, (((1,), (1,)), ((), ())), preferred_element_type=F32) * z_scale
        neg_abs = lax.bitcast_convert_type(lax.bitcast_convert_type(z, jnp.uint32) | sign_bit, F32)
        softplus = jnp.maximum(z, 0.0) + jnp.log(1.0 + jnp.exp2(neg_abs)) * LOG2E
        lb_ref[h] = z - softplus + r_sum
        if diag:
            softplus = jnp.where(strict_lower, softplus, 0.0)
        hi = softplus.astype(BF16)
        lo = (softplus - hi.astype(F32)).astype(BF16)
        hilo_ref[h] = jnp.concatenate([hi, lo], axis=1)
        return r_sum - jnp.sum(softplus, axis=-1, keepdims=True)

    def weights_stage(h, start, acc, diag):
        cols = slice(h * HEAD_DIM, (h + 1) * HEAD_DIM)
        v = v_ref[pl.ds(start, tq), cols]
        suffix = jnp.dot(hilo_ref[h], neg_later, preferred_element_type=F32)
        att = jnp.exp2(lb_ref[h] + suffix)
        if diag:
            att = jnp.where(strict_lower, att, 0.0)
        return acc + jnp.dot(att.astype(BF16), v, preferred_element_type=F32)

    def block(j, carry, diag):
        start = pl.multiple_of(j * tq, tq)
        r_new = [logits_stage(h, start, carry[h][0], diag) for h in range(hp)]
        return tuple((r_new[h], weights_stage(h, start, carry[h][1], diag)) for h in range(hp))

    def any_row_alive(carry):
        top = functools.reduce(jnp.maximum, [jnp.max(c[0], axis=0, keepdims=True) for c in carry])
        return top[0, 0] > SB_UNDERFLOW_LOG2

    def cond(state):
        t, alive, _ = state
        return jnp.logical_and(t < i, alive)

    def body(state):
        t, _, carry = state
        carry = block(i - 1 - t, carry, False)
        return t + 1, any_row_alive(carry), carry

    init = tuple((jnp.zeros((tq, 1), F32), jnp.zeros((tq, HEAD_DIM), F32)) for _ in range(hp))
    carry = block(i, init, True)
    _, _, carry = lax.while_loop(cond, body, (jnp.int32(0), any_row_alive(carry), carry))
    for h in range(hp):
        o_ref[:, h * HEAD_DIM:(h + 1) * HEAD_DIM] = carry[h][1].astype(o_ref.dtype)


def _sb(proj, *, batch, seq, heads, q_col, k_col, v_col):
    tq = _tile(seq, 256)
    nq = seq // tq
    hp = 4 if heads % 4 == 0 else 1
    hw = hp * HEAD_DIM
    assert q_col % hw == 0 and k_col % hw == 0 and v_col % hw == 0
    qb, kb, vb = q_col // hw, k_col // hw, v_col // hw
    return pl.pallas_call(
        functools.partial(_sb_kernel, tq=tq, hp=hp),
        out_shape=jax.ShapeDtypeStruct((batch * seq, heads * HEAD_DIM), BF16),
        grid=(batch, heads // hp, nq),
        in_specs=[
            pl.BlockSpec((tq, hw), lambda b, h, i: (b * nq + i, qb + h)),
            pl.BlockSpec((seq, hw), lambda b, h, i: (b, kb + h)),
            pl.BlockSpec((seq, hw), lambda b, h, i: (b, vb + h)),
        ],
        out_specs=pl.BlockSpec((tq, hw), lambda b, h, i: (b * nq + i, h)),
        scratch_shapes=[pltpu.VMEM((hp, tq, 2 * tq), BF16), pltpu.VMEM((hp, tq, tq), F32)],
        compiler_params=_cparams(("arbitrary", "arbitrary", "arbitrary")),
        name="sb_attention",
    )(proj, proj, proj)


def _merge_kernel(a1_ref, a2_ref, w1_ref, w2_ref, ga_ref, gb_ref, o_ref):
    y1 = jnp.dot(a1_ref[...], w1_ref[...], preferred_element_type=F32)
    y2 = jnp.dot(a2_ref[...], w2_ref[...], preferred_element_type=F32)
    o = ga_ref[...].astype(F32) * y1 + gb_ref[...].astype(F32) * y2
    o_ref[...] = o.astype(o_ref.dtype)


def _merge(a_swa, a_sb, w_swa, w_sb, gates):
    t, k1 = a_swa.shape
    k2 = a_sb.shape[1]
    d = w_swa.shape[1]
    tm = _tile(t, 1024)
    tn = _tile(d, 1024)
    ga, gb = 0, d // tn
    return pl.pallas_call(
        _merge_kernel,
        out_shape=jax.ShapeDtypeStruct((t, d), BF16),
        grid=(t // tm, d // tn),
        in_specs=[
            pl.BlockSpec((tm, k1), lambda i, j: (i, 0)),
            pl.BlockSpec((tm, k2), lambda i, j: (i, 0)),
            pl.BlockSpec((k1, tn), lambda i, j: (0, j)),
            pl.BlockSpec((k2, tn), lambda i, j: (0, j)),
            pl.BlockSpec((tm, tn), lambda i, j: (i, ga + j)),
            pl.BlockSpec((tm, tn), lambda i, j: (i, gb + j)),
        ],
        out_specs=pl.BlockSpec((tm, tn), lambda i, j: (i, j)),
        compiler_params=_cparams(("arbitrary", "arbitrary")),
        name="branch_merge",
    )(a_swa, a_sb, w_swa, w_sb, gates, gates)


def _mm_kernel(a_ref, w_ref, o_ref, *, relu_sq):
    acc = jnp.dot(a_ref[...], w_ref[...], preferred_element_type=F32)
    if relu_sq:
        acc = jnp.square(jnp.maximum(acc, 0.0))
    o_ref[...] = acc.astype(o_ref.dtype)


def _mm(a, w, out_dtype, *, relu_sq=False, name):
    t, k = a.shape
    n = w.shape[1]
    tm = _tile(t, 1024)
    tn = _tile(n, 1024)
    return pl.pallas_call(
        functools.partial(_mm_kernel, relu_sq=relu_sq),
        out_shape=jax.ShapeDtypeStruct((t, n), out_dtype),
        grid=(t // tm, n // tn),
        in_specs=[pl.BlockSpec((tm, k), lambda i, j: (i, 0)),
                  pl.BlockSpec((k, tn), lambda i, j: (0, j))],
        out_specs=pl.BlockSpec((tm, tn), lambda i, j: (i, j)),
        compiler_params=_cparams(("arbitrary", "arbitrary")),
        name=name,
    )(a, w)


def _mm_ksplit_kernel(a_ref, w_ref, o_ref):
    kk = pl.program_id(2)
    part = jnp.dot(a_ref[...], w_ref[...], preferred_element_type=F32)

    @pl.when(kk == 0)
    def _():
        o_ref[...] = part

    @pl.when(kk > 0)
    def _():
        o_ref[...] += part


def _mm_ksplit(a, w, *, name):
    t, k = a.shape
    n = w.shape[1]
    tm = _tile(t, 1024)
    tn = _tile(n, 1024)
    tk = _tile(k, 4096)
    return pl.pallas_call(
        _mm_ksplit_kernel,
        out_shape=jax.ShapeDtypeStruct((t, n), F32),
        grid=(t // tm, n // tn, k // tk),
        in_specs=[pl.BlockSpec((tm, tk), lambda i, j, kk: (i, kk)),
                  pl.BlockSpec((tk, tn), lambda i, j, kk: (kk, j))],
        out_specs=pl.BlockSpec((tm, tn), lambda i, j, kk: (i, j)),
        compiler_params=_cparams(("arbitrary", "arbitrary", "arbitrary")),
        name=name,
    )(a, w)


def _rope_tables(seq):
    inv_freq = 1.0 / (ROPE_THETA ** (jnp.arange(0, HEAD_DIM, 2, dtype=F32) / HEAD_DIM))
    ang = jnp.arange(seq, dtype=F32)[:, None] * inv_freq[None, :]
    cos, sin = jnp.cos(ang), jnp.sin(ang)
    return jnp.concatenate([cos, cos], axis=-1), jnp.concatenate([-sin, sin], axis=-1)


def kernel(x, c, w_ada, b_ada, g_pre_mix, g_post_mix, g_pre_mlp, g_post_mlp, w_in, attn_sinks,
           w_branch_gate, b_branch_gate, w_proj_swa, w_proj_sb, w_out, w_ff_in, w_ff_out):
    batch, seq, d = x.shape
    depth = w_in.shape[0]
    d_in = w_in.shape[2]
    swa_q_heads = attn_sinks.shape[1]
    swa_q_w = w_proj_swa.shape[1]
    sb_w = w_proj_sb.shape[1]
    swa_kv_w = (d_in - swa_q_w - 3 * sb_w) // 2
    sb_heads = sb_w // HEAD_DIM
    assert swa_q_w == swa_q_heads * HEAD_DIM and swa_kv_w * SWA_GROUP == swa_q_w
    assert seq % SWA_WINDOW == 0
    t = batch * seq

    cos_full, sin_signed = _rope_tables(seq)
    pad_rows = (-batch) % 8
    c_pad = jnp.pad(c, ((0, pad_rows), (0, 0)))

    swa_w = swa_q_w + 2 * swa_kv_w
    n_rope = swa_q_w + swa_kv_w

    for l in range(depth):
        mod = _ada_mod(c_pad, w_ada[l], b_ada[l])[:batch]
        mod3 = mod.reshape(batch * N_MOD, 1, d)

        h = _prenorm(x, g_pre_mix[l], mod3, 0, 1).reshape(t, d)
        proj_swa = _rope_proj(h, w_in[l][:, :swa_w].astype(BF16), cos_full, sin_signed,
                              n_rope=n_rope, seq=seq)
        proj_sb = _mm(h, w_in[l][:, swa_w:].astype(BF16), BF16, name="sb_proj")
        gates = _gate_proj(h, w_branch_gate[l].astype(BF16), b_branch_gate[l])
        a_swa = _swa(proj_swa, attn_sinks[l], batch=batch, seq=seq, q_heads=swa_q_heads,
                     k_col=swa_q_w, v_col=n_rope)
        a_sb = _sb(proj_sb, batch=batch, seq=seq, heads=sb_heads,
                   q_col=0, k_col=sb_w, v_col=2 * sb_w)
        merged = _merge(a_swa, a_sb, w_proj_swa[l].astype(BF16), w_proj_sb[l].astype(BF16), gates)
        mixed = _mm(merged, w_out[l].astype(BF16), F32, name="out_proj").reshape(batch, seq, d)
        x1, h2 = _post_pre(mixed, x, g_post_mix[l], g_pre_mlp[l], mod3, 2, 3, 4)
        act = _mm(h2.reshape(t, d), w_ff_in[l].astype(BF16), BF16, relu_sq=True, name="ff_in")
        f = _mm_ksplit(act, w_ff_out[l].astype(BF16), name="ff_out").reshape(batch, seq, d)
        x = _post(f, x1, g_post_mlp[l], mod3, 5)
    return x
```

```python
import functools
import math

import jax
import jax.numpy as jnp
from jax import lax
from jax.experimental import pallas as pl
from jax.experimental.pallas import tpu as pltpu

F32 = jnp.float32
BF16 = jnp.bfloat16

HEAD_DIM = 128
LANES = 128
SWA_WINDOW = 128
SWA_GROUP = 4
ROPE_THETA = 10000.0
NORM_EPS = 1e-6
LOG2E = 1.4426950408889634
SB_UNDERFLOW_LOG2 = -152.0
N_MOD = 6
MIB = 1024 * 1024
VMEM_LIMIT_BYTES = 56 * MIB


def _tile(dim, pref):
    t = min(dim, pref)
    while dim % t:
        t //= 2
    return t


def _cparams(semantics, vmem=VMEM_LIMIT_BYTES, flags=None):
    return pltpu.CompilerParams(dimension_semantics=semantics, vmem_limit_bytes=vmem, flags=flags)


def _ada_kernel(c_ref, w_ref, b_ref, o_ref):
    c = c_ref[...]
    act = (c * jax.nn.sigmoid(c)).astype(BF16)
    o_ref[...] = jnp.dot(act, w_ref[...].astype(BF16), preferred_element_type=F32) + b_ref[...]


def _ada_mod(c_pad, w_ada, b_ada):
    rows, d = c_pad.shape
    n = w_ada.shape[1]
    tn = _tile(n, 1024)
    return pl.pallas_call(
        _ada_kernel,
        out_shape=jax.ShapeDtypeStruct((rows, n), F32),
        grid=(n // tn,),
        in_specs=[
            pl.BlockSpec((rows, d), lambda j: (0, 0)),
            pl.BlockSpec((d, tn), lambda j: (0, j)),
            pl.BlockSpec((1, tn), lambda j: (0, j)),
        ],
        out_specs=pl.BlockSpec((rows, tn), lambda j: (0, j)),
        compiler_params=_cparams(("arbitrary",)),
        name="ada_mod",
    )(c_pad, w_ada, b_ada.reshape(1, n))


def _rms(x, g):
    y = x * lax.rsqrt(jnp.mean(x * x, axis=-1, keepdims=True) + NORM_EPS)
    return y * g


def _prenorm_kernel(x_ref, g_ref, scale_ref, shift_ref, h_ref):
    h = _rms(x_ref[0], g_ref[...]) * (1.0 + scale_ref[0]) + shift_ref[0]
    h_ref[0] = h.astype(h_ref.dtype)


def _mod_spec(d, k):
    return pl.BlockSpec((1, 1, d), lambda b, i: (b * N_MOD + k, 0, 0))


def _prenorm(x, g, mod3, k_shift, k_scale):
    b, s, d = x.shape
    ts = _tile(s, 256)
    row = pl.BlockSpec((1, ts, d), lambda bb, i: (bb, i, 0))
    return pl.pallas_call(
        _prenorm_kernel,
        out_shape=jax.ShapeDtypeStruct((b, s, d), BF16),
        grid=(b, s // ts),
        in_specs=[row, pl.BlockSpec((1, d), lambda bb, i: (0, 0)),
                  _mod_spec(d, k_scale), _mod_spec(d, k_shift)],
        out_specs=row,
        compiler_params=_cparams(("arbitrary", "arbitrary")),
        name="prenorm",
    )(x, g.reshape(1, d), mod3, mod3)


def _post_pre_kernel(y_ref, x_ref, gpost_ref, gate_ref, gpre_ref, scale_ref, shift_ref,
                     x1_ref, h_ref):
    x1 = x_ref[0] + gate_ref[0] * _rms(y_ref[0], gpost_ref[...])
    x1_ref[0] = x1
    h = _rms(x1, gpre_ref[...]) * (1.0 + scale_ref[0]) + shift_ref[0]
    h_ref[0] = h.astype(h_ref.dtype)


def _post_pre(y, x, gpost, gpre, mod3, k_gate, k_shift, k_scale):
    b, s, d = x.shape
    ts = _tile(s, 256)
    row = pl.BlockSpec((1, ts, d), lambda bb, i: (bb, i, 0))
    vec = pl.BlockSpec((1, d), lambda bb, i: (0, 0))
    return pl.pallas_call(
        _post_pre_kernel,
        out_shape=(jax.ShapeDtypeStruct((b, s, d), F32), jax.ShapeDtypeStruct((b, s, d), BF16)),
        grid=(b, s // ts),
        in_specs=[row, row, vec, _mod_spec(d, k_gate), vec, _mod_spec(d, k_scale),
                  _mod_spec(d, k_shift)],
        out_specs=(row, row),
        compiler_params=_cparams(("arbitrary", "arbitrary")),
        name="post_pre_norm",
    )(y, x, gpost.reshape(1, d), mod3, gpre.reshape(1, d), mod3, mod3)


def _post_kernel(y_ref, x_ref, gpost_ref, gate_ref, o_ref):
    o_ref[0] = x_ref[0] + gate_ref[0] * _rms(y_ref[0], gpost_ref[...])


def _post(y, x, gpost, mod3, k_gate):
    b, s, d = x.shape
    ts = _tile(s, 256)
    row = pl.BlockSpec((1, ts, d), lambda bb, i: (bb, i, 0))
    return pl.pallas_call(
        _post_kernel,
        out_shape=jax.ShapeDtypeStruct((b, s, d), F32),
        grid=(b, s // ts),
        in_specs=[row, row, pl.BlockSpec((1, d), lambda bb, i: (0, 0)), _mod_spec(d, k_gate)],
        out_specs=row,
        compiler_params=_cparams(("arbitrary", "arbitrary")),
        name="post_norm",
    )(y, x, gpost.reshape(1, d), mod3)


def _rope_proj_kernel(a_ref, w_ref, cos_ref, sin_ref, o_ref, *, full_rope_tiles, partial_chunks):
    j = pl.program_id(1)
    acc = jnp.dot(a_ref[...], w_ref[...], preferred_element_type=F32)
    cos, sin = cos_ref[...], sin_ref[...]
    head = j <= full_rope_tiles
    tail = j < full_rope_tiles
    tables = {True: (jnp.where(head, cos, 1.0), jnp.where(head, sin, 0.0)),
              False: (jnp.where(tail, cos, 1.0), jnp.where(tail, sin, 0.0))}
    for c in range(acc.shape[1] // LANES):
        cols = slice(c * LANES, (c + 1) * LANES)
        cos_c, sin_c = tables[c < partial_chunks]
        blk = acc[:, cols]
        o_ref[:, cols] = (blk * cos_c + pltpu.roll(blk, HEAD_DIM // 2, 1) * sin_c).astype(o_ref.dtype)


def _rope_proj(h, w, cos_full, sin_signed, *, n_rope, seq):
    t, d = h.shape
    n = w.shape[1]
    tm = _tile(seq, 1024)
    tn = _tile(n, 1024)
    assert n_rope % LANES == 0 and tn % LANES == 0
    n_seq_tiles = seq // tm
    kern = functools.partial(_rope_proj_kernel, full_rope_tiles=n_rope // tn,
                             partial_chunks=(n_rope % tn) // LANES)
    return pl.pallas_call(
        kern,
        out_shape=jax.ShapeDtypeStruct((t, n), BF16),
        grid=(t // tm, n // tn),
        in_specs=[
            pl.BlockSpec((tm, d), lambda i, j: (i, 0)),
            pl.BlockSpec((d, tn), lambda i, j: (0, j)),
            pl.BlockSpec((tm, HEAD_DIM), lambda i, j: (i % n_seq_tiles, 0)),
            pl.BlockSpec((tm, HEAD_DIM), lambda i, j: (i % n_seq_tiles, 0)),
        ],
        out_specs=pl.BlockSpec((tm, tn), lambda i, j: (i, j)),
        compiler_params=_cparams(("arbitrary", "arbitrary")),
        name="swa_proj",
    )(h, w, cos_full, sin_signed)


def _gate_proj_kernel(a_ref, w_ref, b_ref, o_ref):
    acc = jnp.dot(a_ref[...], w_ref[...], preferred_element_type=F32)
    o_ref[...] = jax.nn.sigmoid(acc + b_ref[...]).astype(o_ref.dtype)


def _gate_proj(h, w, bias):
    t, d = h.shape
    n = w.shape[1]
    tm = _tile(t, 1024)
    tn = _tile(n, 1024)
    return pl.pallas_call(
        _gate_proj_kernel,
        out_shape=jax.ShapeDtypeStruct((t, n), BF16),
        grid=(t // tm, n // tn),
        in_specs=[pl.BlockSpec((tm, d), lambda i, j: (i, 0)),
                  pl.BlockSpec((d, tn), lambda i, j: (0, j)),
                  pl.BlockSpec((1, tn), lambda i, j: (0, j))],
        out_specs=pl.BlockSpec((tm, tn), lambda i, j: (i, j)),
        compiler_params=_cparams(("arbitrary", "arbitrary")),
        name="gate_proj",
    )(h, w, bias.reshape(1, n))


def _swa_kernel(sink_ref, q_ref, kp_ref, k_ref, vp_ref, v_ref, o_ref, p_ref, *, tq):
    i = pl.program_id(1)
    kvh = pl.program_id(2)
    w = SWA_WINDOW
    rows = SWA_GROUP * w
    qi = lax.broadcasted_iota(jnp.int32, (rows, 2 * w), 0) % w
    kj = lax.broadcasted_iota(jnp.int32, (rows, 2 * w), 1)
    band = jnp.logical_and(kj > qi, kj <= qi + w)
    first = jnp.logical_and(band, jnp.logical_or(kj >= w, i > 0))
    sink_fill = jnp.concatenate(
        [jnp.full((w, 2 * w), sink_ref[kvh * SWA_GROUP + g], F32) for g in range(SWA_GROUP)], axis=0)
    masked_fill = jnp.where(kj == 0, sink_fill, -jnp.inf)
    key_row = lax.broadcasted_iota(jnp.int32, (2 * w, HEAD_DIM), 0)
    ones = jnp.ones((2 * w, HEAD_DIM), BF16)
    scale = HEAD_DIM ** -0.5
    n_sub = tq // w
    for s in range(n_sub):
        q = jnp.concatenate(
            [q_ref[s * w:(s + 1) * w, g * HEAD_DIM:(g + 1) * HEAD_DIM] for g in range(SWA_GROUP)],
            axis=0)
        if s == 0:
            k = jnp.concatenate([kp_ref[...], k_ref[0:w, :]], axis=0)
            valid = first
        else:
            k = k_ref[(s - 1) * w:(s + 1) * w, :]
            valid = band
        sc = lax.dot_general(q, k, (((1,), (1,)), ((), ())), preferred_element_type=F32) * scale
        sc = jnp.where(valid, sc, masked_fill)
        p_ref[s] = jnp.exp(sc - jnp.max(sc, axis=-1, keepdims=True)).astype(BF16)
    for s in range(n_sub):
        if s == 0:
            v = jnp.concatenate([vp_ref[...], v_ref[0:w, :]], axis=0)
        else:
            v = v_ref[(s - 1) * w:(s + 1) * w, :]
        v = jnp.where(key_row == 0, jnp.zeros_like(v), v)
        ov = jnp.dot(p_ref[s], jnp.concatenate([v, ones], axis=1), preferred_element_type=F32)
        o = ov[:, :HEAD_DIM] / ov[:, HEAD_DIM:]
        for g in range(SWA_GROUP):
            o_ref[s * w:(s + 1) * w, g * HEAD_DIM:(g + 1) * HEAD_DIM] = (
                o[g * w:(g + 1) * w, :].astype(o_ref.dtype))


def _swa(proj, sinks, *, batch, seq, q_heads, k_col, v_col):
    kv_heads = q_heads // SWA_GROUP
    w = SWA_WINDOW
    tq = _tile(seq, 512)
    nq = seq // tq
    sub = tq // w
    gw = SWA_GROUP * HEAD_DIM
    kb, vb = k_col // HEAD_DIM, v_col // HEAD_DIM

    def prev_rows(b, i):
        return b * (seq // w) + jnp.maximum(i * sub - 1, 0)

    grid_spec = pltpu.PrefetchScalarGridSpec(
        num_scalar_prefetch=1,
        grid=(batch, nq, kv_heads),
        in_specs=[
            pl.BlockSpec((tq, gw), lambda b, i, h, s: (b * nq + i, h)),
            pl.BlockSpec((w, HEAD_DIM), lambda b, i, h, s: (prev_rows(b, i), kb + h)),
            pl.BlockSpec((tq, HEAD_DIM), lambda b, i, h, s: (b * nq + i, kb + h)),
            pl.BlockSpec((w, HEAD_DIM), lambda b, i, h, s: (prev_rows(b, i), vb + h)),
            pl.BlockSpec((tq, HEAD_DIM), lambda b, i, h, s: (b * nq + i, vb + h)),
        ],
        out_specs=pl.BlockSpec((tq, gw), lambda b, i, h, s: (b * nq + i, h)),
        scratch_shapes=[pltpu.VMEM((sub, SWA_GROUP * w, 2 * w), BF16)],
    )
    return pl.pallas_call(
        functools.partial(_swa_kernel, tq=tq),
        out_shape=jax.ShapeDtypeStruct((batch * seq, q_heads * HEAD_DIM), BF16),
        grid_spec=grid_spec,
        compiler_params=_cparams(("arbitrary", "arbitrary", "arbitrary")),
        name="swa_attention",
    )(sinks, proj, proj, proj, proj, proj)


def _sb_kernel(q_ref, k_ref, v_ref, o_ref, hilo_ref, lb_ref, *, tq, hp):
    i = pl.program_id(2)
    row = lax.broadcasted_iota(jnp.int32, (tq, tq), 0)
    col = lax.broadcasted_iota(jnp.int32, (tq, tq), 1)
    strict_lower = col < row
    neg_later = jnp.where(row > col, -1.0, 0.0).astype(BF16)
    neg_later = jnp.concatenate([neg_later, neg_later], axis=0)
    z_scale = HEAD_DIM ** -0.5 * LOG2E
    sign_bit = jnp.uint32(0x80000000)

    def logits_stage(h, start, r_sum, diag):
        cols = slice(h * HEAD_DIM, (h + 1) * HEAD_DIM)
        q = q_ref[:, cols]
        k = k_ref[pl.ds(start, tq), cols]
        z = lax.dot_general(q, k, (((1,), (1,)), ((), ())), preferred_element_type=F32) * z_scale
        neg_abs = lax.bitcast_convert_type(lax.bitcast_convert_type(z, jnp.uint32) | sign_bit, F32)
        softplus = jnp.maximum(z, 0.0) + jnp.log(1.0 + jnp.exp2(neg_abs)) * LOG2E
        lb_ref[h] = z - softplus + r_sum
        if diag:
            softplus = jnp.where(strict_lower, softplus, 0.0)
        hi = softplus.astype(BF16)
        lo = (softplus - hi.astype(F32)).astype(BF16)
        hilo_ref[h] = jnp.concatenate([hi, lo], axis=1)
        return r_sum - jnp.sum(softplus, axis=-1, keepdims=True)

    def weights_stage(h, start, acc, diag):
        cols = slice(h * HEAD_DIM, (h + 1) * HEAD_DIM)
        v = v_ref[pl.ds(start, tq), cols]
        suffix = jnp.dot(hilo_ref[h], neg_later, preferred_element_type=F32)
        att = jnp.exp2(lb_ref[h] + suffix)
        if diag:
            att = jnp.where(strict_lower, att, 0.0)
        return acc + jnp.dot(att.astype(BF16), v, preferred_element_type=F32)

    def block(j, carry, diag):
        start = pl.multiple_of(j * tq, tq)
        r_new = [logits_stage(h, start, carry[h][0], diag) for h in range(hp)]
        return tuple((r_new[h], weights_stage(h, start, carry[h][1], diag)) for h in range(hp))

    def any_row_alive(carry):
        top = functools.reduce(jnp.maximum, [jnp.max(c[0], axis=0, keepdims=True) for c in carry])
        return top[0, 0] > SB_UNDERFLOW_LOG2

    def cond(state):
        t, alive, _ = state
        return jnp.logical_and(t < i, alive)

    def body(state):
        t, _, carry = state
        carry = block(i - 1 - t, carry, False)
        return t + 1, any_row_alive(carry), carry

    init = tuple((jnp.zeros((tq, 1), F32), jnp.zeros((tq, HEAD_DIM), F32)) for _ in range(hp))
    carry = block(i, init, True)
    _, _, carry = lax.while_loop(cond, body, (jnp.int32(0), any_row_alive(carry), carry))
    for h in range(hp):
        o_ref[:, h * HEAD_DIM:(h + 1) * HEAD_DIM] = carry[h][1].astype(o_ref.dtype)


def _sb(proj, *, batch, seq, heads, q_col, k_col, v_col):
    tq = _tile(seq, 256)
    nq = seq // tq
    hp = 4 if heads % 4 == 0 else 1
    hw = hp * HEAD_DIM
    assert q_col % hw == 0 and k_col % hw == 0 and v_col % hw == 0
    qb, kb, vb = q_col // hw, k_col // hw, v_col // hw
    return pl.pallas_call(
        functools.partial(_sb_kernel, tq=tq, hp=hp),
        out_shape=jax.ShapeDtypeStruct((batch * seq, heads * HEAD_DIM), BF16),
        grid=(batch, heads // hp, nq),
        in_specs=[
            pl.BlockSpec((tq, hw), lambda b, h, i: (b * nq + i, qb + h)),
            pl.BlockSpec((seq, hw), lambda b, h, i: (b, kb + h)),
            pl.BlockSpec((seq, hw), lambda b, h, i: (b, vb + h)),
        ],
        out_specs=pl.BlockSpec((tq, hw), lambda b, h, i: (b * nq + i, h)),
        scratch_shapes=[pltpu.VMEM((hp, tq, 2 * tq), BF16), pltpu.VMEM((hp, tq, tq), F32)],
        compiler_params=_cparams(("arbitrary", "arbitrary", "arbitrary")),
        name="sb_attention",
    )(proj, proj, proj)


def _merge_kernel(a1_ref, a2_ref, w1_ref, w2_ref, ga_ref, gb_ref, o_ref):
    y1 = jnp.dot(a1_ref[...], w1_ref[...], preferred_element_type=F32)
    y2 = jnp.dot(a2_ref[...], w2_ref[...], preferred_element_type=F32)
    o = ga_ref[...].astype(F32) * y1 + gb_ref[...].astype(F32) * y2
    o_ref[...] = o.astype(o_ref.dtype)


def _merge(a_swa, a_sb, w_swa, w_sb, gates):
    t, k1 = a_swa.shape
    k2 = a_sb.shape[1]
    d = w_swa.shape[1]
    tm = _tile(t, 1024)
    tn = _tile(d, 1024)
    ga, gb = 0, d // tn
    return pl.pallas_call(
        _merge_kernel,
        out_shape=jax.ShapeDtypeStruct((t, d), BF16),
        grid=(t // tm, d // tn),
        in_specs=[
            pl.BlockSpec((tm, k1), lambda i, j: (i, 0)),
            pl.BlockSpec((tm, k2), lambda i, j: (i, 0)),
            pl.BlockSpec((k1, tn), lambda i, j: (0, j)),
            pl.BlockSpec((k2, tn), lambda i, j: (0, j)),
            pl.BlockSpec((tm, tn), lambda i, j: (i, ga + j)),
            pl.BlockSpec((tm, tn), lambda i, j: (i, gb + j)),
        ],
        out_specs=pl.BlockSpec((tm, tn), lambda i, j: (i, j)),
        compiler_params=_cparams(("arbitrary", "arbitrary")),
        name="branch_merge",
    )(a_swa, a_sb, w_swa, w_sb, gates, gates)


def _mm_kernel(a_ref, w_ref, o_ref, *, relu_sq):
    acc = jnp.dot(a_ref[...], w_ref[...], preferred_element_type=F32)
    if relu_sq:
        acc = jnp.square(jnp.maximum(acc, 0.0))
    o_ref[...] = acc.astype(o_ref.dtype)


def _mm(a, w, out_dtype, *, relu_sq=False, name):
    t, k = a.shape
    n = w.shape[1]
    tm = _tile(t, 1024)
    tn = _tile(n, 1024)
    return pl.pallas_call(
        functools.partial(_mm_kernel, relu_sq=relu_sq),
        out_shape=jax.ShapeDtypeStruct((t, n), out_dtype),
        grid=(t // tm, n // tn),
        in_specs=[pl.BlockSpec((tm, k), lambda i, j: (i, 0)),
                  pl.BlockSpec((k, tn), lambda i, j: (0, j))],
        out_specs=pl.BlockSpec((tm, tn), lambda i, j: (i, j)),
        compiler_params=_cparams(("arbitrary", "arbitrary")),
        name=name,
    )(a, w)


def _mm_ksplit_kernel(a_ref, w_ref, o_ref):
    kk = pl.program_id(2)
    part = jnp.dot(a_ref[...], w_ref[...], preferred_element_type=F32)

    @pl.when(kk == 0)
    def _():
        o_ref[...] = part

    @pl.when(kk > 0)
    def _():
        o_ref[...] += part


def _mm_ksplit(a, w, *, name):
    t, k = a.shape
    n = w.shape[1]
    tm = _tile(t, 1024)
    tn = _tile(n, 1024)
    tk = _tile(k, 4096)
    return pl.pallas_call(
        _mm_ksplit_kernel,
        out_shape=jax.ShapeDtypeStruct((t, n), F32),
        grid=(t // tm, n // tn, k // tk),
        in_specs=[pl.BlockSpec((tm, tk), lambda i, j, kk: (i, kk)),
                  pl.BlockSpec((tk, tn), lambda i, j, kk: (kk, j))],
        out_specs=pl.BlockSpec((tm, tn), lambda i, j, kk: (i, j)),
        compiler_params=_cparams(("arbitrary", "arbitrary", "arbitrary")),
        name=name,
    )(a, w)


def _rope_tables(seq):
    inv_freq = 1.0 / (ROPE_THETA ** (jnp.arange(0, HEAD_DIM, 2, dtype=F32) / HEAD_DIM))
    ang = jnp.arange(seq, dtype=F32)[:, None] * inv_freq[None, :]
    cos, sin = jnp.cos(ang), jnp.sin(ang)
    return jnp.concatenate([cos, cos], axis=-1), jnp.concatenate([-sin, sin], axis=-1)


def kernel(x, c, w_ada, b_ada, g_pre_mix, g_post_mix, g_pre_mlp, g_post_mlp, w_in, attn_sinks,
           w_branch_gate, b_branch_gate, w_proj_swa, w_proj_sb, w_out, w_ff_in, w_ff_out):
    batch, seq, d = x.shape
    depth = w_in.shape[0]
    d_in = w_in.shape[2]
    swa_q_heads = attn_sinks.shape[1]
    swa_q_w = w_proj_swa.shape[1]
    sb_w = w_proj_sb.shape[1]
    swa_kv_w = (d_in - swa_q_w - 3 * sb_w) // 2
    sb_heads = sb_w // HEAD_DIM
    assert swa_q_w == swa_q_heads * HEAD_DIM and swa_kv_w * SWA_GROUP == swa_q_w
    assert seq % SWA_WINDOW == 0
    t = batch * seq

    cos_full, sin_signed = _rope_tables(seq)
    pad_rows = (-batch) % 8
    c_pad = jnp.pad(c, ((0, pad_rows), (0, 0)))

    swa_w = swa_q_w + 2 * swa_kv_w
    n_rope = swa_q_w + swa_kv_w

    for l in range(depth):
        mod = _ada_mod(c_pad, w_ada[l], b_ada[l])[:batch]
        mod3 = mod.reshape(batch * N_MOD, 1, d)

        h = _prenorm(x, g_pre_mix[l], mod3, 0, 1).reshape(t, d)
        proj_swa = _rope_proj(h, w_in[l][:, :swa_w].astype(BF16), cos_full, sin_signed,
                              n_rope=n_rope, seq=seq)
        proj_sb = _mm(h, w_in[l][:, swa_w:].astype(BF16), BF16, name="sb_proj")
        gates = _gate_proj(h, w_branch_gate[l].astype(BF16), b_branch_gate[l])
        a_swa = _swa(proj_swa, attn_sinks[l], batch=batch, seq=seq, q_heads=swa_q_heads,
                     k_col=swa_q_w, v_col=n_rope)
        a_sb = _sb(proj_sb, batch=batch, seq=seq, heads=sb_heads,
                   q_col=0, k_col=sb_w, v_col=2 * sb_w)
        merged = _merge(a_swa, a_sb, w_proj_swa[l].astype(BF16), w_proj_sb[l].astype(BF16), gates)
        mixed = _mm(merged, w_out[l].astype(BF16), F32, name="out_proj").reshape(batch, seq, d)
        x1, h2 = _post_pre(mixed, x, g_post_mix[l], g_pre_mlp[l], mod3, 2, 3, 4)
        act = _mm(h2.reshape(t, d), w_ff_in[l].astype(BF16), BF16, relu_sq=True, name="ff_in")
        f = _mm_ksplit(act, w_ff_out[l].astype(BF16), name="ff_out").reshape(batch, seq, d)
        x = _post(f, x1, g_post_mlp[l], mod3, 5)
    return x
```

```python
import functools
import math

import jax
import jax.numpy as jnp
from jax import lax
from jax.experimental import pallas as pl
from jax.experimental.pallas import tpu as pltpu

F32 = jnp.float32
BF16 = jnp.bfloat16

HEAD_DIM = 128
LANES = 128
BF16_SUBLANES = 16
SWA_WINDOW = 128
SWA_GROUP = 4
ROPE_THETA = 10000.0
NORM_EPS = 1e-6
LOG2E = 1.4426950408889634
SB_UNDERFLOW_LOG2 = -152.0
N_MOD = 6
MIB = 1024 * 1024
VMEM_LIMIT_BYTES = 56 * MIB


def _tile(dim, pref):
    t = min(dim, pref)
    while dim % t:
        t //= 2
    return t


def _cparams(semantics, vmem=VMEM_LIMIT_BYTES, flags=None):
    return pltpu.CompilerParams(dimension_semantics=semantics, vmem_limit_bytes=vmem, flags=flags)


def _ada_kernel(c_ref, w_ref, b_ref, o_ref):
    c = c_ref[...]
    act = (c * jax.nn.sigmoid(c)).astype(BF16)
    o_ref[...] = jnp.dot(act, w_ref[...].astype(BF16), preferred_element_type=F32) + b_ref[...]


def _ada_mod(c_pad, w_ada, b_ada):
    rows, d = c_pad.shape
    n = w_ada.shape[1]
    tn = _tile(n, 1024)
    return pl.pallas_call(
        _ada_kernel,
        out_shape=jax.ShapeDtypeStruct((rows, n), F32),
        grid=(n // tn,),
        in_specs=[
            pl.BlockSpec((rows, d), lambda j: (0, 0)),
            pl.BlockSpec((d, tn), lambda j: (0, j)),
            pl.BlockSpec((1, tn), lambda j: (0, j)),
        ],
        out_specs=pl.BlockSpec((rows, tn), lambda j: (0, j)),
        compiler_params=_cparams(("arbitrary",)),
        name="ada_mod",
    )(c_pad, w_ada, b_ada.reshape(1, n))


def _rms(x, g):
    y = x * lax.rsqrt(jnp.mean(x * x, axis=-1, keepdims=True) + NORM_EPS)
    return y * g


def _prenorm_kernel(x_ref, g_ref, scale_ref, shift_ref, h_ref):
    h = _rms(x_ref[0], g_ref[...]) * (1.0 + scale_ref[0]) + shift_ref[0]
    h_ref[0] = h.astype(h_ref.dtype)


def _mod_spec(d, k):
    return pl.BlockSpec((1, 1, d), lambda b, i: (b * N_MOD + k, 0, 0))


def _prenorm(x, g, mod3, k_shift, k_scale):
    b, s, d = x.shape
    ts = _tile(s, 256)
    row = pl.BlockSpec((1, ts, d), lambda bb, i: (bb, i, 0))
    return pl.pallas_call(
        _prenorm_kernel,
        out_shape=jax.ShapeDtypeStruct((b, s, d), BF16),
        grid=(b, s // ts),
        in_specs=[row, pl.BlockSpec((1, d), lambda bb, i: (0, 0)),
                  _mod_spec(d, k_scale), _mod_spec(d, k_shift)],
        out_specs=row,
        compiler_params=_cparams(("arbitrary", "arbitrary")),
        name="prenorm",
    )(x, g.reshape(1, d), mod3, mod3)


def _post_pre_kernel(y_ref, x_ref, gpost_ref, gate_ref, gpre_ref, scale_ref, shift_ref,
                     x1_ref, h_ref):
    x1 = x_ref[0] + gate_ref[0] * _rms(y_ref[0], gpost_ref[...])
    x1_ref[0] = x1
    h = _rms(x1, gpre_ref[...]) * (1.0 + scale_ref[0]) + shift_ref[0]
    h_ref[0] = h.astype(h_ref.dtype)


def _post_pre(y, x, gpost, gpre, mod3, k_gate, k_shift, k_scale):
    b, s, d = x.shape
    ts = _tile(s, 256)
    row = pl.BlockSpec((1, ts, d), lambda bb, i: (bb, i, 0))
    vec = pl.BlockSpec((1, d), lambda bb, i: (0, 0))
    return pl.pallas_call(
        _post_pre_kernel,
        out_shape=(jax.ShapeDtypeStruct((b, s, d), F32), jax.ShapeDtypeStruct((b, s, d), BF16)),
        grid=(b, s // ts),
        in_specs=[row, row, vec, _mod_spec(d, k_gate), vec, _mod_spec(d, k_scale),
                  _mod_spec(d, k_shift)],
        out_specs=(row, row),
        compiler_params=_cparams(("arbitrary", "arbitrary")),
        name="post_pre_norm",
    )(y, x, gpost.reshape(1, d), mod3, gpre.reshape(1, d), mod3, mod3)


def _post_kernel(y_ref, x_ref, gpost_ref, gate_ref, o_ref):
    o_ref[0] = x_ref[0] + gate_ref[0] * _rms(y_ref[0], gpost_ref[...])


def _post(y, x, gpost, mod3, k_gate):
    b, s, d = x.shape
    ts = _tile(s, 256)
    row = pl.BlockSpec((1, ts, d), lambda bb, i: (bb, i, 0))
    return pl.pallas_call(
        _post_kernel,
        out_shape=jax.ShapeDtypeStruct((b, s, d), F32),
        grid=(b, s // ts),
        in_specs=[row, row, pl.BlockSpec((1, d), lambda bb, i: (0, 0)), _mod_spec(d, k_gate)],
        out_specs=row,
        compiler_params=_cparams(("arbitrary", "arbitrary")),
        name="post_norm",
    )(y, x, gpost.reshape(1, d), mod3)


def _rope_proj_kernel(a_ref, w_ref, cos_ref, sin_ref, o_ref, *, full_rope_tiles, partial_chunks):
    j = pl.program_id(1)
    acc = jnp.dot(a_ref[...], w_ref[...], preferred_element_type=F32)
    cos, sin = cos_ref[...], sin_ref[...]
    head = j <= full_rope_tiles
    tail = j < full_rope_tiles
    tables = {True: (jnp.where(head, cos, 1.0), jnp.where(head, sin, 0.0)),
              False: (jnp.where(tail, cos, 1.0), jnp.where(tail, sin, 0.0))}
    for c in range(acc.shape[1] // LANES):
        cols = slice(c * LANES, (c + 1) * LANES)
        cos_c, sin_c = tables[c < partial_chunks]
        blk = acc[:, cols]
        o_ref[:, cols] = (blk * cos_c + pltpu.roll(blk, HEAD_DIM // 2, 1) * sin_c).astype(o_ref.dtype)


def _rope_proj(h, w, cos_full, sin_signed, *, n_rope, seq):
    t, d = h.shape
    n = w.shape[1]
    tm = _tile(seq, 1024)
    tn = _tile(n, 1024)
    assert n_rope % LANES == 0 and tn % LANES == 0
    n_seq_tiles = seq // tm
    kern = functools.partial(_rope_proj_kernel, full_rope_tiles=n_rope // tn,
                             partial_chunks=(n_rope % tn) // LANES)
    return pl.pallas_call(
        kern,
        out_shape=jax.ShapeDtypeStruct((t, n), BF16),
        grid=(t // tm, n // tn),
        in_specs=[
            pl.BlockSpec((tm, d), lambda i, j: (i, 0)),
            pl.BlockSpec((d, tn), lambda i, j: (0, j)),
            pl.BlockSpec((tm, HEAD_DIM), lambda i, j: (i % n_seq_tiles, 0)),
            pl.BlockSpec((tm, HEAD_DIM), lambda i, j: (i % n_seq_tiles, 0)),
        ],
        out_specs=pl.BlockSpec((tm, tn), lambda i, j: (i, j)),
        compiler_params=_cparams(("arbitrary", "arbitrary")),
        name="swa_proj",
    )(h, w, cos_full, sin_signed)


def _swa_kernel(sink_ref, q_ref, kp_ref, k_ref, vp_ref, v_ref, o_ref, p_ref, *, tq):
    i = pl.program_id(1)
    kvh = pl.program_id(2)
    w = SWA_WINDOW
    rows = SWA_GROUP * w
    qi = lax.broadcasted_iota(jnp.int32, (rows, 2 * w), 0) % w
    kj = lax.broadcasted_iota(jnp.int32, (rows, 2 * w), 1)
    band = jnp.logical_and(kj > qi, kj <= qi + w)
    first = jnp.logical_and(band, jnp.logical_or(kj >= w, i > 0))
    sink_fill = jnp.concatenate(
        [jnp.full((w, 2 * w), sink_ref[kvh * SWA_GROUP + g], F32) for g in range(SWA_GROUP)], axis=0)
    masked_fill = jnp.where(kj == 0, sink_fill, -jnp.inf)
    key_row = lax.broadcasted_iota(jnp.int32, (2 * w, HEAD_DIM), 0)
    ones = jnp.ones((2 * w, HEAD_DIM), BF16)
    scale = HEAD_DIM ** -0.5
    n_sub = tq // w
    for s in range(n_sub):
        q = jnp.concatenate(
            [q_ref[s * w:(s + 1) * w, g * HEAD_DIM:(g + 1) * HEAD_DIM] for g in range(SWA_GROUP)],
            axis=0)
        if s == 0:
            k = jnp.concatenate([kp_ref[...], k_ref[0:w, :]], axis=0)
            valid = first
        else:
            k = k_ref[(s - 1) * w:(s + 1) * w, :]
            valid = band
        sc = lax.dot_general(q, k, (((1,), (1,)), ((), ())), preferred_element_type=F32) * scale
        sc = jnp.where(valid, sc, masked_fill)
        p_ref[s] = jnp.exp(sc - jnp.max(sc, axis=-1, keepdims=True)).astype(BF16)
    for s in range(n_sub):
        if s == 0:
            v = jnp.concatenate([vp_ref[...], v_ref[0:w, :]], axis=0)
        else:
            v = v_ref[(s - 1) * w:(s + 1) * w, :]
        v = jnp.where(key_row == 0, jnp.zeros_like(v), v)
        ov = jnp.dot(p_ref[s], jnp.concatenate([v, ones], axis=1), preferred_element_type=F32)
        o = ov[:, :HEAD_DIM] / ov[:, HEAD_DIM:]
        for g in range(SWA_GROUP):
            o_ref[s * w:(s + 1) * w, g * HEAD_DIM:(g + 1) * HEAD_DIM] = (
                o[g * w:(g + 1) * w, :].astype(o_ref.dtype))


def _swa(proj, sinks, *, batch, seq, q_heads, k_col, v_col):
    kv_heads = q_heads // SWA_GROUP
    w = SWA_WINDOW
    tq = _tile(seq, 512)
    nq = seq // tq
    sub = tq // w
    gw = SWA_GROUP * HEAD_DIM
    kb, vb = k_col // HEAD_DIM, v_col // HEAD_DIM

    def prev_rows(b, i):
        return b * (seq // w) + jnp.maximum(i * sub - 1, 0)

    grid_spec = pltpu.PrefetchScalarGridSpec(
        num_scalar_prefetch=1,
        grid=(batch, nq, kv_heads),
        in_specs=[
            pl.BlockSpec((tq, gw), lambda b, i, h, s: (b * nq + i, h)),
            pl.BlockSpec((w, HEAD_DIM), lambda b, i, h, s: (prev_rows(b, i), kb + h)),
            pl.BlockSpec((tq, HEAD_DIM), lambda b, i, h, s: (b * nq + i, kb + h)),
            pl.BlockSpec((w, HEAD_DIM), lambda b, i, h, s: (prev_rows(b, i), vb + h)),
            pl.BlockSpec((tq, HEAD_DIM), lambda b, i, h, s: (b * nq + i, vb + h)),
        ],
        out_specs=pl.BlockSpec((tq, gw), lambda b, i, h, s: (b * nq + i, h)),
        scratch_shapes=[pltpu.VMEM((sub, SWA_GROUP * w, 2 * w), BF16)],
    )
    return pl.pallas_call(
        functools.partial(_swa_kernel, tq=tq),
        out_shape=jax.ShapeDtypeStruct((batch * seq, q_heads * HEAD_DIM), BF16),
        grid_spec=grid_spec,
        compiler_params=_cparams(("arbitrary", "arbitrary", "arbitrary")),
        name="swa_attention",
    )(sinks, proj, proj, proj, proj, proj)


def _sb_kernel(q_ref, k_ref, v_ref, o_ref, hilo0_ref, hilo1_ref, lb0_ref, lb1_ref, acc_ref, r_ref,
               *, tq, hp):
    hilo_refs = (hilo0_ref, hilo1_ref)
    lb_refs = (lb0_ref, lb1_ref)
    i = pl.program_id(2)
    row = lax.broadcasted_iota(jnp.int32, (tq, tq), 0)
    col = lax.broadcasted_iota(jnp.int32, (tq, tq), 1)
    strict_lower = col < row
    neg_later = jnp.where(row > col, -1.0, 0.0).astype(BF16)
    neg_later = jnp.concatenate([neg_later, neg_later], axis=0)
    z_scale = HEAD_DIM ** -0.5 * LOG2E
    sign_bit = jnp.uint32(0x80000000)

    def logits_stage(slot, j, diag, heads=range(hp)):
        start = pl.multiple_of(j * tq, tq)
        for h in heads:
            cols = slice(h * HEAD_DIM, (h + 1) * HEAD_DIM)
            q = q_ref[:, cols]
            k = k_ref[pl.ds(start, tq), cols]
            z = lax.dot_general(q, k, (((1,), (1,)), ((), ())),
                                preferred_element_type=F32) * z_scale
            neg_abs = lax.bitcast_convert_type(
                lax.bitcast_convert_type(z, jnp.uint32) | sign_bit, F32)
            softplus = jnp.maximum(z, 0.0) + jnp.log(1.0 + jnp.exp2(neg_abs)) * LOG2E
            log_beta = z - softplus
            if diag:
                lb_refs[slot][h] = jnp.where(strict_lower, log_beta, -jnp.inf)
                softplus = jnp.where(strict_lower, softplus, 0.0)
                r_ref[h] = -jnp.sum(softplus, axis=-1, keepdims=True)
            else:
                r_sum = r_ref[h]
                lb_refs[slot][h] = log_beta + r_sum
                r_ref[h] = r_sum - jnp.sum(softplus, axis=-1, keepdims=True)
            hi = softplus.astype(BF16)
            lo = (softplus - hi.astype(F32)).astype(BF16)
            hilo_refs[slot][h] = jnp.concatenate([hi, lo], axis=1)

    def weights_stage(slot, j, first, heads=range(hp)):
        start = pl.multiple_of(j * tq, tq)
        for h in heads:
            cols = slice(h * HEAD_DIM, (h + 1) * HEAD_DIM)
            v = v_ref[pl.ds(start, tq), cols]
            suffix = jnp.dot(hilo_refs[slot][h], neg_later, preferred_element_type=F32)
            att = jnp.exp2(lb_refs[slot][h] + suffix)
            pv = jnp.dot(att.astype(BF16), v, preferred_element_type=F32)
            if first:
                acc_ref[h] = pv
            else:
                acc_ref[h] += pv

    def any_row_alive():
        top = functools.reduce(
            jnp.maximum, [jnp.max(r_ref[h], axis=0, keepdims=True) for h in range(hp)])
        return top[0, 0] > SB_UNDERFLOW_LOG2

    logits_stage(0, i, True)

    @pl.when(i == 0)
    def _():
        weights_stage(0, i, True)

    @pl.when(i > 0)
    def _():
        for h in range(hp):
            weights_stage(0, i, True, heads=(h,))
            logits_stage(1, i - 1, False, heads=(h,))

        def cond(state):
            n, alive = state
            return jnp.logical_and(n <= i, alive)

        def body(state):
            n, _ = state
            weights_stage(1, i - (n - 1), False)
            logits_stage(1, i - n, False)
            return n + 1, any_row_alive()

        n, _ = lax.while_loop(cond, body, (jnp.int32(2), any_row_alive()))
        weights_stage(1, i - (n - 1), False)

    for h in range(hp):
        o_ref[:, h * HEAD_DIM:(h + 1) * HEAD_DIM] = acc_ref[h].astype(o_ref.dtype)


def _sb(proj, *, batch, seq, heads, q_col, k_col, v_col):
    tq = _tile(seq, 256)
    nq = seq // tq
    hp = 4 if heads % 4 == 0 else 1
    hw = hp * HEAD_DIM
    assert q_col % hw == 0 and k_col % hw == 0 and v_col % hw == 0
    qb, kb, vb = q_col // hw, k_col // hw, v_col // hw
    return pl.pallas_call(
        functools.partial(_sb_kernel, tq=tq, hp=hp),
        out_shape=jax.ShapeDtypeStruct((batch * seq, heads * HEAD_DIM), BF16),
        grid=(batch, heads // hp, nq),
        in_specs=[
            pl.BlockSpec((tq, hw), lambda b, h, i: (b * nq + i, qb + h)),
            pl.BlockSpec((seq, hw), lambda b, h, i: (b, kb + h)),
            pl.BlockSpec((seq, hw), lambda b, h, i: (b, vb + h)),
        ],
        out_specs=pl.BlockSpec((tq, hw), lambda b, h, i: (b * nq + i, h)),
        scratch_shapes=[pltpu.VMEM((hp, tq, 2 * tq), BF16), pltpu.VMEM((hp, tq, 2 * tq), BF16),
                        pltpu.VMEM((hp, tq, tq), F32), pltpu.VMEM((hp, tq, tq), F32),
                        pltpu.VMEM((hp, tq, HEAD_DIM), F32), pltpu.VMEM((hp, tq, 1), F32)],
        compiler_params=_cparams(("arbitrary", "arbitrary", "arbitrary")),
        name="sb_attention",
    )(proj, proj, proj)


def _merge_kernel(a1_ref, a2_ref, w1_ref, w2_ref, ga_ref, gb_ref, o_ref):
    y1 = jnp.dot(a1_ref[...], w1_ref[...], preferred_element_type=F32)
    y2 = jnp.dot(a2_ref[...], w2_ref[...], preferred_element_type=F32)
    o = ga_ref[...].astype(F32) * y1 + gb_ref[...].astype(F32) * y2
    o_ref[...] = o.astype(o_ref.dtype)


def _merge(a_swa, a_sb, w_swa, w_sb, gates):
    t, k1 = a_swa.shape
    k2 = a_sb.shape[1]
    d = w_swa.shape[1]
    tm = _tile(t, 1024)
    tn = _tile(d, 1024)
    ga, gb = 0, d // tn
    return pl.pallas_call(
        _merge_kernel,
        out_shape=jax.ShapeDtypeStruct((t, d), BF16),
        grid=(t // tm, d // tn),
        in_specs=[
            pl.BlockSpec((tm, k1), lambda i, j: (i, 0)),
            pl.BlockSpec((tm, k2), lambda i, j: (i, 0)),
            pl.BlockSpec((k1, tn), lambda i, j: (0, j)),
            pl.BlockSpec((k2, tn), lambda i, j: (0, j)),
            pl.BlockSpec((tm, tn), lambda i, j: (i, ga + j)),
            pl.BlockSpec((tm, tn), lambda i, j: (i, gb + j)),
        ],
        out_specs=pl.BlockSpec((tm, tn), lambda i, j: (i, j)),
        compiler_params=_cparams(("arbitrary", "arbitrary")),
        name="branch_merge",
    )(a_swa, a_sb, w_swa, w_sb, gates, gates)


def _mm_kernel(*refs, epilogue, has_bias, n_casts):
    a_ref, w_ref = refs[:2]
    b_ref = refs[2] if has_bias else None
    first_cast = 2 + has_bias
    cast_in = refs[first_cast:first_cast + n_casts]
    o_ref = refs[first_cast + n_casts]
    cast_out = refs[first_cast + n_casts + 1:]
    acc = jnp.dot(a_ref[...], w_ref[...], preferred_element_type=F32)
    if has_bias:
        acc = acc + b_ref[...]
    if epilogue == "relu_sq":
        acc = jnp.square(jnp.maximum(acc, 0.0))
    elif epilogue == "sigmoid":
        acc = jax.nn.sigmoid(acc)
    o_ref[...] = acc.astype(o_ref.dtype)
    for src, dst in zip(cast_in, cast_out):
        dst[...] = src[...].astype(dst.dtype)


def _mm(a, w, out_dtype, *, bias=None, epilogue=None, casts=(), name):
    t, k = a.shape
    n = w.shape[1]
    tm = _tile(t, 1024)
    tn = _tile(n, 1024)
    ni, nj = t // tm, n // tn
    steps = ni * nj
    in_specs = [pl.BlockSpec((tm, k), lambda i, j: (i, 0)),
                pl.BlockSpec((k, tn), lambda i, j: (0, j))]
    args = [a, w]
    if bias is not None:
        in_specs.append(pl.BlockSpec((1, tn), lambda i, j: (0, j)))
        args.append(bias.reshape(1, n))
    out_specs = [pl.BlockSpec((tm, tn), lambda i, j: (i, j))]
    out_shape = [jax.ShapeDtypeStruct((t, n), out_dtype)]
    for c in casts:
        rows, cols = c.shape
        slab = rows // steps
        assert rows % steps == 0 and slab % BF16_SUBLANES == 0, (c.shape, steps)
        spec = pl.BlockSpec((1, slab, cols), lambda i, j: (i * nj + j, 0, 0))
        in_specs.append(spec)
        args.append(c.reshape(steps, slab, cols))
        out_specs.append(spec)
        out_shape.append(jax.ShapeDtypeStruct((steps, slab, cols), BF16))
    outs = pl.pallas_call(
        functools.partial(_mm_kernel, epilogue=epilogue, has_bias=bias is not None,
                          n_casts=len(casts)),
        out_shape=out_shape,
        grid=(ni, nj),
        in_specs=in_specs,
        out_specs=out_specs,
        compiler_params=_cparams(("arbitrary", "arbitrary")),
        name=name,
    )(*args)
    return outs[0], [o.reshape(c.shape) for o, c in zip(outs[1:], casts)]


def _mm_ksplit_kernel(a_ref, w_ref, o_ref):
    kk = pl.program_id(2)
    part = jnp.dot(a_ref[...], w_ref[...], preferred_element_type=F32)

    @pl.when(kk == 0)
    def _():
        o_ref[...] = part

    @pl.when(kk > 0)
    def _():
        o_ref[...] += part


def _mm_ksplit(a, w, *, name):
    t, k = a.shape
    n = w.shape[1]
    tm = _tile(t, 1024)
    tn = _tile(n, 1024)
    tk = _tile(k, 4096)
    return pl.pallas_call(
        _mm_ksplit_kernel,
        out_shape=jax.ShapeDtypeStruct((t, n), F32),
        grid=(t // tm, n // tn, k // tk),
        in_specs=[pl.BlockSpec((tm, tk), lambda i, j, kk: (i, kk)),
                  pl.BlockSpec((tk, tn), lambda i, j, kk: (kk, j))],
        out_specs=pl.BlockSpec((tm, tn), lambda i, j, kk: (i, j)),
        compiler_params=_cparams(("arbitrary", "arbitrary", "arbitrary")),
        name=name,
    )(a, w)


def _rope_tables(seq):
    inv_freq = 1.0 / (ROPE_THETA ** (jnp.arange(0, HEAD_DIM, 2, dtype=F32) / HEAD_DIM))
    ang = jnp.arange(seq, dtype=F32)[:, None] * inv_freq[None, :]
    cos, sin = jnp.cos(ang), jnp.sin(ang)
    return jnp.concatenate([cos, cos], axis=-1), jnp.concatenate([-sin, sin], axis=-1)


def kernel(x, c, w_ada, b_ada, g_pre_mix, g_post_mix, g_pre_mlp, g_post_mlp, w_in, attn_sinks,
           w_branch_gate, b_branch_gate, w_proj_swa, w_proj_sb, w_out, w_ff_in, w_ff_out):
    batch, seq, d = x.shape
    depth = w_in.shape[0]
    d_in = w_in.shape[2]
    swa_q_heads = attn_sinks.shape[1]
    swa_q_w = w_proj_swa.shape[1]
    sb_w = w_proj_sb.shape[1]
    swa_kv_w = (d_in - swa_q_w - 3 * sb_w) // 2
    sb_heads = sb_w // HEAD_DIM
    assert swa_q_w == swa_q_heads * HEAD_DIM and swa_kv_w * SWA_GROUP == swa_q_w
    assert seq % SWA_WINDOW == 0
    t = batch * seq

    cos_full, sin_signed = _rope_tables(seq)
    pad_rows = (-batch) % 8
    c_pad = jnp.pad(c, ((0, pad_rows), (0, 0)))

    swa_w = swa_q_w + 2 * swa_kv_w
    n_rope = swa_q_w + swa_kv_w

    for l in range(depth):
        mod = _ada_mod(c_pad, w_ada[l], b_ada[l])[:batch]
        mod3 = mod.reshape(batch * N_MOD, 1, d)

        h = _prenorm(x, g_pre_mix[l], mod3, 0, 1).reshape(t, d)
        proj_swa = _rope_proj(h, w_in[l][:, :swa_w].astype(BF16), cos_full, sin_signed,
                              n_rope=n_rope, seq=seq)
        proj_sb, _ = _mm(h, w_in[l][:, swa_w:].astype(BF16), BF16, name="sb_proj")
        gates, (w_swa_bf, w_sb_bf, w_out_bf, w_ff_in_bf) = _mm(
            h, w_branch_gate[l].astype(BF16), BF16, bias=b_branch_gate[l], epilogue="sigmoid",
            casts=(w_proj_swa[l], w_proj_sb[l], w_out[l], w_ff_in[l]), name="gate_proj")
        a_swa = _swa(proj_swa, attn_sinks[l], batch=batch, seq=seq, q_heads=swa_q_heads,
                     k_col=swa_q_w, v_col=n_rope)
        a_sb = _sb(proj_sb, batch=batch, seq=seq, heads=sb_heads,
                   q_col=0, k_col=sb_w, v_col=2 * sb_w)
        merged = _merge(a_swa, a_sb, w_swa_bf, w_sb_bf, gates)
        mixed, _ = _mm(merged, w_out_bf, F32, name="out_proj")
        x1, h2 = _post_pre(mixed.reshape(batch, seq, d), x, g_post_mix[l], g_pre_mlp[l], mod3,
                           2, 3, 4)
        act, (w_ff_out_bf,) = _mm(h2.reshape(t, d), w_ff_in_bf, BF16, epilogue="relu_sq",
                                  casts=(w_ff_out[l],), name="ff_in")
        f = _mm_ksplit(act, w_ff_out_bf, name="ff_out").reshape(batch, seq, d)
        x = _post(f, x1, g_post_mlp[l], mod3, 5)
    return x
```

```python
import functools
import math

import jax
import jax.numpy as jnp
from jax import lax
from jax.experimental import pallas as pl
from jax.experimental.pallas import tpu as pltpu

F32 = jnp.float32
BF16 = jnp.bfloat16

HEAD_DIM = 128
LANES = 128
BF16_SUBLANES = 16
SWA_WINDOW = 128
SWA_GROUP = 4
ROPE_THETA = 10000.0
NORM_EPS = 1e-6
LOG2E = 1.4426950408889634
SB_UNDERFLOW_LOG2 = -152.0
N_MOD = 6
MIB = 1024 * 1024
VMEM_LIMIT_BYTES = 56 * MIB


def _tile(dim, pref):
    t = min(dim, pref)
    while dim % t:
        t //= 2
    return t


def _cparams(semantics, vmem=VMEM_LIMIT_BYTES, flags=None):
    return pltpu.CompilerParams(dimension_semantics=semantics, vmem_limit_bytes=vmem, flags=flags)


def _ada_kernel(c_ref, w_ref, b_ref, o_ref):
    c = c_ref[...]
    act = (c * jax.nn.sigmoid(c)).astype(BF16)
    o_ref[...] = jnp.dot(act, w_ref[...].astype(BF16), preferred_element_type=F32) + b_ref[...]


def _ada_mod(c_pad, w_ada, b_ada):
    rows, d = c_pad.shape
    n = w_ada.shape[1]
    tn = _tile(n, 1024)
    return pl.pallas_call(
        _ada_kernel,
        out_shape=jax.ShapeDtypeStruct((rows, n), F32),
        grid=(n // tn,),
        in_specs=[
            pl.BlockSpec((rows, d), lambda j: (0, 0)),
            pl.BlockSpec((d, tn), lambda j: (0, j)),
            pl.BlockSpec((1, tn), lambda j: (0, j)),
        ],
        out_specs=pl.BlockSpec((rows, tn), lambda j: (0, j)),
        compiler_params=_cparams(("arbitrary",)),
        name="ada_mod",
    )(c_pad, w_ada, b_ada.reshape(1, n))


def _rms(x, g):
    y = x * lax.rsqrt(jnp.mean(x * x, axis=-1, keepdims=True) + NORM_EPS)
    return y * g


def _prenorm_kernel(x_ref, g_ref, scale_ref, shift_ref, h_ref):
    h = _rms(x_ref[0], g_ref[...]) * (1.0 + scale_ref[0]) + shift_ref[0]
    h_ref[0] = h.astype(h_ref.dtype)


def _mod_spec(d, k):
    return pl.BlockSpec((1, 1, d), lambda b, i: (b * N_MOD + k, 0, 0))


def _prenorm(x, g, mod3, k_shift, k_scale):
    b, s, d = x.shape
    ts = _tile(s, 256)
    row = pl.BlockSpec((1, ts, d), lambda bb, i: (bb, i, 0))
    return pl.pallas_call(
        _prenorm_kernel,
        out_shape=jax.ShapeDtypeStruct((b, s, d), BF16),
        grid=(b, s // ts),
        in_specs=[row, pl.BlockSpec((1, d), lambda bb, i: (0, 0)),
                  _mod_spec(d, k_scale), _mod_spec(d, k_shift)],
        out_specs=row,
        compiler_params=_cparams(("arbitrary", "arbitrary")),
        name="prenorm",
    )(x, g.reshape(1, d), mod3, mod3)


def _post_pre_kernel(y_ref, x_ref, gpost_ref, gate_ref, gpre_ref, scale_ref, shift_ref,
                     x1_ref, h_ref):
    x1 = x_ref[0] + gate_ref[0] * _rms(y_ref[0], gpost_ref[...])
    x1_ref[0] = x1
    h = _rms(x1, gpre_ref[...]) * (1.0 + scale_ref[0]) + shift_ref[0]
    h_ref[0] = h.astype(h_ref.dtype)


def _post_pre(y, x, gpost, gpre, mod3, k_gate, k_shift, k_scale):
    b, s, d = x.shape
    ts = _tile(s, 256)
    row = pl.BlockSpec((1, ts, d), lambda bb, i: (bb, i, 0))
    vec = pl.BlockSpec((1, d), lambda bb, i: (0, 0))
    return pl.pallas_call(
        _post_pre_kernel,
        out_shape=(jax.ShapeDtypeStruct((b, s, d), F32), jax.ShapeDtypeStruct((b, s, d), BF16)),
        grid=(b, s // ts),
        in_specs=[row, row, vec, _mod_spec(d, k_gate), vec, _mod_spec(d, k_scale),
                  _mod_spec(d, k_shift)],
        out_specs=(row, row),
        compiler_params=_cparams(("arbitrary", "arbitrary")),
        name="post_pre_norm",
    )(y, x, gpost.reshape(1, d), mod3, gpre.reshape(1, d), mod3, mod3)


def _post_kernel(y_ref, x_ref, gpost_ref, gate_ref, o_ref):
    o_ref[0] = x_ref[0] + gate_ref[0] * _rms(y_ref[0], gpost_ref[...])


def _post(y, x, gpost, mod3, k_gate):
    b, s, d = x.shape
    ts = _tile(s, 256)
    row = pl.BlockSpec((1, ts, d), lambda bb, i: (bb, i, 0))
    return pl.pallas_call(
        _post_kernel,
        out_shape=jax.ShapeDtypeStruct((b, s, d), F32),
        grid=(b, s // ts),
        in_specs=[row, row, pl.BlockSpec((1, d), lambda bb, i: (0, 0)), _mod_spec(d, k_gate)],
        out_specs=row,
        compiler_params=_cparams(("arbitrary", "arbitrary")),
        name="post_norm",
    )(y, x, gpost.reshape(1, d), mod3)


def _rope_proj_kernel(*refs, full_rope_tiles, partial_chunks, n_casts):
    a_ref, w_ref, cos_ref, sin_ref = refs[:4]
    cast_in = refs[4:4 + n_casts]
    o_ref = refs[4 + n_casts]
    cast_out = refs[5 + n_casts:]
    for src, dst in zip(cast_in, cast_out):
        dst[...] = src[...].astype(dst.dtype)
    j = pl.program_id(1)
    acc = jnp.dot(a_ref[...], w_ref[...], preferred_element_type=F32)
    cos, sin = cos_ref[...], sin_ref[...]
    head = j <= full_rope_tiles
    tail = j < full_rope_tiles
    tables = {True: (jnp.where(head, cos, 1.0), jnp.where(head, sin, 0.0)),
              False: (jnp.where(tail, cos, 1.0), jnp.where(tail, sin, 0.0))}
    for c in range(acc.shape[1] // LANES):
        cols = slice(c * LANES, (c + 1) * LANES)
        cos_c, sin_c = tables[c < partial_chunks]
        blk = acc[:, cols]
        o_ref[:, cols] = (blk * cos_c + pltpu.roll(blk, HEAD_DIM // 2, 1) * sin_c).astype(o_ref.dtype)


def _rope_proj(h, w, cos_full, sin_signed, *, n_rope, seq, casts=()):
    t, d = h.shape
    n = w.shape[1]
    tm = _tile(seq, 1024)
    tn = _tile(n, 1024)
    assert n_rope % LANES == 0 and tn % LANES == 0
    n_seq_tiles = seq // tm
    ni, nj = t // tm, n // tn
    cast_in_specs, cast_out_specs, cast_args, cast_shapes = _cast_plan(casts, ni, nj)
    kern = functools.partial(_rope_proj_kernel, full_rope_tiles=n_rope // tn,
                             partial_chunks=(n_rope % tn) // LANES, n_casts=len(casts))
    outs = pl.pallas_call(
        kern,
        out_shape=[jax.ShapeDtypeStruct((t, n), BF16)] + cast_shapes,
        grid=(ni, nj),
        in_specs=[
            pl.BlockSpec((tm, d), lambda i, j: (i, 0)),
            pl.BlockSpec((d, tn), lambda i, j: (0, j)),
            pl.BlockSpec((tm, HEAD_DIM), lambda i, j: (i % n_seq_tiles, 0)),
            pl.BlockSpec((tm, HEAD_DIM), lambda i, j: (i % n_seq_tiles, 0)),
        ] + cast_in_specs,
        out_specs=[pl.BlockSpec((tm, tn), lambda i, j: (i, j))] + cast_out_specs,
        compiler_params=_cparams(("arbitrary", "arbitrary")),
        name="swa_proj",
    )(h, w, cos_full, sin_signed, *cast_args)
    return outs[0], [o.reshape(-1, o.shape[-1]) for o in outs[1:]]


def _swa_kernel(sink_ref, q_ref, kp_ref, k_ref, vp_ref, v_ref, o_ref, p_ref, *, tq):
    i = pl.program_id(1)
    kvh = pl.program_id(2)
    w = SWA_WINDOW
    rows = SWA_GROUP * w
    qi = lax.broadcasted_iota(jnp.int32, (rows, 2 * w), 0) % w
    kj = lax.broadcasted_iota(jnp.int32, (rows, 2 * w), 1)
    band = jnp.logical_and(kj > qi, kj <= qi + w)
    first = jnp.logical_and(band, jnp.logical_or(kj >= w, i > 0))
    sink_fill = jnp.concatenate(
        [jnp.full((w, 2 * w), sink_ref[kvh * SWA_GROUP + g], F32) for g in range(SWA_GROUP)], axis=0)
    masked_fill = jnp.where(kj == 0, sink_fill, -jnp.inf)
    key_row = lax.broadcasted_iota(jnp.int32, (2 * w, HEAD_DIM), 0)
    ones = jnp.ones((2 * w, HEAD_DIM), BF16)
    scale = HEAD_DIM ** -0.5
    n_sub = tq // w
    for s in range(n_sub):
        q = jnp.concatenate(
            [q_ref[s * w:(s + 1) * w, g * HEAD_DIM:(g + 1) * HEAD_DIM] for g in range(SWA_GROUP)],
            axis=0)
        if s == 0:
            k = jnp.concatenate([kp_ref[...], k_ref[0:w, :]], axis=0)
            valid = first
        else:
            k = k_ref[(s - 1) * w:(s + 1) * w, :]
            valid = band
        sc = lax.dot_general(q, k, (((1,), (1,)), ((), ())), preferred_element_type=F32) * scale
        sc = jnp.where(valid, sc, masked_fill)
        p_ref[s] = jnp.exp(sc - jnp.max(sc, axis=-1, keepdims=True)).astype(BF16)
    for s in range(n_sub):
        if s == 0:
            v = jnp.concatenate([vp_ref[...], v_ref[0:w, :]], axis=0)
        else:
            v = v_ref[(s - 1) * w:(s + 1) * w, :]
        v = jnp.where(key_row == 0, jnp.zeros_like(v), v)
        ov = jnp.dot(p_ref[s], jnp.concatenate([v, ones], axis=1), preferred_element_type=F32)
        o = ov[:, :HEAD_DIM] / ov[:, HEAD_DIM:]
        for g in range(SWA_GROUP):
            o_ref[s * w:(s + 1) * w, g * HEAD_DIM:(g + 1) * HEAD_DIM] = (
                o[g * w:(g + 1) * w, :].astype(o_ref.dtype))


def _swa(proj, sinks, *, batch, seq, q_heads, k_col, v_col):
    kv_heads = q_heads // SWA_GROUP
    w = SWA_WINDOW
    tq = _tile(seq, 512)
    nq = seq // tq
    sub = tq // w
    gw = SWA_GROUP * HEAD_DIM
    kb, vb = k_col // HEAD_DIM, v_col // HEAD_DIM

    def prev_rows(b, i):
        return b * (seq // w) + jnp.maximum(i * sub - 1, 0)

    grid_spec = pltpu.PrefetchScalarGridSpec(
        num_scalar_prefetch=1,
        grid=(batch, nq, kv_heads),
        in_specs=[
            pl.BlockSpec((tq, gw), lambda b, i, h, s: (b * nq + i, h)),
            pl.BlockSpec((w, HEAD_DIM), lambda b, i, h, s: (prev_rows(b, i), kb + h)),
            pl.BlockSpec((tq, HEAD_DIM), lambda b, i, h, s: (b * nq + i, kb + h)),
            pl.BlockSpec((w, HEAD_DIM), lambda b, i, h, s: (prev_rows(b, i), vb + h)),
            pl.BlockSpec((tq, HEAD_DIM), lambda b, i, h, s: (b * nq + i, vb + h)),
        ],
        out_specs=pl.BlockSpec((tq, gw), lambda b, i, h, s: (b * nq + i, h)),
        scratch_shapes=[pltpu.VMEM((sub, SWA_GROUP * w, 2 * w), BF16)],
    )
    return pl.pallas_call(
        functools.partial(_swa_kernel, tq=tq),
        out_shape=jax.ShapeDtypeStruct((batch * seq, q_heads * HEAD_DIM), BF16),
        grid_spec=grid_spec,
        compiler_params=_cparams(("arbitrary", "arbitrary", "arbitrary")),
        name="swa_attention",
    )(sinks, proj, proj, proj, proj, proj)


def _sb_kernel(q_ref, k_ref, v_ref, o_ref, hilo0_ref, hilo1_ref, lb0_ref, lb1_ref, acc_ref, r_ref,
               *, tq, hp):
    hilo_refs = (hilo0_ref, hilo1_ref)
    lb_refs = (lb0_ref, lb1_ref)
    i = pl.program_id(2)
    row = lax.broadcasted_iota(jnp.int32, (tq, tq), 0)
    col = lax.broadcasted_iota(jnp.int32, (tq, tq), 1)
    strict_lower = col < row
    neg_later = jnp.where(row > col, -1.0, 0.0).astype(BF16)
    neg_later = jnp.concatenate([neg_later, neg_later], axis=0)
    z_scale = HEAD_DIM ** -0.5 * LOG2E
    sign_bit = jnp.uint32(0x80000000)

    def logits_stage(slot, j, diag, heads=range(hp)):
        start = pl.multiple_of(j * tq, tq)
        for h in heads:
            cols = slice(h * HEAD_DIM, (h + 1) * HEAD_DIM)
            q = q_ref[:, cols]
            k = k_ref[pl.ds(start, tq), cols]
            z = lax.dot_general(q, k, (((1,), (1,)), ((), ())),
                                preferred_element_type=F32) * z_scale
            neg_abs = lax.bitcast_convert_type(
                lax.bitcast_convert_type(z, jnp.uint32) | sign_bit, F32)
            softplus = jnp.maximum(z, 0.0) + jnp.log(1.0 + jnp.exp2(neg_abs)) * LOG2E
            log_beta = z - softplus
            if diag:
                lb_refs[slot][h] = jnp.where(strict_lower, log_beta, -jnp.inf)
                softplus = jnp.where(strict_lower, softplus, 0.0)
                r_ref[h] = -jnp.sum(softplus, axis=-1, keepdims=True)
            else:
                r_sum = r_ref[h]
                lb_refs[slot][h] = log_beta + r_sum
                r_ref[h] = r_sum - jnp.sum(softplus, axis=-1, keepdims=True)
            hi = softplus.astype(BF16)
            lo = (softplus - hi.astype(F32)).astype(BF16)
            hilo_refs[slot][h] = jnp.concatenate([hi, lo], axis=1)

    def weights_stage(slot, j, first, heads=range(hp)):
        start = pl.multiple_of(j * tq, tq)
        for h in heads:
            cols = slice(h * HEAD_DIM, (h + 1) * HEAD_DIM)
            v = v_ref[pl.ds(start, tq), cols]
            suffix = jnp.dot(hilo_refs[slot][h], neg_later, preferred_element_type=F32)
            att = jnp.exp2(lb_refs[slot][h] + suffix)
            pv = jnp.dot(att.astype(BF16), v, preferred_element_type=F32)
            if first:
                acc_ref[h] = pv
            else:
                acc_ref[h] += pv

    def any_row_alive():
        top = functools.reduce(
            jnp.maximum, [jnp.max(r_ref[h], axis=0, keepdims=True) for h in range(hp)])
        return top[0, 0] > SB_UNDERFLOW_LOG2

    logits_stage(0, i, True)

    @pl.when(i == 0)
    def _():
        weights_stage(0, i, True)

    @pl.when(i > 0)
    def _():
        for h in range(hp):
            weights_stage(0, i, True, heads=(h,))
            logits_stage(1, i - 1, False, heads=(h,))

        def cond(state):
            n, alive = state
            return jnp.logical_and(n <= i, alive)

        def body(state):
            n, _ = state
            weights_stage(1, i - (n - 1), False)
            logits_stage(1, i - n, False)
            return n + 1, any_row_alive()

        n, _ = lax.while_loop(cond, body, (jnp.int32(2), any_row_alive()))
        weights_stage(1, i - (n - 1), False)

    for h in range(hp):
        o_ref[:, h * HEAD_DIM:(h + 1) * HEAD_DIM] = acc_ref[h].astype(o_ref.dtype)


def _sb(proj, *, batch, seq, heads, q_col, k_col, v_col):
    tq = _tile(seq, 256)
    nq = seq // tq
    hp = 4 if heads % 4 == 0 else 1
    hw = hp * HEAD_DIM
    assert q_col % hw == 0 and k_col % hw == 0 and v_col % hw == 0
    qb, kb, vb = q_col // hw, k_col // hw, v_col // hw
    return pl.pallas_call(
        functools.partial(_sb_kernel, tq=tq, hp=hp),
        out_shape=jax.ShapeDtypeStruct((batch * seq, heads * HEAD_DIM), BF16),
        grid=(batch, heads // hp, nq),
        in_specs=[
            pl.BlockSpec((tq, hw), lambda b, h, i: (b * nq + i, qb + h)),
            pl.BlockSpec((seq, hw), lambda b, h, i: (b, kb + h)),
            pl.BlockSpec((seq, hw), lambda b, h, i: (b, vb + h)),
        ],
        out_specs=pl.BlockSpec((tq, hw), lambda b, h, i: (b * nq + i, h)),
        scratch_shapes=[pltpu.VMEM((hp, tq, 2 * tq), BF16), pltpu.VMEM((hp, tq, 2 * tq), BF16),
                        pltpu.VMEM((hp, tq, tq), F32), pltpu.VMEM((hp, tq, tq), F32),
                        pltpu.VMEM((hp, tq, HEAD_DIM), F32), pltpu.VMEM((hp, tq, 1), F32)],
        compiler_params=_cparams(("arbitrary", "arbitrary", "arbitrary")),
        name="sb_attention",
    )(proj, proj, proj)


def _merge_kernel(a1_ref, a2_ref, w1_ref, w2_ref, ga_ref, gb_ref, o_ref):
    y1 = jnp.dot(a1_ref[...], w1_ref[...], preferred_element_type=F32)
    y2 = jnp.dot(a2_ref[...], w2_ref[...], preferred_element_type=F32)
    o = ga_ref[...].astype(F32) * y1 + gb_ref[...].astype(F32) * y2
    o_ref[...] = o.astype(o_ref.dtype)


def _merge(a_swa, a_sb, w_swa, w_sb, gates):
    t, k1 = a_swa.shape
    k2 = a_sb.shape[1]
    d = w_swa.shape[1]
    tm = _tile(t, 1024)
    tn = _tile(d, 1024)
    ga, gb = 0, d // tn
    return pl.pallas_call(
        _merge_kernel,
        out_shape=jax.ShapeDtypeStruct((t, d), BF16),
        grid=(t // tm, d // tn),
        in_specs=[
            pl.BlockSpec((tm, k1), lambda i, j: (i, 0)),
            pl.BlockSpec((tm, k2), lambda i, j: (i, 0)),
            pl.BlockSpec((k1, tn), lambda i, j: (0, j)),
            pl.BlockSpec((k2, tn), lambda i, j: (0, j)),
            pl.BlockSpec((tm, tn), lambda i, j: (i, ga + j)),
            pl.BlockSpec((tm, tn), lambda i, j: (i, gb + j)),
        ],
        out_specs=pl.BlockSpec((tm, tn), lambda i, j: (i, j)),
        compiler_params=_cparams(("arbitrary", "arbitrary")),
        name="branch_merge",
    )(a_swa, a_sb, w_swa, w_sb, gates, gates)


def _mm_kernel(*refs, epilogue, has_bias, n_casts):
    a_ref, w_ref = refs[:2]
    b_ref = refs[2] if has_bias else None
    first_cast = 2 + has_bias
    cast_in = refs[first_cast:first_cast + n_casts]
    o_ref = refs[first_cast + n_casts]
    cast_out = refs[first_cast + n_casts + 1:]
    acc = jnp.dot(a_ref[...], w_ref[...], preferred_element_type=F32)
    if has_bias:
        acc = acc + b_ref[...]
    if epilogue == "relu_sq":
        acc = jnp.square(jnp.maximum(acc, 0.0))
    elif epilogue == "sigmoid":
        acc = jax.nn.sigmoid(acc)
    o_ref[...] = acc.astype(o_ref.dtype)
    for src, dst in zip(cast_in, cast_out):
        dst[...] = src[...].astype(dst.dtype)


def _cast_plan(casts, ni, nj):
    steps = ni * nj
    in_specs, out_specs, args, shapes = [], [], [], []
    for c in casts:
        c, col0, cols = c if isinstance(c, tuple) else (c, 0, c.shape[1])
        rows = c.shape[0]
        if col0 == 0 and cols == c.shape[1] and rows % (steps * BF16_SUBLANES) == 0:
            slab = rows // steps
            spec = pl.BlockSpec((1, slab, cols), lambda i, j: (i * nj + j, 0, 0))
            in_specs.append(spec)
            out_specs.append(spec)
            args.append(c.reshape(steps, slab, cols))
            shapes.append(jax.ShapeDtypeStruct((steps, slab, cols), BF16))
        else:
            slab = cols // steps
            assert cols % (steps * LANES) == 0 and col0 % slab == 0, (c.shape, col0, cols, steps)
            first = col0 // slab
            in_specs.append(pl.BlockSpec((rows, slab), lambda i, j: (0, first + i * nj + j)))
            out_specs.append(pl.BlockSpec((rows, slab), lambda i, j: (0, i * nj + j)))
            args.append(c)
            shapes.append(jax.ShapeDtypeStruct((rows, cols), BF16))
    return in_specs, out_specs, args, shapes


def _mm(a, w, out_dtype, *, bias=None, epilogue=None, casts=(), tn_pref=1024, name):
    t, k = a.shape
    n = w.shape[1]
    tm = _tile(t, 1024)
    tn = tn_pref if n % tn_pref == 0 else _tile(n, 1024)
    ni, nj = t // tm, n // tn
    in_specs = [pl.BlockSpec((tm, k), lambda i, j: (i, 0)),
                pl.BlockSpec((k, tn), lambda i, j: (0, j))]
    args = [a, w]
    if bias is not None:
        in_specs.append(pl.BlockSpec((1, tn), lambda i, j: (0, j)))
        args.append(bias.reshape(1, n))
    cast_in_specs, cast_out_specs, cast_args, cast_shapes = _cast_plan(casts, ni, nj)
    in_specs += cast_in_specs
    args += cast_args
    out_specs = [pl.BlockSpec((tm, tn), lambda i, j: (i, j))] + cast_out_specs
    out_shape = [jax.ShapeDtypeStruct((t, n), out_dtype)] + cast_shapes
    outs = pl.pallas_call(
        functools.partial(_mm_kernel, epilogue=epilogue, has_bias=bias is not None,
                          n_casts=len(casts)),
        out_shape=out_shape,
        grid=(ni, nj),
        in_specs=in_specs,
        out_specs=out_specs,
        compiler_params=_cparams(("arbitrary", "arbitrary")),
        name=name,
    )(*args)
    return outs[0], [o.reshape(-1, o.shape[-1]) for o in outs[1:]]


def _mm_ksplit_kernel(a_ref, w_ref, o_ref):
    @pl.when(pl.program_id(2) == 0)
    def _():
        o_ref[...] = jnp.zeros_like(o_ref)

    o_ref[...] += jnp.dot(a_ref[...], w_ref[...], preferred_element_type=F32)


def _mm_ksplit(a, w, *, name):
    t, k = a.shape
    n = w.shape[1]
    tm = _tile(t, 1024)
    tn = _tile(n, 1024)
    tk = _tile(k, 4096)
    return pl.pallas_call(
        _mm_ksplit_kernel,
        out_shape=jax.ShapeDtypeStruct((t, n), F32),
        grid=(t // tm, n // tn, k // tk),
        in_specs=[pl.BlockSpec((tm, tk), lambda i, j, kk: (i, kk)),
                  pl.BlockSpec((tk, tn), lambda i, j, kk: (kk, j))],
        out_specs=pl.BlockSpec((tm, tn), lambda i, j, kk: (i, j)),
        compiler_params=_cparams(("arbitrary", "arbitrary", "arbitrary")),
        name=name,
    )(a, w)


def _rope_tables(seq):
    inv_freq = 1.0 / (ROPE_THETA ** (jnp.arange(0, HEAD_DIM, 2, dtype=F32) / HEAD_DIM))
    ang = jnp.arange(seq, dtype=F32)[:, None] * inv_freq[None, :]
    cos, sin = jnp.cos(ang), jnp.sin(ang)
    return jnp.concatenate([cos, cos], axis=-1), jnp.concatenate([-sin, sin], axis=-1)


def kernel(x, c, w_ada, b_ada, g_pre_mix, g_post_mix, g_pre_mlp, g_post_mlp, w_in, attn_sinks,
           w_branch_gate, b_branch_gate, w_proj_swa, w_proj_sb, w_out, w_ff_in, w_ff_out):
    batch, seq, d = x.shape
    depth = w_in.shape[0]
    d_in = w_in.shape[2]
    swa_q_heads = attn_sinks.shape[1]
    swa_q_w = w_proj_swa.shape[1]
    sb_w = w_proj_sb.shape[1]
    swa_kv_w = (d_in - swa_q_w - 3 * sb_w) // 2
    sb_heads = sb_w // HEAD_DIM
    assert swa_q_w == swa_q_heads * HEAD_DIM and swa_kv_w * SWA_GROUP == swa_q_w
    assert seq % SWA_WINDOW == 0
    t = batch * seq

    cos_full, sin_signed = _rope_tables(seq)
    pad_rows = (-batch) % 8
    c_pad = jnp.pad(c, ((0, pad_rows), (0, 0)))

    swa_w = swa_q_w + 2 * swa_kv_w
    n_rope = swa_q_w + swa_kv_w

    for l in range(depth):
        mod = _ada_mod(c_pad, w_ada[l], b_ada[l])[:batch]
        mod3 = mod.reshape(batch * N_MOD, 1, d)

        h = _prenorm(x, g_pre_mix[l], mod3, 0, 1).reshape(t, d)
        proj_swa, (w_sb_in_bf,) = _rope_proj(
            h, w_in[l][:, :swa_w].astype(BF16), cos_full, sin_signed, n_rope=n_rope, seq=seq,
            casts=((w_in[l], swa_w, d_in - swa_w),))
        proj_sb, (w_gate_bf,) = _mm(h, w_sb_in_bf, BF16, casts=(w_branch_gate[l],),
                                    tn_pref=768, name="sb_proj")
        gates, (w_swa_bf, w_sb_bf, w_out_bf, w_ff_in_bf) = _mm(
            h, w_gate_bf, BF16, bias=b_branch_gate[l], epilogue="sigmoid",
            casts=(w_proj_swa[l], w_proj_sb[l], w_out[l], w_ff_in[l]), name="gate_proj")
        a_swa = _swa(proj_swa, attn_sinks[l], batch=batch, seq=seq, q_heads=swa_q_heads,
                     k_col=swa_q_w, v_col=n_rope)
        a_sb = _sb(proj_sb, batch=batch, seq=seq, heads=sb_heads,
                   q_col=0, k_col=sb_w, v_col=2 * sb_w)
        merged = _merge(a_swa, a_sb, w_swa_bf, w_sb_bf, gates)
        mixed, _ = _mm(merged, w_out_bf, F32, name="out_proj")
        x1, h2 = _post_pre(mixed.reshape(batch, seq, d), x, g_post_mix[l], g_pre_mlp[l], mod3,
                           2, 3, 4)
        act, (w_ff_out_bf,) = _mm(h2.reshape(t, d), w_ff_in_bf, BF16, epilogue="relu_sq",
                                  casts=(w_ff_out[l],), name="ff_in")
        f = _mm_ksplit(act, w_ff_out_bf, name="ff_out").reshape(batch, seq, d)
        x = _post(f, x1, g_post_mlp[l], mod3, 5)
    return x
```

```python
import functools
import math

import jax
import jax.numpy as jnp
from jax import lax
from jax.experimental import pallas as pl
from jax.experimental.pallas import tpu as pltpu

F32 = jnp.float32
BF16 = jnp.bfloat16

HEAD_DIM = 128
LANES = 128
BF16_SUBLANES = 16
SWA_WINDOW = 128
SWA_GROUP = 4
ROPE_THETA = 10000.0
NORM_EPS = 1e-6
LOG2E = 1.4426950408889634
SB_UNDERFLOW_LOG2 = -152.0
N_MOD = 6
MIB = 1024 * 1024
VMEM_LIMIT_BYTES = 56 * MIB


def _tile(dim, pref):
    t = min(dim, pref)
    while dim % t:
        t //= 2
    return t


def _cparams(semantics, vmem=VMEM_LIMIT_BYTES, flags=None):
    return pltpu.CompilerParams(dimension_semantics=semantics, vmem_limit_bytes=vmem, flags=flags)


def _ada_kernel(c_ref, w_ref, b_ref, o_ref):
    c = c_ref[...]
    act = (c * jax.nn.sigmoid(c)).astype(BF16)
    o_ref[...] = jnp.dot(act, w_ref[...].astype(BF16), preferred_element_type=F32) + b_ref[...]


def _ada_mod(c_pad, w_ada, b_ada):
    rows, d = c_pad.shape
    n = w_ada.shape[1]
    tn = _tile(n, 1024)
    return pl.pallas_call(
        _ada_kernel,
        out_shape=jax.ShapeDtypeStruct((rows, n), F32),
        grid=(n // tn,),
        in_specs=[
            pl.BlockSpec((rows, d), lambda j: (0, 0)),
            pl.BlockSpec((d, tn), lambda j: (0, j)),
            pl.BlockSpec((1, tn), lambda j: (0, j)),
        ],
        out_specs=pl.BlockSpec((rows, tn), lambda j: (0, j)),
        compiler_params=_cparams(("arbitrary",)),
        name="ada_mod",
    )(c_pad, w_ada, b_ada.reshape(1, n))


def _rms(x, g):
    y = x * lax.rsqrt(jnp.mean(x * x, axis=-1, keepdims=True) + NORM_EPS)
    return y * g


def _prenorm_kernel(x_ref, g_ref, scale_ref, shift_ref, h_ref):
    h = _rms(x_ref[0], g_ref[...]) * (1.0 + scale_ref[0]) + shift_ref[0]
    h_ref[0] = h.astype(h_ref.dtype)


def _mod_spec(d, k):
    return pl.BlockSpec((1, 1, d), lambda b, i: (b * N_MOD + k, 0, 0))


def _prenorm(x, g, mod3, k_shift, k_scale):
    b, s, d = x.shape
    ts = _tile(s, 256)
    row = pl.BlockSpec((1, ts, d), lambda bb, i: (bb, i, 0))
    return pl.pallas_call(
        _prenorm_kernel,
        out_shape=jax.ShapeDtypeStruct((b, s, d), BF16),
        grid=(b, s // ts),
        in_specs=[row, pl.BlockSpec((1, d), lambda bb, i: (0, 0)),
                  _mod_spec(d, k_scale), _mod_spec(d, k_shift)],
        out_specs=row,
        compiler_params=_cparams(("arbitrary", "arbitrary")),
        name="prenorm",
    )(x, g.reshape(1, d), mod3, mod3)


def _post_pre_kernel(y_ref, x_ref, gpost_ref, gate_ref, gpre_ref, scale_ref, shift_ref,
                     x1_ref, h_ref):
    x1 = x_ref[0] + gate_ref[0] * _rms(y_ref[0], gpost_ref[...])
    x1_ref[0] = x1
    h = _rms(x1, gpre_ref[...]) * (1.0 + scale_ref[0]) + shift_ref[0]
    h_ref[0] = h.astype(h_ref.dtype)


def _post_pre(y, x, gpost, gpre, mod3, k_gate, k_shift, k_scale):
    b, s, d = x.shape
    ts = _tile(s, 256)
    row = pl.BlockSpec((1, ts, d), lambda bb, i: (bb, i, 0))
    vec = pl.BlockSpec((1, d), lambda bb, i: (0, 0))
    return pl.pallas_call(
        _post_pre_kernel,
        out_shape=(jax.ShapeDtypeStruct((b, s, d), F32), jax.ShapeDtypeStruct((b, s, d), BF16)),
        grid=(b, s // ts),
        in_specs=[row, row, vec, _mod_spec(d, k_gate), vec, _mod_spec(d, k_scale),
                  _mod_spec(d, k_shift)],
        out_specs=(row, row),
        compiler_params=_cparams(("arbitrary", "arbitrary")),
        name="post_pre_norm",
    )(y, x, gpost.reshape(1, d), mod3, gpre.reshape(1, d), mod3, mod3)


def _post_kernel(y_ref, x_ref, gpost_ref, gate_ref, o_ref):
    o_ref[0] = x_ref[0] + gate_ref[0] * _rms(y_ref[0], gpost_ref[...])


def _post(y, x, gpost, mod3, k_gate):
    b, s, d = x.shape
    ts = _tile(s, 256)
    row = pl.BlockSpec((1, ts, d), lambda bb, i: (bb, i, 0))
    return pl.pallas_call(
        _post_kernel,
        out_shape=jax.ShapeDtypeStruct((b, s, d), F32),
        grid=(b, s // ts),
        in_specs=[row, row, pl.BlockSpec((1, d), lambda bb, i: (0, 0)), _mod_spec(d, k_gate)],
        out_specs=row,
        compiler_params=_cparams(("arbitrary", "arbitrary")),
        name="post_norm",
    )(y, x, gpost.reshape(1, d), mod3)


def _rope_proj_kernel(*refs, full_rope_tiles, partial_chunks, n_casts):
    a_ref, w_ref, cos_ref, sin_ref = refs[:4]
    cast_in = refs[4:4 + n_casts]
    o_ref = refs[4 + n_casts]
    cast_out = refs[5 + n_casts:]
    for src, dst in zip(cast_in, cast_out):
        dst[...] = src[...].astype(dst.dtype)
    j = pl.program_id(1)
    acc = jnp.dot(a_ref[...], w_ref[...], preferred_element_type=F32)
    cos, sin = cos_ref[...], sin_ref[...]
    head = j <= full_rope_tiles
    tail = j < full_rope_tiles
    tables = {True: (jnp.where(head, cos, 1.0), jnp.where(head, sin, 0.0)),
              False: (jnp.where(tail, cos, 1.0), jnp.where(tail, sin, 0.0))}
    for c in range(acc.shape[1] // LANES):
        cols = slice(c * LANES, (c + 1) * LANES)
        cos_c, sin_c = tables[c < partial_chunks]
        blk = acc[:, cols]
        o_ref[:, cols] = (blk * cos_c + pltpu.roll(blk, HEAD_DIM // 2, 1) * sin_c).astype(o_ref.dtype)


def _rope_proj(h, w, cos_full, sin_signed, *, n_rope, seq, casts=()):
    t, d = h.shape
    n = w.shape[1]
    tm = _tile(seq, 1024)
    tn = _tile(n, 1024)
    assert n_rope % LANES == 0 and tn % LANES == 0
    n_seq_tiles = seq // tm
    ni, nj = t // tm, n // tn
    cast_in_specs, cast_out_specs, cast_args, cast_shapes = _cast_plan(casts, ni, nj)
    kern = functools.partial(_rope_proj_kernel, full_rope_tiles=n_rope // tn,
                             partial_chunks=(n_rope % tn) // LANES, n_casts=len(casts))
    outs = pl.pallas_call(
        kern,
        out_shape=[jax.ShapeDtypeStruct((t, n), BF16)] + cast_shapes,
        grid=(ni, nj),
        in_specs=[
            pl.BlockSpec((tm, d), lambda i, j: (i, 0)),
            pl.BlockSpec((d, tn), lambda i, j: (0, j)),
            pl.BlockSpec((tm, HEAD_DIM), lambda i, j: (i % n_seq_tiles, 0)),
            pl.BlockSpec((tm, HEAD_DIM), lambda i, j: (i % n_seq_tiles, 0)),
        ] + cast_in_specs,
        out_specs=[pl.BlockSpec((tm, tn), lambda i, j: (i, j))] + cast_out_specs,
        compiler_params=_cparams(("arbitrary", "arbitrary")),
        name="swa_proj",
    )(h, w, cos_full, sin_signed, *cast_args)
    return outs[0], [o.reshape(-1, o.shape[-1]) for o in outs[1:]]


def _swa_kernel(sink_ref, q_ref, kp_ref, k_ref, vp_ref, v_ref, o_ref, p_ref, *, tq):
    i = pl.program_id(1)
    kvh = pl.program_id(2)
    w = SWA_WINDOW
    rows = SWA_GROUP * w
    qi = lax.broadcasted_iota(jnp.int32, (rows, 2 * w), 0) % w
    kj = lax.broadcasted_iota(jnp.int32, (rows, 2 * w), 1)
    band = jnp.logical_and(kj > qi, kj <= qi + w)
    first = jnp.logical_and(band, jnp.logical_or(kj >= w, i > 0))
    sink_fill = jnp.concatenate(
        [jnp.full((w, 2 * w), sink_ref[kvh * SWA_GROUP + g], F32) for g in range(SWA_GROUP)], axis=0)
    masked_fill = jnp.where(kj == 0, sink_fill, -jnp.inf)
    key_row = lax.broadcasted_iota(jnp.int32, (2 * w, HEAD_DIM), 0)
    ones = jnp.ones((2 * w, HEAD_DIM), BF16)
    scale = HEAD_DIM ** -0.5
    n_sub = tq // w
    for s in range(n_sub):
        q = jnp.concatenate(
            [q_ref[s * w:(s + 1) * w, g * HEAD_DIM:(g + 1) * HEAD_DIM] for g in range(SWA_GROUP)],
            axis=0)
        if s == 0:
            k = jnp.concatenate([kp_ref[...], k_ref[0:w, :]], axis=0)
            valid = first
        else:
            k = k_ref[(s - 1) * w:(s + 1) * w, :]
            valid = band
        sc = lax.dot_general(q, k, (((1,), (1,)), ((), ())), preferred_element_type=F32) * scale
        sc = jnp.where(valid, sc, masked_fill)
        p_ref[s] = jnp.exp(sc - jnp.max(sc, axis=-1, keepdims=True)).astype(BF16)
    for s in range(n_sub):
        if s == 0:
            v = jnp.concatenate([vp_ref[...], v_ref[0:w, :]], axis=0)
        else:
            v = v_ref[(s - 1) * w:(s + 1) * w, :]
        v = jnp.where(key_row == 0, jnp.zeros_like(v), v)
        ov = jnp.dot(p_ref[s], jnp.concatenate([v, ones], axis=1), preferred_element_type=F32)
        o = ov[:, :HEAD_DIM] / ov[:, HEAD_DIM:]
        for g in range(SWA_GROUP):
            o_ref[s * w:(s + 1) * w, g * HEAD_DIM:(g + 1) * HEAD_DIM] = (
                o[g * w:(g + 1) * w, :].astype(o_ref.dtype))


def _swa(proj, sinks, *, batch, seq, q_heads, k_col, v_col):
    kv_heads = q_heads // SWA_GROUP
    w = SWA_WINDOW
    tq = _tile(seq, 512)
    nq = seq // tq
    sub = tq // w
    gw = SWA_GROUP * HEAD_DIM
    kb, vb = k_col // HEAD_DIM, v_col // HEAD_DIM

    def prev_rows(b, i):
        return b * (seq // w) + jnp.maximum(i * sub - 1, 0)

    grid_spec = pltpu.PrefetchScalarGridSpec(
        num_scalar_prefetch=1,
        grid=(batch, nq, kv_heads),
        in_specs=[
            pl.BlockSpec((tq, gw), lambda b, i, h, s: (b * nq + i, h)),
            pl.BlockSpec((w, HEAD_DIM), lambda b, i, h, s: (prev_rows(b, i), kb + h)),
            pl.BlockSpec((tq, HEAD_DIM), lambda b, i, h, s: (b * nq + i, kb + h)),
            pl.BlockSpec((w, HEAD_DIM), lambda b, i, h, s: (prev_rows(b, i), vb + h)),
            pl.BlockSpec((tq, HEAD_DIM), lambda b, i, h, s: (b * nq + i, vb + h)),
        ],
        out_specs=pl.BlockSpec((tq, gw), lambda b, i, h, s: (b * nq + i, h)),
        scratch_shapes=[pltpu.VMEM((sub, SWA_GROUP * w, 2 * w), BF16)],
    )
    return pl.pallas_call(
        functools.partial(_swa_kernel, tq=tq),
        out_shape=jax.ShapeDtypeStruct((batch * seq, q_heads * HEAD_DIM), BF16),
        grid_spec=grid_spec,
        compiler_params=_cparams(("arbitrary", "arbitrary", "arbitrary")),
        name="swa_attention",
    )(sinks, proj, proj, proj, proj, proj)


def _sb_kernel(q_ref, k_ref, v_ref, o_ref, hilo0_ref, hilo1_ref, lb0_ref, lb1_ref, acc_ref, r_ref,
               *, tq, hp):
    hilo_refs = (hilo0_ref, hilo1_ref)
    lb_refs = (lb0_ref, lb1_ref)
    i = pl.program_id(2)
    row = lax.broadcasted_iota(jnp.int32, (tq, tq), 0)
    col = lax.broadcasted_iota(jnp.int32, (tq, tq), 1)
    strict_lower = col < row
    neg_later = jnp.where(row > col, -1.0, 0.0).astype(BF16)
    neg_later = jnp.concatenate([neg_later, neg_later], axis=0)
    z_scale = HEAD_DIM ** -0.5 * LOG2E
    sign_bit = jnp.uint32(0x80000000)

    def logits_stage(slot, j, diag, heads=range(hp)):
        start = pl.multiple_of(j * tq, tq)
        for h in heads:
            cols = slice(h * HEAD_DIM, (h + 1) * HEAD_DIM)
            q = q_ref[:, cols]
            k = k_ref[pl.ds(start, tq), cols]
            z = lax.dot_general(q, k, (((1,), (1,)), ((), ())),
                                preferred_element_type=F32) * z_scale
            neg_abs = lax.bitcast_convert_type(
                lax.bitcast_convert_type(z, jnp.uint32) | sign_bit, F32)
            softplus = jnp.maximum(z, 0.0) + jnp.log(1.0 + jnp.exp2(neg_abs)) * LOG2E
            log_beta = z - softplus
            if diag:
                lb_refs[slot][h] = jnp.where(strict_lower, log_beta, -jnp.inf)
                softplus = jnp.where(strict_lower, softplus, 0.0)
                r_ref[h] = -jnp.sum(softplus, axis=-1, keepdims=True)
            else:
                r_sum = r_ref[h]
                lb_refs[slot][h] = log_beta + r_sum
                r_ref[h] = r_sum - jnp.sum(softplus, axis=-1, keepdims=True)
            hi = softplus.astype(BF16)
            lo = (softplus - hi.astype(F32)).astype(BF16)
            hilo_refs[slot][h] = jnp.concatenate([hi, lo], axis=1)

    def weights_stage(slot, j, first, heads=range(hp)):
        start = pl.multiple_of(j * tq, tq)
        for h in heads:
            cols = slice(h * HEAD_DIM, (h + 1) * HEAD_DIM)
            v = v_ref[pl.ds(start, tq), cols]
            suffix = jnp.dot(hilo_refs[slot][h], neg_later, preferred_element_type=F32)
            att = jnp.exp2(lb_refs[slot][h] + suffix)
            pv = jnp.dot(att.astype(BF16), v, preferred_element_type=F32)
            if first:
                acc_ref[h] = pv
            else:
                acc_ref[h] += pv

    def any_row_alive():
        top = functools.reduce(
            jnp.maximum, [jnp.max(r_ref[h], axis=0, keepdims=True) for h in range(hp)])
        return top[0, 0] > SB_UNDERFLOW_LOG2

    def staggered(logits_stages, weights_stages):
        lag = 2
        for k, stage in enumerate(logits_stages):
            stage()
            if k >= lag:
                weights_stages[k - lag]()
        for stage in weights_stages[max(len(logits_stages) - lag, 0):]:
            stage()

    def per_head(stage, *args):
        return [functools.partial(stage, *args, heads=(h,)) for h in range(hp)]

    @pl.when(i == 0)
    def _():
        staggered(per_head(logits_stage, 0, i, True), per_head(weights_stage, 0, i, True))

    @pl.when(i > 0)
    def _():
        staggered(per_head(logits_stage, 0, i, True) + per_head(logits_stage, 1, i - 1, False),
                  per_head(weights_stage, 0, i, True) + per_head(weights_stage, 1, i - 1, False))

        def cond(state):
            n, alive = state
            return jnp.logical_and(n <= i, alive)

        def body(state):
            n, _ = state
            staggered(per_head(logits_stage, 1, i - n, False),
                      per_head(weights_stage, 1, i - n, False))
            return n + 1, any_row_alive()

        lax.while_loop(cond, body, (jnp.int32(2), any_row_alive()))

    for h in range(hp):
        o_ref[:, h * HEAD_DIM:(h + 1) * HEAD_DIM] = acc_ref[h].astype(o_ref.dtype)


def _sb(proj, *, batch, seq, heads, q_col, k_col, v_col):
    tq = _tile(seq, 256)
    nq = seq // tq
    hp = 8 if heads % 8 == 0 else 1
    hw = hp * HEAD_DIM
    assert q_col % hw == 0 and k_col % hw == 0 and v_col % hw == 0
    qb, kb, vb = q_col // hw, k_col // hw, v_col // hw
    return pl.pallas_call(
        functools.partial(_sb_kernel, tq=tq, hp=hp),
        out_shape=jax.ShapeDtypeStruct((batch * seq, heads * HEAD_DIM), BF16),
        grid=(batch, heads // hp, nq),
        in_specs=[
            pl.BlockSpec((tq, hw), lambda b, h, i: (b * nq + i, qb + h)),
            pl.BlockSpec((seq, hw), lambda b, h, i: (b, kb + h)),
            pl.BlockSpec((seq, hw), lambda b, h, i: (b, vb + h)),
        ],
        out_specs=pl.BlockSpec((tq, hw), lambda b, h, i: (b * nq + i, h)),
        scratch_shapes=[pltpu.VMEM((hp, tq, 2 * tq), BF16), pltpu.VMEM((hp, tq, 2 * tq), BF16),
                        pltpu.VMEM((hp, tq, tq), F32), pltpu.VMEM((hp, tq, tq), F32),
                        pltpu.VMEM((hp, tq, HEAD_DIM), F32), pltpu.VMEM((hp, tq, 1), F32)],
        compiler_params=_cparams(("arbitrary", "arbitrary", "arbitrary")),
        name="sb_attention",
    )(proj, proj, proj)


def _merge_kernel(a1_ref, a2_ref, w1_ref, w2_ref, ga_ref, gb_ref, o_ref):
    y1 = jnp.dot(a1_ref[...], w1_ref[...], preferred_element_type=F32)
    y2 = jnp.dot(a2_ref[...], w2_ref[...], preferred_element_type=F32)
    o = ga_ref[...].astype(F32) * y1 + gb_ref[...].astype(F32) * y2
    o_ref[...] = o.astype(o_ref.dtype)


def _merge(a_swa, a_sb, w_swa, w_sb, gates):
    t, k1 = a_swa.shape
    k2 = a_sb.shape[1]
    d = w_swa.shape[1]
    tm = _tile(t, 1024)
    tn = _tile(d, 1024)
    ga, gb = 0, d // tn
    return pl.pallas_call(
        _merge_kernel,
        out_shape=jax.ShapeDtypeStruct((t, d), BF16),
        grid=(t // tm, d // tn),
        in_specs=[
            pl.BlockSpec((tm, k1), lambda i, j: (i, 0)),
            pl.BlockSpec((tm, k2), lambda i, j: (i, 0)),
            pl.BlockSpec((k1, tn), lambda i, j: (0, j)),
            pl.BlockSpec((k2, tn), lambda i, j: (0, j)),
            pl.BlockSpec((tm, tn), lambda i, j: (i, ga + j)),
            pl.BlockSpec((tm, tn), lambda i, j: (i, gb + j)),
        ],
        out_specs=pl.BlockSpec((tm, tn), lambda i, j: (i, j)),
        compiler_params=_cparams(("arbitrary", "arbitrary")),
        name="branch_merge",
    )(a_swa, a_sb, w_swa, w_sb, gates, gates)


def _mm_kernel(*refs, epilogue, has_bias, n_casts):
    a_ref, w_ref = refs[:2]
    b_ref = refs[2] if has_bias else None
    first_cast = 2 + has_bias
    cast_in = refs[first_cast:first_cast + n_casts]
    o_ref = refs[first_cast + n_casts]
    cast_out = refs[first_cast + n_casts + 1:]
    acc = jnp.dot(a_ref[...], w_ref[...], preferred_element_type=F32)
    if has_bias:
        acc = acc + b_ref[...]
    if epilogue == "relu_sq":
        acc = jnp.square(jnp.maximum(acc, 0.0))
    elif epilogue == "sigmoid":
        acc = jax.nn.sigmoid(acc)
    o_ref[...] = acc.astype(o_ref.dtype)
    for src, dst in zip(cast_in, cast_out):
        dst[...] = src[...].astype(dst.dtype)


def _cast_plan(casts, ni, nj):
    steps = ni * nj
    in_specs, out_specs, args, shapes = [], [], [], []
    for c in casts:
        c, col0, cols = c if isinstance(c, tuple) else (c, 0, c.shape[1])
        rows = c.shape[0]
        if col0 == 0 and cols == c.shape[1] and rows % (steps * BF16_SUBLANES) == 0:
            slab = rows // steps
            spec = pl.BlockSpec((1, slab, cols), lambda i, j: (i * nj + j, 0, 0))
            in_specs.append(spec)
            out_specs.append(spec)
            args.append(c.reshape(steps, slab, cols))
            shapes.append(jax.ShapeDtypeStruct((steps, slab, cols), BF16))
        else:
            slab = cols // steps
            assert cols % (steps * LANES) == 0 and col0 % slab == 0, (c.shape, col0, cols, steps)
            first = col0 // slab
            in_specs.append(pl.BlockSpec((rows, slab), lambda i, j: (0, first + i * nj + j)))
            out_specs.append(pl.BlockSpec((rows, slab), lambda i, j: (0, i * nj + j)))
            args.append(c)
            shapes.append(jax.ShapeDtypeStruct((rows, cols), BF16))
    return in_specs, out_specs, args, shapes


def _mm(a, w, out_dtype, *, bias=None, epilogue=None, casts=(), tn_pref=1024, name):
    t, k = a.shape
    n = w.shape[1]
    tm = _tile(t, 1024)
    tn = tn_pref if n % tn_pref == 0 else _tile(n, 1024)
    ni, nj = t // tm, n // tn
    in_specs = [pl.BlockSpec((tm, k), lambda i, j: (i, 0)),
                pl.BlockSpec((k, tn), lambda i, j: (0, j))]
    args = [a, w]
    if bias is not None:
        in_specs.append(pl.BlockSpec((1, tn), lambda i, j: (0, j)))
        args.append(bias.reshape(1, n))
    cast_in_specs, cast_out_specs, cast_args, cast_shapes = _cast_plan(casts, ni, nj)
    in_specs += cast_in_specs
    args += cast_args
    out_specs = [pl.BlockSpec((tm, tn), lambda i, j: (i, j))] + cast_out_specs
    out_shape = [jax.ShapeDtypeStruct((t, n), out_dtype)] + cast_shapes
    outs = pl.pallas_call(
        functools.partial(_mm_kernel, epilogue=epilogue, has_bias=bias is not None,
                          n_casts=len(casts)),
        out_shape=out_shape,
        grid=(ni, nj),
        in_specs=in_specs,
        out_specs=out_specs,
        compiler_params=_cparams(("arbitrary", "arbitrary")),
        name=name,
    )(*args)
    return outs[0], [o.reshape(-1, o.shape[-1]) for o in outs[1:]]


def _mm_ksplit_kernel(a_ref, w_ref, o_ref):
    @pl.when(pl.program_id(2) == 0)
    def _():
        o_ref[...] = jnp.zeros_like(o_ref)

    o_ref[...] += jnp.dot(a_ref[...], w_ref[...], preferred_element_type=F32)


def _mm_ksplit(a, w, *, name):
    t, k = a.shape
    n = w.shape[1]
    tm = _tile(t, 1024)
    tn = _tile(n, 1024)
    tk = _tile(k, 4096)
    return pl.pallas_call(
        _mm_ksplit_kernel,
        out_shape=jax.ShapeDtypeStruct((t, n), F32),
        grid=(t // tm, n // tn, k // tk),
        in_specs=[pl.BlockSpec((tm, tk), lambda i, j, kk: (i, kk)),
                  pl.BlockSpec((tk, tn), lambda i, j, kk: (kk, j))],
        out_specs=pl.BlockSpec((tm, tn), lambda i, j, kk: (i, j)),
        compiler_params=_cparams(("arbitrary", "arbitrary", "arbitrary")),
        name=name,
    )(a, w)


def _rope_tables(seq):
    inv_freq = 1.0 / (ROPE_THETA ** (jnp.arange(0, HEAD_DIM, 2, dtype=F32) / HEAD_DIM))
    ang = jnp.arange(seq, dtype=F32)[:, None] * inv_freq[None, :]
    cos, sin = jnp.cos(ang), jnp.sin(ang)
    return jnp.concatenate([cos, cos], axis=-1), jnp.concatenate([-sin, sin], axis=-1)


def kernel(x, c, w_ada, b_ada, g_pre_mix, g_post_mix, g_pre_mlp, g_post_mlp, w_in, attn_sinks,
           w_branch_gate, b_branch_gate, w_proj_swa, w_proj_sb, w_out, w_ff_in, w_ff_out):
    batch, seq, d = x.shape
    depth = w_in.shape[0]
    d_in = w_in.shape[2]
    swa_q_heads = attn_sinks.shape[1]
    swa_q_w = w_proj_swa.shape[1]
    sb_w = w_proj_sb.shape[1]
    swa_kv_w = (d_in - swa_q_w - 3 * sb_w) // 2
    sb_heads = sb_w // HEAD_DIM
    assert swa_q_w == swa_q_heads * HEAD_DIM and swa_kv_w * SWA_GROUP == swa_q_w
    assert seq % SWA_WINDOW == 0
    t = batch * seq

    cos_full, sin_signed = _rope_tables(seq)
    pad_rows = (-batch) % 8
    c_pad = jnp.pad(c, ((0, pad_rows), (0, 0)))

    swa_w = swa_q_w + 2 * swa_kv_w
    n_rope = swa_q_w + swa_kv_w

    for l in range(depth):
        mod = _ada_mod(c_pad, w_ada[l], b_ada[l])[:batch]
        mod3 = mod.reshape(batch * N_MOD, 1, d)

        h = _prenorm(x, g_pre_mix[l], mod3, 0, 1).reshape(t, d)
        proj_swa, (w_sb_in_bf,) = _rope_proj(
            h, w_in[l][:, :swa_w].astype(BF16), cos_full, sin_signed, n_rope=n_rope, seq=seq,
            casts=((w_in[l], swa_w, d_in - swa_w),))
        proj_sb, (w_gate_bf,) = _mm(h, w_sb_in_bf, BF16, casts=(w_branch_gate[l],),
                                    tn_pref=768, name="sb_proj")
        gates, (w_swa_bf, w_sb_bf, w_out_bf, w_ff_in_bf) = _mm(
            h, w_gate_bf, BF16, bias=b_branch_gate[l], epilogue="sigmoid",
            casts=(w_proj_swa[l], w_proj_sb[l], w_out[l], w_ff_in[l]), name="gate_proj")
        a_swa = _swa(proj_swa, attn_sinks[l], batch=batch, seq=seq, q_heads=swa_q_heads,
                     k_col=swa_q_w, v_col=n_rope)
        a_sb = _sb(proj_sb, batch=batch, seq=seq, heads=sb_heads,
                   q_col=0, k_col=sb_w, v_col=2 * sb_w)
        merged = _merge(a_swa, a_sb, w_swa_bf, w_sb_bf, gates)
        mixed, _ = _mm(merged, w_out_bf, F32, name="out_proj")
        x1, h2 = _post_pre(mixed.reshape(batch, seq, d), x, g_post_mix[l], g_pre_mlp[l], mod3,
                           2, 3, 4)
        act, (w_ff_out_bf,) = _mm(h2.reshape(t, d), w_ff_in_bf, BF16, epilogue="relu_sq",
                                  casts=(w_ff_out[l],), name="ff_in")
        f = _mm_ksplit(act, w_ff_out_bf, name="ff_out").reshape(batch, seq, d)
        x = _post(f, x1, g_post_mlp[l], mod3, 5)
    return x
```

```python
import functools
import math

import jax
import jax.numpy as jnp
from jax import lax
from jax.experimental import pallas as pl
from jax.experimental.pallas import tpu as pltpu

F32 = jnp.float32
BF16 = jnp.bfloat16

HEAD_DIM = 128
LANES = 128
BF16_SUBLANES = 16
SWA_WINDOW = 128
SWA_GROUP = 4
ROPE_THETA = 10000.0
NORM_EPS = 1e-6
LOG2E = 1.4426950408889634
SB_UNDERFLOW_LOG2 = -152.0
N_MOD = 6
MIB = 1024 * 1024
VMEM_LIMIT_BYTES = 56 * MIB


def _tile(dim, pref):
    t = min(dim, pref)
    while dim % t:
        t //= 2
    return t


def _cparams(semantics, vmem=VMEM_LIMIT_BYTES, flags=None):
    return pltpu.CompilerParams(dimension_semantics=semantics, vmem_limit_bytes=vmem, flags=flags)


def _ada_kernel(c_ref, w_ref, b_ref, o_ref):
    c = c_ref[...]
    act = (c * jax.nn.sigmoid(c)).astype(BF16)
    o_ref[...] = jnp.dot(act, w_ref[...].astype(BF16), preferred_element_type=F32) + b_ref[...]


def _ada_mod(c_pad, w_ada, b_ada):
    rows, d = c_pad.shape
    n = w_ada.shape[1]
    tn = _tile(n, 1024)
    return pl.pallas_call(
        _ada_kernel,
        out_shape=jax.ShapeDtypeStruct((rows, n), F32),
        grid=(n // tn,),
        in_specs=[
            pl.BlockSpec((rows, d), lambda j: (0, 0)),
            pl.BlockSpec((d, tn), lambda j: (0, j)),
            pl.BlockSpec((1, tn), lambda j: (0, j)),
        ],
        out_specs=pl.BlockSpec((rows, tn), lambda j: (0, j)),
        compiler_params=_cparams(("arbitrary",)),
        name="ada_mod",
    )(c_pad, w_ada, b_ada.reshape(1, n))


def _rms(x, g):
    y = x * lax.rsqrt(jnp.mean(x * x, axis=-1, keepdims=True) + NORM_EPS)
    return y * g


def _prenorm_kernel(x_ref, g_ref, scale_ref, shift_ref, h_ref):
    h = _rms(x_ref[0], g_ref[...]) * (1.0 + scale_ref[0]) + shift_ref[0]
    h_ref[0] = h.astype(h_ref.dtype)


def _mod_spec(d, k):
    return pl.BlockSpec((1, 1, d), lambda b, i: (b * N_MOD + k, 0, 0))


def _prenorm(x, g, mod3, k_shift, k_scale):
    b, s, d = x.shape
    ts = _tile(s, 512)
    row = pl.BlockSpec((1, ts, d), lambda bb, i: (bb, i, 0))
    return pl.pallas_call(
        _prenorm_kernel,
        out_shape=jax.ShapeDtypeStruct((b, s, d), BF16),
        grid=(b, s // ts),
        in_specs=[row, pl.BlockSpec((1, d), lambda bb, i: (0, 0)),
                  _mod_spec(d, k_scale), _mod_spec(d, k_shift)],
        out_specs=row,
        compiler_params=_cparams(("arbitrary", "arbitrary")),
        name="prenorm",
    )(x, g.reshape(1, d), mod3, mod3)


def _post_pre_math(y, x, gpost, gate, gpre, scale, shift):
    x1 = x + gate * _rms(y, gpost)
    h = _rms(x1, gpre) * (1.0 + scale) + shift
    return x1, h


def _first_h2_kernel(y_ref, x_ref, gpost_ref, gate_ref, gpre_ref, scale_ref, shift_ref, h_ref):
    _, h = _post_pre_math(y_ref[...], x_ref[...], gpost_ref[...], gate_ref[0], gpre_ref[...],
                          scale_ref[0], shift_ref[0])
    h_ref[...] = h.astype(h_ref.dtype)


def _first_h2(y, x, gpost, gpre, mod3, rows, k_gate, k_shift, k_scale):
    d = x.shape[1]
    ts = _tile(rows, 256)
    row = pl.BlockSpec((ts, d), lambda i: (i, 0))
    vec = pl.BlockSpec((1, d), lambda i: (0, 0))

    def mod(k):
        return pl.BlockSpec((1, 1, d), lambda i: (k, 0, 0))

    return pl.pallas_call(
        _first_h2_kernel,
        out_shape=jax.ShapeDtypeStruct((rows, d), BF16),
        grid=(rows // ts,),
        in_specs=[row, row, vec, mod(k_gate), vec, mod(k_scale), mod(k_shift)],
        out_specs=row,
        compiler_params=_cparams(("arbitrary",)),
        name="first_h2",
    )(y, x, gpost.reshape(1, d), mod3, gpre.reshape(1, d), mod3, mod3)


def _mlp_in_kernel(*refs, chunk, n_casts):
    (y_ref, x_ref, gpost_ref, gate_ref, gpre_ref, scale_ref, shift_ref, h_first_hbm,
     w_ref) = refs[:9]
    cast_in = refs[9:9 + n_casts]
    act_ref, x1_ref = refs[9 + n_casts:11 + n_casts]
    cast_out = refs[11 + n_casts:11 + 2 * n_casts]
    h_even, h_odd, sem = refs[11 + 2 * n_casts:]
    i = pl.program_id(0)
    j = pl.program_id(1)

    @pl.when(jnp.logical_and(i == 0, j == 0))
    def _():
        first = pltpu.make_async_copy(h_first_hbm, h_even, sem)
        first.start()
        first.wait()

    for src, dst in zip(cast_in, cast_out):
        dst[...] = src[...].astype(dst.dtype)

    def step(h_cur, h_next):
        x1, h = _post_pre_math(y_ref[...], x_ref[...], gpost_ref[...], gate_ref[0], gpre_ref[...],
                               scale_ref[0], shift_ref[0])
        x1_ref[...] = x1
        h_next[pl.ds(pl.multiple_of(j * chunk, chunk), chunk), :] = h.astype(h_next.dtype)
        acc = jnp.dot(h_cur[...], w_ref[...], preferred_element_type=F32)
        act_ref[...] = jnp.square(jnp.maximum(acc, 0.0)).astype(act_ref.dtype)

    @pl.when(i % 2 == 0)
    def _():
        step(h_even, h_odd)

    @pl.when(i % 2 == 1)
    def _():
        step(h_odd, h_even)


def _mlp_in(y, x, gpost, gpre, mod3, w, *, seq, k_gate, k_shift, k_scale, casts=()):
    t, d = x.shape
    n = w.shape[1]
    tm = _tile(seq, 1024)
    tn = _tile(n, 1024)
    ni, nj = t // tm, n // tn
    chunk = tm // nj
    assert tm % nj == 0 and chunk % BF16_SUBLANES == 0
    blocks_per_batch = seq // tm

    def nxt(i):
        return (i + 1) % ni

    rows = pl.BlockSpec((chunk, d), lambda i, j: (nxt(i) * nj + j, 0))
    vec = pl.BlockSpec((1, d), lambda i, j: (0, 0))

    def mod(k):
        return pl.BlockSpec((1, 1, d),
                            lambda i, j: ((nxt(i) // blocks_per_batch) * N_MOD + k, 0, 0))

    h_first = _first_h2(y, x, gpost, gpre, mod3, tm, k_gate, k_shift, k_scale)
    cast_in_specs, cast_out_specs, cast_args, cast_shapes = _cast_plan(casts, ni, nj)
    outs = pl.pallas_call(
        functools.partial(_mlp_in_kernel, chunk=chunk, n_casts=len(casts)),
        out_shape=[jax.ShapeDtypeStruct((t, n), BF16), jax.ShapeDtypeStruct((t, d), F32)]
        + cast_shapes,
        grid=(ni, nj),
        in_specs=[rows, rows, vec, mod(k_gate), vec, mod(k_scale), mod(k_shift),
                  pl.BlockSpec(memory_space=pl.ANY),
                  pl.BlockSpec((d, tn), lambda i, j: (0, j))] + cast_in_specs,
        out_specs=[pl.BlockSpec((tm, tn), lambda i, j: (i, j)), rows] + cast_out_specs,
        scratch_shapes=[pltpu.VMEM((tm, d), BF16), pltpu.VMEM((tm, d), BF16),
                        pltpu.SemaphoreType.DMA(())],
        compiler_params=_cparams(("arbitrary", "arbitrary")),
        name="ff_in",
    )(y, x, gpost.reshape(1, d), mod3, gpre.reshape(1, d), mod3, mod3, h_first, w, *cast_args)
    return outs[0], outs[1], [o.reshape(-1, o.shape[-1]) for o in outs[2:]]


def _post_kernel(y_ref, x_ref, gpost_ref, gate_ref, o_ref):
    o_ref[0] = x_ref[0] + gate_ref[0] * _rms(y_ref[0], gpost_ref[...])


def _post(y, x, gpost, mod3, k_gate):
    b, s, d = x.shape
    ts = _tile(s, 512)
    row = pl.BlockSpec((1, ts, d), lambda bb, i: (bb, i, 0))
    return pl.pallas_call(
        _post_kernel,
        out_shape=jax.ShapeDtypeStruct((b, s, d), F32),
        grid=(b, s // ts),
        in_specs=[row, row, pl.BlockSpec((1, d), lambda bb, i: (0, 0)), _mod_spec(d, k_gate)],
        out_specs=row,
        compiler_params=_cparams(("arbitrary", "arbitrary")),
        name="post_norm",
    )(y, x, gpost.reshape(1, d), mod3)


def _rope_proj_kernel(*refs, full_rope_tiles, partial_chunks, n_casts):
    a_ref, w_ref, cos_ref, sin_ref = refs[:4]
    cast_in = refs[4:4 + n_casts]
    o_ref = refs[4 + n_casts]
    cast_out = refs[5 + n_casts:]
    for src, dst in zip(cast_in, cast_out):
        dst[...] = src[...].astype(dst.dtype)
    j = pl.program_id(1)
    acc = jnp.dot(a_ref[...], w_ref[...], preferred_element_type=F32)
    cos, sin = cos_ref[...], sin_ref[...]
    head = j <= full_rope_tiles
    tail = j < full_rope_tiles
    tables = {True: (jnp.where(head, cos, 1.0), jnp.where(head, sin, 0.0)),
              False: (jnp.where(tail, cos, 1.0), jnp.where(tail, sin, 0.0))}
    for c in range(acc.shape[1] // LANES):
        cols = slice(c * LANES, (c + 1) * LANES)
        cos_c, sin_c = tables[c < partial_chunks]
        blk = acc[:, cols]
        o_ref[:, cols] = (blk * cos_c + pltpu.roll(blk, HEAD_DIM // 2, 1) * sin_c).astype(o_ref.dtype)


def _rope_proj(h, w, cos_full, sin_signed, *, n_rope, seq, casts=()):
    t, d = h.shape
    n = w.shape[1]
    tm = _tile(seq, 1024)
    tn = _tile(n, 1024)
    assert n_rope % LANES == 0 and tn % LANES == 0
    n_seq_tiles = seq // tm
    ni, nj = t // tm, n // tn
    cast_in_specs, cast_out_specs, cast_args, cast_shapes = _cast_plan(casts, ni, nj)
    kern = functools.partial(_rope_proj_kernel, full_rope_tiles=n_rope // tn,
                             partial_chunks=(n_rope % tn) // LANES, n_casts=len(casts))
    outs = pl.pallas_call(
        kern,
        out_shape=[jax.ShapeDtypeStruct((t, n), BF16)] + cast_shapes,
        grid=(ni, nj),
        in_specs=[
            pl.BlockSpec((tm, d), lambda i, j: (i, 0)),
            pl.BlockSpec((d, tn), lambda i, j: (0, j)),
            pl.BlockSpec((tm, HEAD_DIM), lambda i, j: (i % n_seq_tiles, 0)),
            pl.BlockSpec((tm, HEAD_DIM), lambda i, j: (i % n_seq_tiles, 0)),
        ] + cast_in_specs,
        out_specs=[pl.BlockSpec((tm, tn), lambda i, j: (i, j))] + cast_out_specs,
        compiler_params=_cparams(("arbitrary", "arbitrary")),
        name="swa_proj",
    )(h, w, cos_full, sin_signed, *cast_args)
    return outs[0], [o.reshape(-1, o.shape[-1]) for o in outs[1:]]


def _swa_kernel(sink_ref, q_ref, kp_ref, k_ref, vp_ref, v_ref, o_ref, p_ref, *, tq):
    i = pl.program_id(1)
    kvh = pl.program_id(2)
    w = SWA_WINDOW
    rows = SWA_GROUP * w
    qi = lax.broadcasted_iota(jnp.int32, (rows, 2 * w), 0) % w
    kj = lax.broadcasted_iota(jnp.int32, (rows, 2 * w), 1)
    band = jnp.logical_and(kj > qi, kj <= qi + w)
    first = jnp.logical_and(band, jnp.logical_or(kj >= w, i > 0))
    sink_fill = jnp.concatenate(
        [jnp.full((w, 2 * w), sink_ref[kvh * SWA_GROUP + g], F32) for g in range(SWA_GROUP)], axis=0)
    masked_fill = jnp.where(kj == 0, sink_fill, -jnp.inf)
    key_row = lax.broadcasted_iota(jnp.int32, (2 * w, HEAD_DIM), 0)
    ones = jnp.ones((2 * w, HEAD_DIM), BF16)
    scale = HEAD_DIM ** -0.5
    n_sub = tq // w
    def probs_stage(s):
        q = jnp.concatenate(
            [q_ref[s * w:(s + 1) * w, g * HEAD_DIM:(g + 1) * HEAD_DIM] for g in range(SWA_GROUP)],
            axis=0)
        if s == 0:
            k = jnp.concatenate([kp_ref[...], k_ref[0:w, :]], axis=0)
            valid = first
        else:
            k = k_ref[(s - 1) * w:(s + 1) * w, :]
            valid = band
        sc = lax.dot_general(q, k, (((1,), (1,)), ((), ())), preferred_element_type=F32) * scale
        sc = jnp.where(valid, sc, masked_fill)
        p_ref[s] = jnp.exp(sc - jnp.max(sc, axis=-1, keepdims=True)).astype(BF16)

    def values_stage(s):
        if s == 0:
            v = jnp.concatenate([vp_ref[...], v_ref[0:w, :]], axis=0)
        else:
            v = v_ref[(s - 1) * w:(s + 1) * w, :]
        v = jnp.where(key_row == 0, jnp.zeros_like(v), v)
        ov = jnp.dot(p_ref[s], jnp.concatenate([v, ones], axis=1), preferred_element_type=F32)
        o = ov[:, :HEAD_DIM] / ov[:, HEAD_DIM:]
        for g in range(SWA_GROUP):
            o_ref[s * w:(s + 1) * w, g * HEAD_DIM:(g + 1) * HEAD_DIM] = (
                o[g * w:(g + 1) * w, :].astype(o_ref.dtype))

    for s in range(n_sub):
        probs_stage(s)
        if s >= 1:
            values_stage(s - 1)
    values_stage(n_sub - 1)


def _swa(proj, sinks, *, batch, seq, q_heads, k_col, v_col):
    kv_heads = q_heads // SWA_GROUP
    w = SWA_WINDOW
    tq = _tile(seq, 512)
    nq = seq // tq
    sub = tq // w
    gw = SWA_GROUP * HEAD_DIM
    kb, vb = k_col // HEAD_DIM, v_col // HEAD_DIM

    def prev_rows(b, i):
        return b * (seq // w) + jnp.maximum(i * sub - 1, 0)

    grid_spec = pltpu.PrefetchScalarGridSpec(
        num_scalar_prefetch=1,
        grid=(batch, nq, kv_heads),
        in_specs=[
            pl.BlockSpec((tq, gw), lambda b, i, h, s: (b * nq + i, h)),
            pl.BlockSpec((w, HEAD_DIM), lambda b, i, h, s: (prev_rows(b, i), kb + h)),
            pl.BlockSpec((tq, HEAD_DIM), lambda b, i, h, s: (b * nq + i, kb + h)),
            pl.BlockSpec((w, HEAD_DIM), lambda b, i, h, s: (prev_rows(b, i), vb + h)),
            pl.BlockSpec((tq, HEAD_DIM), lambda b, i, h, s: (b * nq + i, vb + h)),
        ],
        out_specs=pl.BlockSpec((tq, gw), lambda b, i, h, s: (b * nq + i, h)),
        scratch_shapes=[pltpu.VMEM((sub, SWA_GROUP * w, 2 * w), BF16)],
    )
    return pl.pallas_call(
        functools.partial(_swa_kernel, tq=tq),
        out_shape=jax.ShapeDtypeStruct((batch * seq, q_heads * HEAD_DIM), BF16),
        grid_spec=grid_spec,
        compiler_params=_cparams(("arbitrary", "arbitrary", "arbitrary")),
        name="swa_attention",
    )(sinks, proj, proj, proj, proj, proj)


def _sb_kernel(q_ref, k_ref, v_ref, o_ref, hilo0_ref, hilo1_ref, lb0_ref, lb1_ref, acc_ref, r_ref,
               *, tq, hp):
    hilo_refs = (hilo0_ref, hilo1_ref)
    lb_refs = (lb0_ref, lb1_ref)
    i = pl.program_id(2)
    row = lax.broadcasted_iota(jnp.int32, (tq, tq), 0)
    col = lax.broadcasted_iota(jnp.int32, (tq, tq), 1)
    strict_lower = col < row
    neg_later = jnp.where(row > col, -1.0, 0.0).astype(BF16)
    neg_later = jnp.concatenate([neg_later, neg_later], axis=0)
    z_scale = HEAD_DIM ** -0.5 * LOG2E
    sign_bit = jnp.uint32(0x80000000)

    def logits_stage(slot, j, diag, heads=range(hp)):
        start = pl.multiple_of(j * tq, tq)
        for h in heads:
            cols = slice(h * HEAD_DIM, (h + 1) * HEAD_DIM)
            q = q_ref[:, cols]
            k = k_ref[pl.ds(start, tq), cols]
            z = lax.dot_general(q, k, (((1,), (1,)), ((), ())),
                                preferred_element_type=F32) * z_scale
            neg_abs = lax.bitcast_convert_type(
                lax.bitcast_convert_type(z, jnp.uint32) | sign_bit, F32)
            softplus = jnp.maximum(z, 0.0) + jnp.log(1.0 + jnp.exp2(neg_abs)) * LOG2E
            log_beta = z - softplus
            if diag:
                lb_refs[slot][h] = jnp.where(strict_lower, log_beta, -jnp.inf)
                softplus = jnp.where(strict_lower, softplus, 0.0)
                r_ref[h] = -jnp.sum(softplus, axis=-1, keepdims=True)
            else:
                r_sum = r_ref[h]
                lb_refs[slot][h] = log_beta + r_sum
                r_ref[h] = r_sum - jnp.sum(softplus, axis=-1, keepdims=True)
            hi = softplus.astype(BF16)
            lo = (softplus - hi.astype(F32)).astype(BF16)
            hilo_refs[slot][h] = jnp.concatenate([hi, lo], axis=1)

    def weights_stage(slot, j, first, heads=range(hp)):
        start = pl.multiple_of(j * tq, tq)
        for h in heads:
            cols = slice(h * HEAD_DIM, (h + 1) * HEAD_DIM)
            v = v_ref[pl.ds(start, tq), cols]
            suffix = jnp.dot(hilo_refs[slot][h], neg_later, preferred_element_type=F32)
            att = jnp.exp2(lb_refs[slot][h] + suffix)
            pv = jnp.dot(att.astype(BF16), v, preferred_element_type=F32)
            if first:
                acc_ref[h] = pv
            else:
                acc_ref[h] += pv

    def any_row_alive():
        top = functools.reduce(
            jnp.maximum, [jnp.max(r_ref[h], axis=0, keepdims=True) for h in range(hp)])
        return top[0, 0] > SB_UNDERFLOW_LOG2

    def staggered(logits_stages, weights_stages):
        lag = 2
        for k, stage in enumerate(logits_stages):
            stage()
            if k >= lag:
                weights_stages[k - lag]()
        for stage in weights_stages[max(len(logits_stages) - lag, 0):]:
            stage()

    def per_head(stage, *args):
        return [functools.partial(stage, *args, heads=(h,)) for h in range(hp)]

    @pl.when(i == 0)
    def _():
        staggered(per_head(logits_stage, 0, i, True), per_head(weights_stage, 0, i, True))

    @pl.when(i > 0)
    def _():
        staggered(per_head(logits_stage, 0, i, True) + per_head(logits_stage, 1, i - 1, False),
                  per_head(weights_stage, 0, i, True) + per_head(weights_stage, 1, i - 1, False))

        def cond(state):
            n, alive = state
            return jnp.logical_and(n <= i, alive)

        def body(state):
            n, _ = state
            staggered(per_head(logits_stage, 1, i - n, False),
                      per_head(weights_stage, 1, i - n, False))
            return n + 1, any_row_alive()

        lax.while_loop(cond, body, (jnp.int32(2), any_row_alive()))

    for h in range(hp):
        o_ref[:, h * HEAD_DIM:(h + 1) * HEAD_DIM] = acc_ref[h].astype(o_ref.dtype)


def _sb(proj, *, batch, seq, heads, q_col, k_col, v_col):
    tq = _tile(seq, 256)
    nq = seq // tq
    hp = 8 if heads % 8 == 0 else 1
    hw = hp * HEAD_DIM
    assert q_col % hw == 0 and k_col % hw == 0 and v_col % hw == 0
    qb, kb, vb = q_col // hw, k_col // hw, v_col // hw
    return pl.pallas_call(
        functools.partial(_sb_kernel, tq=tq, hp=hp),
        out_shape=jax.ShapeDtypeStruct((batch * seq, heads * HEAD_DIM), BF16),
        grid=(batch, heads // hp, nq),
        in_specs=[
            pl.BlockSpec((tq, hw), lambda b, h, i: (b * nq + i, qb + h)),
            pl.BlockSpec((seq, hw), lambda b, h, i: (b, kb + h)),
            pl.BlockSpec((seq, hw), lambda b, h, i: (b, vb + h)),
        ],
        out_specs=pl.BlockSpec((tq, hw), lambda b, h, i: (b * nq + i, h)),
        scratch_shapes=[pltpu.VMEM((hp, tq, 2 * tq), BF16), pltpu.VMEM((hp, tq, 2 * tq), BF16),
                        pltpu.VMEM((hp, tq, tq), F32), pltpu.VMEM((hp, tq, tq), F32),
                        pltpu.VMEM((hp, tq, HEAD_DIM), F32), pltpu.VMEM((hp, tq, 1), F32)],
        compiler_params=_cparams(("arbitrary", "arbitrary", "arbitrary")),
        name="sb_attention",
    )(proj, proj, proj)


def _merge_kernel(a1_ref, a2_ref, w1_ref, w2_ref, ga_ref, gb_ref, o_ref):
    y1 = jnp.dot(a1_ref[...], w1_ref[...], preferred_element_type=F32)
    y2 = jnp.dot(a2_ref[...], w2_ref[...], preferred_element_type=F32)
    o = ga_ref[...].astype(F32) * y1 + gb_ref[...].astype(F32) * y2
    o_ref[...] = o.astype(o_ref.dtype)


def _merge(a_swa, a_sb, w_swa, w_sb, gates):
    t, k1 = a_swa.shape
    k2 = a_sb.shape[1]
    d = w_swa.shape[1]
    tm = _tile(t, 1024)
    tn = _tile(d, 1024)
    ga, gb = 0, d // tn
    return pl.pallas_call(
        _merge_kernel,
        out_shape=jax.ShapeDtypeStruct((t, d), BF16),
        grid=(t // tm, d // tn),
        in_specs=[
            pl.BlockSpec((tm, k1), lambda i, j: (i, 0)),
            pl.BlockSpec((tm, k2), lambda i, j: (i, 0)),
            pl.BlockSpec((k1, tn), lambda i, j: (0, j)),
            pl.BlockSpec((k2, tn), lambda i, j: (0, j)),
            pl.BlockSpec((tm, tn), lambda i, j: (i, ga + j)),
            pl.BlockSpec((tm, tn), lambda i, j: (i, gb + j)),
        ],
        out_specs=pl.BlockSpec((tm, tn), lambda i, j: (i, j)),
        compiler_params=_cparams(("arbitrary", "arbitrary")),
        name="branch_merge",
    )(a_swa, a_sb, w_swa, w_sb, gates, gates)


def _mm_kernel(*refs, epilogue, has_bias, n_casts):
    a_ref, w_ref = refs[:2]
    b_ref = refs[2] if has_bias else None
    first_cast = 2 + has_bias
    cast_in = refs[first_cast:first_cast + n_casts]
    o_ref = refs[first_cast + n_casts]
    cast_out = refs[first_cast + n_casts + 1:]
    acc = jnp.dot(a_ref[...], w_ref[...], preferred_element_type=F32)
    if has_bias:
        acc = acc + b_ref[...]
    if epilogue == "relu_sq":
        acc = jnp.square(jnp.maximum(acc, 0.0))
    elif epilogue == "sigmoid":
        acc = 0.5 * jnp.tanh(0.5 * acc) + 0.5
    o_ref[...] = acc.astype(o_ref.dtype)
    for src, dst in zip(cast_in, cast_out):
        dst[...] = src[...].astype(dst.dtype)


def _cast_plan(casts, ni, nj):
    steps = ni * nj
    in_specs, out_specs, args, shapes = [], [], [], []
    for c in casts:
        c, col0, cols = c if isinstance(c, tuple) else (c, 0, c.shape[1])
        rows = c.shape[0]
        if col0 == 0 and cols == c.shape[1] and rows % (steps * BF16_SUBLANES) == 0:
            slab = rows // steps
            spec = pl.BlockSpec((1, slab, cols), lambda i, j: (i * nj + j, 0, 0))
            in_specs.append(spec)
            out_specs.append(spec)
            args.append(c.reshape(steps, slab, cols))
            shapes.append(jax.ShapeDtypeStruct((steps, slab, cols), BF16))
        else:
            slab = cols // steps
            assert cols % (steps * LANES) == 0 and col0 % slab == 0, (c.shape, col0, cols, steps)
            first = col0 // slab
            in_specs.append(pl.BlockSpec((rows, slab), lambda i, j: (0, first + i * nj + j)))
            out_specs.append(pl.BlockSpec((rows, slab), lambda i, j: (0, i * nj + j)))
            args.append(c)
            shapes.append(jax.ShapeDtypeStruct((rows, cols), BF16))
    return in_specs, out_specs, args, shapes


def _mm(a, w, out_dtype, *, bias=None, epilogue=None, casts=(), tn_pref=1024, name):
    t, k = a.shape
    n = w.shape[1]
    tm = _tile(t, 1024)
    tn = tn_pref if n % tn_pref == 0 else _tile(n, 1024)
    ni, nj = t // tm, n // tn
    in_specs = [pl.BlockSpec((tm, k), lambda i, j: (i, 0)),
                pl.BlockSpec((k, tn), lambda i, j: (0, j))]
    args = [a, w]
    if bias is not None:
        in_specs.append(pl.BlockSpec((1, tn), lambda i, j: (0, j)))
        args.append(bias.reshape(1, n))
    cast_in_specs, cast_out_specs, cast_args, cast_shapes = _cast_plan(casts, ni, nj)
    in_specs += cast_in_specs
    args += cast_args
    out_specs = [pl.BlockSpec((tm, tn), lambda i, j: (i, j))] + cast_out_specs
    out_shape = [jax.ShapeDtypeStruct((t, n), out_dtype)] + cast_shapes
    outs = pl.pallas_call(
        functools.partial(_mm_kernel, epilogue=epilogue, has_bias=bias is not None,
                          n_casts=len(casts)),
        out_shape=out_shape,
        grid=(ni, nj),
        in_specs=in_specs,
        out_specs=out_specs,
        compiler_params=_cparams(("arbitrary", "arbitrary")),
        name=name,
    )(*args)
    return outs[0], [o.reshape(-1, o.shape[-1]) for o in outs[1:]]


def _mm_ksplit_kernel(a_ref, w_ref, o_ref):
    @pl.when(pl.program_id(2) == 0)
    def _():
        o_ref[...] = jnp.zeros_like(o_ref)

    o_ref[...] += jnp.dot(a_ref[...], w_ref[...], preferred_element_type=F32)


def _mm_ksplit(a, w, *, name):
    t, k = a.shape
    n = w.shape[1]
    tm = _tile(t, 1024)
    tn = _tile(n, 1024)
    tk = _tile(k, 4096)
    return pl.pallas_call(
        _mm_ksplit_kernel,
        out_shape=jax.ShapeDtypeStruct((t, n), F32),
        grid=(t // tm, n // tn, k // tk),
        in_specs=[pl.BlockSpec((tm, tk), lambda i, j, kk: (i, kk)),
                  pl.BlockSpec((tk, tn), lambda i, j, kk: (kk, j))],
        out_specs=pl.BlockSpec((tm, tn), lambda i, j, kk: (i, j)),
        compiler_params=_cparams(("arbitrary", "arbitrary", "arbitrary")),
        name=name,
    )(a, w)


def _rope_tables(seq):
    inv_freq = 1.0 / (ROPE_THETA ** (jnp.arange(0, HEAD_DIM, 2, dtype=F32) / HEAD_DIM))
    ang = jnp.arange(seq, dtype=F32)[:, None] * inv_freq[None, :]
    cos, sin = jnp.cos(ang), jnp.sin(ang)
    return jnp.concatenate([cos, cos], axis=-1), jnp.concatenate([-sin, sin], axis=-1)


def kernel(x, c, w_ada, b_ada, g_pre_mix, g_post_mix, g_pre_mlp, g_post_mlp, w_in, attn_sinks,
           w_branch_gate, b_branch_gate, w_proj_swa, w_proj_sb, w_out, w_ff_in, w_ff_out):
    batch, seq, d = x.shape
    depth = w_in.shape[0]
    d_in = w_in.shape[2]
    swa_q_heads = attn_sinks.shape[1]
    swa_q_w = w_proj_swa.shape[1]
    sb_w = w_proj_sb.shape[1]
    swa_kv_w = (d_in - swa_q_w - 3 * sb_w) // 2
    sb_heads = sb_w // HEAD_DIM
    assert swa_q_w == swa_q_heads * HEAD_DIM and swa_kv_w * SWA_GROUP == swa_q_w
    assert seq % SWA_WINDOW == 0
    t = batch * seq

    cos_full, sin_signed = _rope_tables(seq)
    pad_rows = (-batch) % 8
    c_pad = jnp.pad(c, ((0, pad_rows), (0, 0)))

    swa_w = swa_q_w + 2 * swa_kv_w
    n_rope = swa_q_w + swa_kv_w

    for l in range(depth):
        mod = _ada_mod(c_pad, w_ada[l], b_ada[l])[:batch]
        mod3 = mod.reshape(batch * N_MOD, 1, d)

        h = _prenorm(x, g_pre_mix[l], mod3, 0, 1).reshape(t, d)
        proj_swa, (w_sb_in_bf,) = _rope_proj(
            h, w_in[l][:, :swa_w].astype(BF16), cos_full, sin_signed, n_rope=n_rope, seq=seq,
            casts=((w_in[l], swa_w, d_in - swa_w),))
        proj_sb, (w_gate_bf,) = _mm(h, w_sb_in_bf, BF16, casts=(w_branch_gate[l],),
                                    tn_pref=768, name="sb_proj")
        gates, (w_swa_bf, w_sb_bf, w_out_bf, w_ff_in_bf) = _mm(
            h, w_gate_bf, BF16, bias=b_branch_gate[l], epilogue="sigmoid",
            casts=(w_proj_swa[l], w_proj_sb[l], w_out[l], w_ff_in[l]), name="gate_proj")
        a_swa = _swa(proj_swa, attn_sinks[l], batch=batch, seq=seq, q_heads=swa_q_heads,
                     k_col=swa_q_w, v_col=n_rope)
        a_sb = _sb(proj_sb, batch=batch, seq=seq, heads=sb_heads,
                   q_col=0, k_col=sb_w, v_col=2 * sb_w)
        merged = _merge(a_swa, a_sb, w_swa_bf, w_sb_bf, gates)
        mixed, _ = _mm(merged, w_out_bf, F32, name="out_proj")
        act, x1, (w_ff_out_bf,) = _mlp_in(
            mixed, x.reshape(t, d), g_post_mix[l], g_pre_mlp[l], mod3, w_ff_in_bf, seq=seq,
            k_gate=2, k_shift=3, k_scale=4, casts=(w_ff_out[l],))
        f = _mm_ksplit(act, w_ff_out_bf, name="ff_out").reshape(batch, seq, d)
        x = _post(f, x1.reshape(batch, seq, d), g_post_mlp[l], mod3, 5)
    return x
```

```python
import functools
import math

import jax
import jax.numpy as jnp
from jax import lax
from jax.experimental import pallas as pl
from jax.experimental.pallas import tpu as pltpu

F32 = jnp.float32
BF16 = jnp.bfloat16

HEAD_DIM = 128
LANES = 128
BF16_SUBLANES = 16
SWA_WINDOW = 128
SWA_GROUP = 4
ROPE_THETA = 10000.0
NORM_EPS = 1e-6
LOG2E = 1.4426950408889634
SB_UNDERFLOW_LOG2 = -152.0
N_MOD = 6
MIB = 1024 * 1024
VMEM_LIMIT_BYTES = 56 * MIB


def _tile(dim, pref):
    t = min(dim, pref)
    while dim % t:
        t //= 2
    return t


def _cparams(semantics, vmem=VMEM_LIMIT_BYTES, flags=None):
    return pltpu.CompilerParams(dimension_semantics=semantics, vmem_limit_bytes=vmem, flags=flags)


def _ada_kernel(c_ref, w_ref, b_ref, o_ref):
    c = c_ref[...]
    act = (c * jax.nn.sigmoid(c)).astype(BF16)
    o_ref[...] = jnp.dot(act, w_ref[...].astype(BF16), preferred_element_type=F32) + b_ref[...]


def _ada_mod(c_pad, w_ada, b_ada):
    rows, d = c_pad.shape
    n = w_ada.shape[1]
    tn = _tile(n, 1024)
    return pl.pallas_call(
        _ada_kernel,
        out_shape=jax.ShapeDtypeStruct((rows, n), F32),
        grid=(n // tn,),
        in_specs=[
            pl.BlockSpec((rows, d), lambda j: (0, 0)),
            pl.BlockSpec((d, tn), lambda j: (0, j)),
            pl.BlockSpec((1, tn), lambda j: (0, j)),
        ],
        out_specs=pl.BlockSpec((rows, tn), lambda j: (0, j)),
        compiler_params=_cparams(("arbitrary",)),
        name="ada_mod",
    )(c_pad, w_ada, b_ada.reshape(1, n))


def _rms(x, g):
    y = x * lax.rsqrt(jnp.mean(x * x, axis=-1, keepdims=True) + NORM_EPS)
    return y * g


def _prenorm_kernel(x_ref, g_ref, scale_ref, shift_ref, h_ref):
    h = _rms(x_ref[0], g_ref[...]) * (1.0 + scale_ref[0]) + shift_ref[0]
    h_ref[0] = h.astype(h_ref.dtype)


def _mod_spec(d, k):
    return pl.BlockSpec((1, 1, d), lambda b, i: (b * N_MOD + k, 0, 0))


def _prenorm(x, g, mod3, k_shift, k_scale):
    b, s, d = x.shape
    ts = _tile(s, 512)
    row = pl.BlockSpec((1, ts, d), lambda bb, i: (bb, i, 0))
    return pl.pallas_call(
        _prenorm_kernel,
        out_shape=jax.ShapeDtypeStruct((b, s, d), BF16),
        grid=(b, s // ts),
        in_specs=[row, pl.BlockSpec((1, d), lambda bb, i: (0, 0)),
                  _mod_spec(d, k_scale), _mod_spec(d, k_shift)],
        out_specs=row,
        compiler_params=_cparams(("arbitrary", "arbitrary")),
        name="prenorm",
    )(x, g.reshape(1, d), mod3, mod3)


def _post_pre_math(y, x, gpost, gate, gpre, scale, shift):
    x1 = x + gate * _rms(y, gpost)
    h = _rms(x1, gpre) * (1.0 + scale) + shift
    return x1, h


def _first_h2_kernel(y_ref, x_ref, gpost_ref, gate_ref, gpre_ref, scale_ref, shift_ref, h_ref):
    _, h = _post_pre_math(y_ref[...], x_ref[...], gpost_ref[...], gate_ref[0], gpre_ref[...],
                          scale_ref[0], shift_ref[0])
    h_ref[...] = h.astype(h_ref.dtype)


def _first_h2(y, x, gpost, gpre, mod3, rows, k_gate, k_shift, k_scale):
    d = x.shape[1]
    ts = _tile(rows, 256)
    row = pl.BlockSpec((ts, d), lambda i: (i, 0))
    vec = pl.BlockSpec((1, d), lambda i: (0, 0))

    def mod(k):
        return pl.BlockSpec((1, 1, d), lambda i: (k, 0, 0))

    return pl.pallas_call(
        _first_h2_kernel,
        out_shape=jax.ShapeDtypeStruct((rows, d), BF16),
        grid=(rows // ts,),
        in_specs=[row, row, vec, mod(k_gate), vec, mod(k_scale), mod(k_shift)],
        out_specs=row,
        compiler_params=_cparams(("arbitrary",)),
        name="first_h2",
    )(y, x, gpost.reshape(1, d), mod3, gpre.reshape(1, d), mod3, mod3)


def _mlp_in_kernel(*refs, chunk, n_casts):
    (y_ref, x_ref, gpost_ref, gate_ref, gpre_ref, scale_ref, shift_ref, h_first_hbm,
     w_ref) = refs[:9]
    cast_in = refs[9:9 + n_casts]
    act_ref, x1_ref = refs[9 + n_casts:11 + n_casts]
    cast_out = refs[11 + n_casts:11 + 2 * n_casts]
    h_even, h_odd, sem = refs[11 + 2 * n_casts:]
    i = pl.program_id(0)
    j = pl.program_id(1)

    @pl.when(jnp.logical_and(i == 0, j == 0))
    def _():
        first = pltpu.make_async_copy(h_first_hbm, h_even, sem)
        first.start()
        first.wait()

    for src, dst in zip(cast_in, cast_out):
        dst[...] = src[...].astype(dst.dtype)

    def step(h_cur, h_next):
        x1, h = _post_pre_math(y_ref[...], x_ref[...], gpost_ref[...], gate_ref[0], gpre_ref[...],
                               scale_ref[0], shift_ref[0])
        x1_ref[...] = x1
        h_next[pl.ds(pl.multiple_of(j * chunk, chunk), chunk), :] = h.astype(h_next.dtype)
        acc = jnp.dot(h_cur[...], w_ref[...], preferred_element_type=F32)
        act_ref[...] = jnp.square(jnp.maximum(acc, 0.0)).astype(act_ref.dtype)

    @pl.when(i % 2 == 0)
    def _():
        step(h_even, h_odd)

    @pl.when(i % 2 == 1)
    def _():
        step(h_odd, h_even)


def _mlp_in(y, x, gpost, gpre, mod3, w, *, seq, k_gate, k_shift, k_scale, casts=()):
    t, d = x.shape
    n = w.shape[1]
    tm = _tile(seq, 1024)
    tn = _tile(n, 1024)
    ni, nj = t // tm, n // tn
    chunk = tm // nj
    assert tm % nj == 0 and chunk % BF16_SUBLANES == 0
    blocks_per_batch = seq // tm

    def nxt(i):
        return (i + 1) % ni

    rows = pl.BlockSpec((chunk, d), lambda i, j: (nxt(i) * nj + j, 0))
    vec = pl.BlockSpec((1, d), lambda i, j: (0, 0))

    def mod(k):
        return pl.BlockSpec((1, 1, d),
                            lambda i, j: ((nxt(i) // blocks_per_batch) * N_MOD + k, 0, 0))

    h_first = _first_h2(y, x, gpost, gpre, mod3, tm, k_gate, k_shift, k_scale)
    cast_in_specs, cast_out_specs, cast_args, cast_shapes = _cast_plan(casts, ni, nj)
    outs = pl.pallas_call(
        functools.partial(_mlp_in_kernel, chunk=chunk, n_casts=len(casts)),
        out_shape=[jax.ShapeDtypeStruct((t, n), BF16), jax.ShapeDtypeStruct((t, d), F32)]
        + cast_shapes,
        grid=(ni, nj),
        in_specs=[rows, rows, vec, mod(k_gate), vec, mod(k_scale), mod(k_shift),
                  pl.BlockSpec(memory_space=pl.ANY),
                  pl.BlockSpec((d, tn), lambda i, j: (0, j))] + cast_in_specs,
        out_specs=[pl.BlockSpec((tm, tn), lambda i, j: (i, j)), rows] + cast_out_specs,
        scratch_shapes=[pltpu.VMEM((tm, d), BF16), pltpu.VMEM((tm, d), BF16),
                        pltpu.SemaphoreType.DMA(())],
        compiler_params=_cparams(("arbitrary", "arbitrary")),
        name="ff_in",
    )(y, x, gpost.reshape(1, d), mod3, gpre.reshape(1, d), mod3, mod3, h_first, w, *cast_args)
    return outs[0], outs[1], [o.reshape(-1, o.shape[-1]) for o in outs[2:]]


def _post_kernel(y_ref, x_ref, gpost_ref, gate_ref, o_ref):
    o_ref[0] = x_ref[0] + gate_ref[0] * _rms(y_ref[0], gpost_ref[...])


def _post(y, x, gpost, mod3, k_gate):
    b, s, d = x.shape
    ts = _tile(s, 512)
    row = pl.BlockSpec((1, ts, d), lambda bb, i: (bb, i, 0))
    return pl.pallas_call(
        _post_kernel,
        out_shape=jax.ShapeDtypeStruct((b, s, d), F32),
        grid=(b, s // ts),
        in_specs=[row, row, pl.BlockSpec((1, d), lambda bb, i: (0, 0)), _mod_spec(d, k_gate)],
        out_specs=row,
        compiler_params=_cparams(("arbitrary", "arbitrary")),
        name="post_norm",
    )(y, x, gpost.reshape(1, d), mod3)


def _rope_proj_kernel(*refs, full_rope_tiles, partial_chunks, n_casts):
    a_ref, w_ref, cos_ref, sin_ref = refs[:4]
    cast_in = refs[4:4 + n_casts]
    o_ref = refs[4 + n_casts]
    cast_out = refs[5 + n_casts:]
    for src, dst in zip(cast_in, cast_out):
        dst[...] = src[...].astype(dst.dtype)
    j = pl.program_id(1)
    acc = jnp.dot(a_ref[...], w_ref[...], preferred_element_type=F32)
    cos, sin = cos_ref[...], sin_ref[...]
    head = j <= full_rope_tiles
    tail = j < full_rope_tiles
    tables = {True: (jnp.where(head, cos, 1.0), jnp.where(head, sin, 0.0)),
              False: (jnp.where(tail, cos, 1.0), jnp.where(tail, sin, 0.0))}
    for c in range(acc.shape[1] // LANES):
        cols = slice(c * LANES, (c + 1) * LANES)
        cos_c, sin_c = tables[c < partial_chunks]
        blk = acc[:, cols]
        o_ref[:, cols] = (blk * cos_c + pltpu.roll(blk, HEAD_DIM // 2, 1) * sin_c).astype(o_ref.dtype)


def _rope_proj(h, w, cos_full, sin_signed, *, n_rope, seq, casts=()):
    t, d = h.shape
    n = w.shape[1]
    tm = _tile(seq, 1024)
    tn = _tile(n, 1024)
    assert n_rope % LANES == 0 and tn % LANES == 0
    n_seq_tiles = seq // tm
    ni, nj = t // tm, n // tn
    cast_in_specs, cast_out_specs, cast_args, cast_shapes = _cast_plan(casts, ni, nj)
    kern = functools.partial(_rope_proj_kernel, full_rope_tiles=n_rope // tn,
                             partial_chunks=(n_rope % tn) // LANES, n_casts=len(casts))
    outs = pl.pallas_call(
        kern,
        out_shape=[jax.ShapeDtypeStruct((t, n), BF16)] + cast_shapes,
        grid=(ni, nj),
        in_specs=[
            pl.BlockSpec((tm, d), lambda i, j: (i, 0)),
            pl.BlockSpec((d, tn), lambda i, j: (0, j)),
            pl.BlockSpec((tm, HEAD_DIM), lambda i, j: (i % n_seq_tiles, 0)),
            pl.BlockSpec((tm, HEAD_DIM), lambda i, j: (i % n_seq_tiles, 0)),
        ] + cast_in_specs,
        out_specs=[pl.BlockSpec((tm, tn), lambda i, j: (i, j))] + cast_out_specs,
        compiler_params=_cparams(("arbitrary", "arbitrary")),
        name="swa_proj",
    )(h, w, cos_full, sin_signed, *cast_args)
    return outs[0], [o.reshape(-1, o.shape[-1]) for o in outs[1:]]


def _swa_kernel(sink_ref, q_ref, kp_ref, k_ref, vp_ref, v_ref, o_ref, p_ref, *, tq):
    i = pl.program_id(1)
    kvh = pl.program_id(2)
    w = SWA_WINDOW
    rows = SWA_GROUP * w
    qi = lax.broadcasted_iota(jnp.int32, (rows, 2 * w), 0) % w
    kj = lax.broadcasted_iota(jnp.int32, (rows, 2 * w), 1)
    band = jnp.logical_and(kj > qi, kj <= qi + w)
    first = jnp.logical_and(band, jnp.logical_or(kj >= w, i > 0))
    sink_fill = jnp.concatenate(
        [jnp.full((w, 2 * w), sink_ref[kvh * SWA_GROUP + g], F32) for g in range(SWA_GROUP)], axis=0)
    masked_fill = jnp.where(kj == 0, sink_fill, -jnp.inf)
    key_row = lax.broadcasted_iota(jnp.int32, (2 * w, HEAD_DIM), 0)
    ones = jnp.ones((2 * w, HEAD_DIM), BF16)
    scale = HEAD_DIM ** -0.5
    n_sub = tq // w
    def probs_stage(s):
        q = jnp.concatenate(
            [q_ref[s * w:(s + 1) * w, g * HEAD_DIM:(g + 1) * HEAD_DIM] for g in range(SWA_GROUP)],
            axis=0)
        if s == 0:
            k = jnp.concatenate([kp_ref[...], k_ref[0:w, :]], axis=0)
            valid = first
        else:
            k = k_ref[(s - 1) * w:(s + 1) * w, :]
            valid = band
        sc = lax.dot_general(q, k, (((1,), (1,)), ((), ())), preferred_element_type=F32) * scale
        sc = jnp.where(valid, sc, masked_fill)
        p_ref[s] = jnp.exp(sc - jnp.max(sc, axis=-1, keepdims=True)).astype(BF16)

    def values_stage(s):
        if s == 0:
            v = jnp.concatenate([vp_ref[...], v_ref[0:w, :]], axis=0)
        else:
            v = v_ref[(s - 1) * w:(s + 1) * w, :]
        v = jnp.where(key_row == 0, jnp.zeros_like(v), v)
        ov = jnp.dot(p_ref[s], jnp.concatenate([v, ones], axis=1), preferred_element_type=F32)
        o = ov[:, :HEAD_DIM] / ov[:, HEAD_DIM:]
        for g in range(SWA_GROUP):
            o_ref[s * w:(s + 1) * w, g * HEAD_DIM:(g + 1) * HEAD_DIM] = (
                o[g * w:(g + 1) * w, :].astype(o_ref.dtype))

    for s in range(n_sub):
        probs_stage(s)
        if s >= 1:
            values_stage(s - 1)
    values_stage(n_sub - 1)


def _swa(proj, sinks, *, batch, seq, q_heads, k_col, v_col):
    kv_heads = q_heads // SWA_GROUP
    w = SWA_WINDOW
    tq = _tile(seq, 512)
    nq = seq // tq
    sub = tq // w
    gw = SWA_GROUP * HEAD_DIM
    kb, vb = k_col // HEAD_DIM, v_col // HEAD_DIM

    def prev_rows(b, i):
        return b * (seq // w) + jnp.maximum(i * sub - 1, 0)

    grid_spec = pltpu.PrefetchScalarGridSpec(
        num_scalar_prefetch=1,
        grid=(batch, nq, kv_heads),
        in_specs=[
            pl.BlockSpec((tq, gw), lambda b, i, h, s: (b * nq + i, h)),
            pl.BlockSpec((w, HEAD_DIM), lambda b, i, h, s: (prev_rows(b, i), kb + h)),
            pl.BlockSpec((tq, HEAD_DIM), lambda b, i, h, s: (b * nq + i, kb + h)),
            pl.BlockSpec((w, HEAD_DIM), lambda b, i, h, s: (prev_rows(b, i), vb + h)),
            pl.BlockSpec((tq, HEAD_DIM), lambda b, i, h, s: (b * nq + i, vb + h)),
        ],
        out_specs=pl.BlockSpec((tq, gw), lambda b, i, h, s: (b * nq + i, h)),
        scratch_shapes=[pltpu.VMEM((sub, SWA_GROUP * w, 2 * w), BF16)],
    )
    return pl.pallas_call(
        functools.partial(_swa_kernel, tq=tq),
        out_shape=jax.ShapeDtypeStruct((batch * seq, q_heads * HEAD_DIM), BF16),
        grid_spec=grid_spec,
        compiler_params=_cparams(("arbitrary", "arbitrary", "arbitrary")),
        name="swa_attention",
    )(sinks, proj, proj, proj, proj, proj)


def _sb_kernel(q_ref, kc_ref, kp_ref, vc_ref, vp_ref, kv_hbm, o_ref, hilo0_ref, hilo1_ref, lb0_ref,
               lb1_ref, acc_ref, r_ref, kbuf_ref, vbuf_ref, sem, *, tq, hp, seq, k_col, v_col):
    hilo_refs = (hilo0_ref, hilo1_ref)
    lb_refs = (lb0_ref, lb1_ref)
    i = pl.program_id(2)
    row = lax.broadcasted_iota(jnp.int32, (tq, tq), 0)
    col = lax.broadcasted_iota(jnp.int32, (tq, tq), 1)
    strict_lower = col < row
    neg_later = jnp.where(row > col, -1.0, 0.0).astype(BF16)
    neg_later = jnp.concatenate([neg_later, neg_later], axis=0)
    z_scale = HEAD_DIM ** -0.5 * LOG2E
    sign_bit = jnp.uint32(0x80000000)

    def logits_stage(slot, k_ref, diag, heads=range(hp)):
        for h in heads:
            cols = slice(h * HEAD_DIM, (h + 1) * HEAD_DIM)
            q = q_ref[:, cols]
            k = k_ref[:, cols]
            z = lax.dot_general(q, k, (((1,), (1,)), ((), ())),
                                preferred_element_type=F32) * z_scale
            neg_abs = lax.bitcast_convert_type(
                lax.bitcast_convert_type(z, jnp.uint32) | sign_bit, F32)
            softplus = jnp.maximum(z, 0.0) + jnp.log(1.0 + jnp.exp2(neg_abs)) * LOG2E
            log_beta = z - softplus
            if diag:
                lb_refs[slot][h] = jnp.where(strict_lower, log_beta, -jnp.inf)
                softplus = jnp.where(strict_lower, softplus, 0.0)
                r_ref[h] = -jnp.sum(softplus, axis=-1, keepdims=True)
            else:
                r_sum = r_ref[h]
                lb_refs[slot][h] = log_beta + r_sum
                r_ref[h] = r_sum - jnp.sum(softplus, axis=-1, keepdims=True)
            hi = softplus.astype(BF16)
            lo = (softplus - hi.astype(F32)).astype(BF16)
            hilo_refs[slot][h] = jnp.concatenate([hi, lo], axis=1)

    def weights_stage(slot, v_ref, first, heads=range(hp)):
        for h in heads:
            cols = slice(h * HEAD_DIM, (h + 1) * HEAD_DIM)
            v = v_ref[:, cols]
            suffix = jnp.dot(hilo_refs[slot][h], neg_later, preferred_element_type=F32)
            att = jnp.exp2(lb_refs[slot][h] + suffix)
            pv = jnp.dot(att.astype(BF16), v, preferred_element_type=F32)
            if first:
                acc_ref[h] = pv
            else:
                acc_ref[h] += pv

    def any_row_alive():
        top = functools.reduce(
            jnp.maximum, [jnp.max(r_ref[h], axis=0, keepdims=True) for h in range(hp)])
        return top[0, 0] > SB_UNDERFLOW_LOG2

    def staggered(logits_stages, weights_stages):
        lag = 2
        for k, stage in enumerate(logits_stages):
            stage()
            if k >= lag:
                weights_stages[k - lag]()
        for stage in weights_stages[max(len(logits_stages) - lag, 0):]:
            stage()

    def per_head(stage, *args):
        return [functools.partial(stage, *args, heads=(h,)) for h in range(hp)]

    @pl.when(i == 0)
    def _():
        staggered(per_head(logits_stage, 0, kc_ref, True), per_head(weights_stage, 0, vc_ref, True))

    @pl.when(i > 0)
    def _():
        staggered(
            per_head(logits_stage, 0, kc_ref, True) + per_head(logits_stage, 1, kp_ref, False),
            per_head(weights_stage, 0, vc_ref, True) + per_head(weights_stage, 1, vp_ref, False))

        def cond(state):
            n, alive = state
            return jnp.logical_and(n <= i, alive)

        def body(state):
            n, _ = state
            row0 = pl.multiple_of(pl.program_id(0) * seq + (i - n) * tq, tq)
            col0 = pl.program_id(1) * (hp * HEAD_DIM)
            copies = [
                pltpu.make_async_copy(
                    kv_hbm.at[pl.ds(row0, tq), pl.ds(pl.multiple_of(c + col0, LANES), hp * HEAD_DIM)],
                    buf, sem.at[s])
                for s, (c, buf) in enumerate(((k_col, kbuf_ref), (v_col, vbuf_ref)))]
            for cp in copies:
                cp.start()
            for cp in copies:
                cp.wait()
            staggered(per_head(logits_stage, 1, kbuf_ref, False),
                      per_head(weights_stage, 1, vbuf_ref, False))
            return n + 1, any_row_alive()

        lax.while_loop(cond, body, (jnp.int32(2), any_row_alive()))

    for h in range(hp):
        o_ref[:, h * HEAD_DIM:(h + 1) * HEAD_DIM] = acc_ref[h].astype(o_ref.dtype)


def _sb(proj, *, batch, seq, heads, q_col, k_col, v_col):
    tq = _tile(seq, 256)
    nq = seq // tq
    hp = 8 if heads % 8 == 0 else 1
    hw = hp * HEAD_DIM
    assert q_col % hw == 0 and k_col % hw == 0 and v_col % hw == 0
    qb, kb, vb = q_col // hw, k_col // hw, v_col // hw

    def cur(col):
        return pl.BlockSpec((tq, hw), lambda b, h, i: (b * nq + i, col + h))

    def prev(col):
        return pl.BlockSpec((tq, hw), lambda b, h, i: (b * nq + jnp.maximum(i - 1, 0), col + h))

    return pl.pallas_call(
        functools.partial(_sb_kernel, tq=tq, hp=hp, seq=seq, k_col=k_col, v_col=v_col),
        out_shape=jax.ShapeDtypeStruct((batch * seq, heads * HEAD_DIM), BF16),
        grid=(batch, heads // hp, nq),
        in_specs=[cur(qb), cur(kb), prev(kb), cur(vb), prev(vb),
                  pl.BlockSpec(memory_space=pl.ANY)],
        out_specs=pl.BlockSpec((tq, hw), lambda b, h, i: (b * nq + i, h)),
        scratch_shapes=[pltpu.VMEM((hp, tq, 2 * tq), BF16), pltpu.VMEM((hp, tq, 2 * tq), BF16),
                        pltpu.VMEM((hp, tq, tq), F32), pltpu.VMEM((hp, tq, tq), F32),
                        pltpu.VMEM((hp, tq, HEAD_DIM), F32), pltpu.VMEM((hp, tq, 1), F32),
                        pltpu.VMEM((tq, hw), BF16), pltpu.VMEM((tq, hw), BF16),
                        pltpu.SemaphoreType.DMA((2,))],
        compiler_params=_cparams(("arbitrary", "arbitrary", "arbitrary")),
        name="sb_attention",
    )(proj, proj, proj, proj, proj, proj)


def _merge_kernel(a1_ref, a2_ref, w1_ref, w2_ref, ga_ref, gb_ref, o_ref):
    y1 = jnp.dot(a1_ref[...], w1_ref[...], preferred_element_type=F32)
    y2 = jnp.dot(a2_ref[...], w2_ref[...], preferred_element_type=F32)
    o = ga_ref[...].astype(F32) * y1 + gb_ref[...].astype(F32) * y2
    o_ref[...] = o.astype(o_ref.dtype)


def _merge(a_swa, a_sb, w_swa, w_sb, gates):
    t, k1 = a_swa.shape
    k2 = a_sb.shape[1]
    d = w_swa.shape[1]
    tm = _tile(t, 1024)
    tn = _tile(d, 1024)
    ga, gb = 0, d // tn
    return pl.pallas_call(
        _merge_kernel,
        out_shape=jax.ShapeDtypeStruct((t, d), BF16),
        grid=(t // tm, d // tn),
        in_specs=[
            pl.BlockSpec((tm, k1), lambda i, j: (i, 0)),
            pl.BlockSpec((tm, k2), lambda i, j: (i, 0)),
            pl.BlockSpec((k1, tn), lambda i, j: (0, j)),
            pl.BlockSpec((k2, tn), lambda i, j: (0, j)),
            pl.BlockSpec((tm, tn), lambda i, j: (i, ga + j)),
            pl.BlockSpec((tm, tn), lambda i, j: (i, gb + j)),
        ],
        out_specs=pl.BlockSpec((tm, tn), lambda i, j: (i, j)),
        compiler_params=_cparams(("arbitrary", "arbitrary")),
        name="branch_merge",
    )(a_swa, a_sb, w_swa, w_sb, gates, gates)


def _mm_kernel(*refs, epilogue, has_bias, n_casts):
    a_ref, w_ref = refs[:2]
    b_ref = refs[2] if has_bias else None
    first_cast = 2 + has_bias
    cast_in = refs[first_cast:first_cast + n_casts]
    o_ref = refs[first_cast + n_casts]
    cast_out = refs[first_cast + n_casts + 1:]
    for src, dst in zip(cast_in, cast_out):
        dst[...] = src[...].astype(dst.dtype)
    acc = jnp.dot(a_ref[...], w_ref[...], preferred_element_type=F32)
    if has_bias:
        acc = acc + b_ref[...]
    if epilogue == "relu_sq":
        acc = jnp.square(jnp.maximum(acc, 0.0))
    elif epilogue == "sigmoid":
        acc = 0.5 * jnp.tanh(0.5 * acc) + 0.5
    o_ref[...] = acc.astype(o_ref.dtype)


def _cast_plan(casts, ni, nj):
    steps = ni * nj
    in_specs, out_specs, args, shapes = [], [], [], []
    for c in casts:
        c, col0, cols = c if isinstance(c, tuple) else (c, 0, c.shape[1])
        rows = c.shape[0]
        if col0 == 0 and cols == c.shape[1] and rows % (steps * BF16_SUBLANES) == 0:
            slab = rows // steps
            spec = pl.BlockSpec((1, slab, cols), lambda i, j: (i * nj + j, 0, 0))
            in_specs.append(spec)
            out_specs.append(spec)
            args.append(c.reshape(steps, slab, cols))
            shapes.append(jax.ShapeDtypeStruct((steps, slab, cols), BF16))
        else:
            slab = cols // steps
            assert cols % (steps * LANES) == 0 and col0 % slab == 0, (c.shape, col0, cols, steps)
            first = col0 // slab
            in_specs.append(pl.BlockSpec((rows, slab), lambda i, j: (0, first + i * nj + j)))
            out_specs.append(pl.BlockSpec((rows, slab), lambda i, j: (0, i * nj + j)))
            args.append(c)
            shapes.append(jax.ShapeDtypeStruct((rows, cols), BF16))
    return in_specs, out_specs, args, shapes


def _mm(a, w, out_dtype, *, bias=None, epilogue=None, casts=(), tn_pref=1024, name):
    t, k = a.shape
    n = w.shape[1]
    tm = _tile(t, 1024)
    tn = tn_pref if n % tn_pref == 0 else _tile(n, 1024)
    ni, nj = t // tm, n // tn
    in_specs = [pl.BlockSpec((tm, k), lambda i, j: (i, 0)),
                pl.BlockSpec((k, tn), lambda i, j: (0, j))]
    args = [a, w]
    if bias is not None:
        in_specs.append(pl.BlockSpec((1, tn), lambda i, j: (0, j)))
        args.append(bias.reshape(1, n))
    cast_in_specs, cast_out_specs, cast_args, cast_shapes = _cast_plan(casts, ni, nj)
    in_specs += cast_in_specs
    args += cast_args
    out_specs = [pl.BlockSpec((tm, tn), lambda i, j: (i, j))] + cast_out_specs
    out_shape = [jax.ShapeDtypeStruct((t, n), out_dtype)] + cast_shapes
    outs = pl.pallas_call(
        functools.partial(_mm_kernel, epilogue=epilogue, has_bias=bias is not None,
                          n_casts=len(casts)),
        out_shape=out_shape,
        grid=(ni, nj),
        in_specs=in_specs,
        out_specs=out_specs,
        compiler_params=_cparams(("arbitrary", "arbitrary")),
        name=name,
    )(*args)
    return outs[0], [o.reshape(-1, o.shape[-1]) for o in outs[1:]]


def _mm_ksplit_kernel(a_ref, w_ref, o_ref):
    @pl.when(pl.program_id(2) == 0)
    def _():
        o_ref[...] = jnp.zeros_like(o_ref)

    o_ref[...] += jnp.dot(a_ref[...], w_ref[...], preferred_element_type=F32)


def _mm_ksplit(a, w, *, name):
    t, k = a.shape
    n = w.shape[1]
    tm = _tile(t, 1024)
    tn = _tile(n, 1024)
    tk = _tile(k, 4096)
    return pl.pallas_call(
        _mm_ksplit_kernel,
        out_shape=jax.ShapeDtypeStruct((t, n), F32),
        grid=(t // tm, n // tn, k // tk),
        in_specs=[pl.BlockSpec((tm, tk), lambda i, j, kk: (i, kk)),
                  pl.BlockSpec((tk, tn), lambda i, j, kk: (kk, j))],
        out_specs=pl.BlockSpec((tm, tn), lambda i, j, kk: (i, j)),
        compiler_params=_cparams(("arbitrary", "arbitrary", "arbitrary")),
        name=name,
    )(a, w)


def _rope_tables(seq):
    inv_freq = 1.0 / (ROPE_THETA ** (jnp.arange(0, HEAD_DIM, 2, dtype=F32) / HEAD_DIM))
    ang = jnp.arange(seq, dtype=F32)[:, None] * inv_freq[None, :]
    cos, sin = jnp.cos(ang), jnp.sin(ang)
    return jnp.concatenate([cos, cos], axis=-1), jnp.concatenate([-sin, sin], axis=-1)


def kernel(x, c, w_ada, b_ada, g_pre_mix, g_post_mix, g_pre_mlp, g_post_mlp, w_in, attn_sinks,
           w_branch_gate, b_branch_gate, w_proj_swa, w_proj_sb, w_out, w_ff_in, w_ff_out):
    batch, seq, d = x.shape
    depth = w_in.shape[0]
    d_in = w_in.shape[2]
    swa_q_heads = attn_sinks.shape[1]
    swa_q_w = w_proj_swa.shape[1]
    sb_w = w_proj_sb.shape[1]
    swa_kv_w = (d_in - swa_q_w - 3 * sb_w) // 2
    sb_heads = sb_w // HEAD_DIM
    assert swa_q_w == swa_q_heads * HEAD_DIM and swa_kv_w * SWA_GROUP == swa_q_w
    assert seq % SWA_WINDOW == 0
    t = batch * seq

    cos_full, sin_signed = _rope_tables(seq)
    pad_rows = (-batch) % 8
    c_pad = jnp.pad(c, ((0, pad_rows), (0, 0)))

    swa_w = swa_q_w + 2 * swa_kv_w
    n_rope = swa_q_w + swa_kv_w

    for l in range(depth):
        mod = _ada_mod(c_pad, w_ada[l], b_ada[l])[:batch]
        mod3 = mod.reshape(batch * N_MOD, 1, d)

        h = _prenorm(x, g_pre_mix[l], mod3, 0, 1).reshape(t, d)
        proj_swa, (w_sb_in_bf,) = _rope_proj(
            h, w_in[l][:, :swa_w].astype(BF16), cos_full, sin_signed, n_rope=n_rope, seq=seq,
            casts=((w_in[l], swa_w, d_in - swa_w),))
        proj_sb, (w_gate_bf,) = _mm(h, w_sb_in_bf, BF16, casts=(w_branch_gate[l],),
                                    tn_pref=768, name="sb_proj")
        gates, (w_swa_bf, w_sb_bf, w_out_bf, w_ff_in_bf) = _mm(
            h, w_gate_bf, BF16, bias=b_branch_gate[l], epilogue="sigmoid",
            casts=(w_proj_swa[l], w_proj_sb[l], w_out[l], w_ff_in[l]), name="gate_proj")
        a_swa = _swa(proj_swa, attn_sinks[l], batch=batch, seq=seq, q_heads=swa_q_heads,
                     k_col=swa_q_w, v_col=n_rope)
        a_sb = _sb(proj_sb, batch=batch, seq=seq, heads=sb_heads,
                   q_col=0, k_col=sb_w, v_col=2 * sb_w)
        merged = _merge(a_swa, a_sb, w_swa_bf, w_sb_bf, gates)
        mixed, _ = _mm(merged, w_out_bf, F32, name="out_proj")
        act, x1, (w_ff_out_bf,) = _mlp_in(
            mixed, x.reshape(t, d), g_post_mix[l], g_pre_mlp[l], mod3, w_ff_in_bf, seq=seq,
            k_gate=2, k_shift=3, k_scale=4, casts=(w_ff_out[l],))
        f = _mm_ksplit(act, w_ff_out_bf, name="ff_out").reshape(batch, seq, d)
        x = _post(f, x1.reshape(batch, seq, d), g_post_mlp[l], mod3, 5)
    return x
```

```python
import functools

import jax
import jax.numpy as jnp
from jax import lax
from jax.experimental import pallas as pl
from jax.experimental.pallas import tpu as pltpu

F32 = jnp.float32
BF16 = jnp.bfloat16

HEAD_DIM = 128
LANES = 128
BF16_SUBLANES = 16
SWA_WINDOW = 128
SWA_GROUP = 4
ROPE_THETA = 10000.0
NORM_EPS = 1e-6
LOG2E = 1.4426950408889634
SB_UNDERFLOW_LOG2 = -152.0
N_MOD = 6
MIB = 1024 * 1024
VMEM_LIMIT_BYTES = 56 * MIB


def _tile(dim, pref):
    t = min(dim, pref)
    while dim % t:
        t //= 2
    return t


def _cparams(semantics):
    return pltpu.CompilerParams(dimension_semantics=semantics, vmem_limit_bytes=VMEM_LIMIT_BYTES)


def _ada_kernel(c_ref, w_ref, b_ref, o_ref):
    c = c_ref[...]
    act = (c * jax.nn.sigmoid(c)).astype(BF16)
    o_ref[...] = jnp.dot(act, w_ref[...].astype(BF16), preferred_element_type=F32) + b_ref[...]


def _ada_mod(c_pad, w_ada, b_ada):
    rows, d = c_pad.shape
    n = w_ada.shape[1]
    tn = _tile(n, 1024)
    return pl.pallas_call(
        _ada_kernel,
        out_shape=jax.ShapeDtypeStruct((rows, n), F32),
        grid=(n // tn,),
        in_specs=[
            pl.BlockSpec((rows, d), lambda j: (0, 0)),
            pl.BlockSpec((d, tn), lambda j: (0, j)),
            pl.BlockSpec((1, tn), lambda j: (0, j)),
        ],
        out_specs=pl.BlockSpec((rows, tn), lambda j: (0, j)),
        compiler_params=_cparams(("arbitrary",)),
        name="ada_mod",
    )(c_pad, w_ada, b_ada.reshape(1, n))


def _rms(x, g):
    y = x * lax.rsqrt(jnp.mean(x * x, axis=-1, keepdims=True) + NORM_EPS)
    return y * g


def _prenorm_kernel(x_ref, g_ref, scale_ref, shift_ref, h_ref):
    h = _rms(x_ref[0], g_ref[...]) * (1.0 + scale_ref[0]) + shift_ref[0]
    h_ref[0] = h.astype(h_ref.dtype)


def _mod_spec(d, k):
    return pl.BlockSpec((1, 1, d), lambda b, i: (b * N_MOD + k, 0, 0))


def _prenorm(x, g, mod3, k_shift, k_scale):
    b, s, d = x.shape
    ts = _tile(s, 512)
    row = pl.BlockSpec((1, ts, d), lambda bb, i: (bb, i, 0))
    return pl.pallas_call(
        _prenorm_kernel,
        out_shape=jax.ShapeDtypeStruct((b, s, d), BF16),
        grid=(b, s // ts),
        in_specs=[row, pl.BlockSpec((1, d), lambda bb, i: (0, 0)),
                  _mod_spec(d, k_scale), _mod_spec(d, k_shift)],
        out_specs=row,
        compiler_params=_cparams(("arbitrary", "arbitrary")),
        name="prenorm",
    )(x, g.reshape(1, d), mod3, mod3)


def _post_pre_math(y, x, gpost, gate, gpre, scale, shift):
    x1 = x + _rms(y, gate * gpost)
    h = _rms(x1, gpre * (1.0 + scale)) + shift
    return x1, h


def _first_h2_kernel(y_ref, x_ref, gpost_ref, gate_ref, gpre_ref, scale_ref, shift_ref, h_ref):
    _, h = _post_pre_math(y_ref[...], x_ref[...], gpost_ref[...], gate_ref[0], gpre_ref[...],
                          scale_ref[0], shift_ref[0])
    h_ref[...] = h.astype(h_ref.dtype)


def _first_h2(y, x, gpost, gpre, mod3, rows, k_gate, k_shift, k_scale):
    d = x.shape[1]
    ts = _tile(rows, 256)
    row = pl.BlockSpec((ts, d), lambda i: (i, 0))
    vec = pl.BlockSpec((1, d), lambda i: (0, 0))

    def mod(k):
        return pl.BlockSpec((1, 1, d), lambda i: (k, 0, 0))

    return pl.pallas_call(
        _first_h2_kernel,
        out_shape=jax.ShapeDtypeStruct((rows, d), BF16),
        grid=(rows // ts,),
        in_specs=[row, row, vec, mod(k_gate), vec, mod(k_scale), mod(k_shift)],
        out_specs=row,
        compiler_params=_cparams(("arbitrary",)),
        name="first_h2",
    )(y, x, gpost.reshape(1, d), mod3, gpre.reshape(1, d), mod3, mod3)


def _mlp_in_kernel(*refs, chunk, n_casts):
    (y_ref, x_ref, gpost_ref, gate_ref, gpre_ref, scale_ref, shift_ref, h_first_hbm,
     w_ref) = refs[:9]
    cast_in = refs[9:9 + n_casts]
    act_ref, x1_ref = refs[9 + n_casts:11 + n_casts]
    cast_out = refs[11 + n_casts:11 + 2 * n_casts]
    h_even, h_odd, sem = refs[11 + 2 * n_casts:]
    i = pl.program_id(0)
    j = pl.program_id(1)

    @pl.when(jnp.logical_and(i == 0, j == 0))
    def _():
        first = pltpu.make_async_copy(h_first_hbm, h_even, sem)
        first.start()
        first.wait()

    for src, dst in zip(cast_in, cast_out):
        dst[...] = src[...].astype(dst.dtype)

    def step(h_cur, h_next):
        x1, h = _post_pre_math(y_ref[...], x_ref[...], gpost_ref[...], gate_ref[0], gpre_ref[...],
                               scale_ref[0], shift_ref[0])
        x1_ref[...] = x1
        h_next[pl.ds(pl.multiple_of(j * chunk, chunk), chunk), :] = h.astype(h_next.dtype)
        acc = jnp.dot(h_cur[...], w_ref[...], preferred_element_type=F32)
        act_ref[...] = jnp.square(jnp.maximum(acc, 0.0)).astype(act_ref.dtype)

    @pl.when(i % 2 == 0)
    def _():
        step(h_even, h_odd)

    @pl.when(i % 2 == 1)
    def _():
        step(h_odd, h_even)


def _mlp_in(y, x, gpost, gpre, mod3, w, *, seq, k_gate, k_shift, k_scale, casts=()):
    t, d = x.shape
    n = w.shape[1]
    tm = _tile(seq, 1024)
    tn = _tile(n, 1024)
    ni, nj = t // tm, n // tn
    chunk = tm // nj
    assert tm % nj == 0 and chunk % BF16_SUBLANES == 0
    blocks_per_batch = seq // tm

    def nxt(i):
        return (i + 1) % ni

    rows = pl.BlockSpec((chunk, d), lambda i, j: (nxt(i) * nj + j, 0))
    vec = pl.BlockSpec((1, d), lambda i, j: (0, 0))

    def mod(k):
        return pl.BlockSpec((1, 1, d),
                            lambda i, j: ((nxt(i) // blocks_per_batch) * N_MOD + k, 0, 0))

    h_first = _first_h2(y, x, gpost, gpre, mod3, tm, k_gate, k_shift, k_scale)
    cast_in_specs, cast_out_specs, cast_args, cast_shapes = _cast_plan(casts, ni, nj)
    outs = pl.pallas_call(
        functools.partial(_mlp_in_kernel, chunk=chunk, n_casts=len(casts)),
        out_shape=[jax.ShapeDtypeStruct((t, n), BF16), jax.ShapeDtypeStruct((t, d), F32)]
        + cast_shapes,
        grid=(ni, nj),
        in_specs=[rows, rows, vec, mod(k_gate), vec, mod(k_scale), mod(k_shift),
                  pl.BlockSpec(memory_space=pl.ANY),
                  pl.BlockSpec((d, tn), lambda i, j: (0, j))] + cast_in_specs,
        out_specs=[pl.BlockSpec((tm, tn), lambda i, j: (i, j)), rows] + cast_out_specs,
        scratch_shapes=[pltpu.VMEM((tm, d), BF16), pltpu.VMEM((tm, d), BF16),
                        pltpu.SemaphoreType.DMA(())],
        compiler_params=_cparams(("arbitrary", "arbitrary")),
        name="ff_in",
    )(y, x, gpost.reshape(1, d), mod3, gpre.reshape(1, d), mod3, mod3, h_first, w, *cast_args)
    return outs[0], outs[1], [o.reshape(-1, o.shape[-1]) for o in outs[2:]]


def _post_kernel(y_ref, x_ref, gpost_ref, gate_ref, o_ref):
    o_ref[0] = x_ref[0] + gate_ref[0] * _rms(y_ref[0], gpost_ref[...])


def _post(y, x, gpost, mod3, k_gate):
    b, s, d = x.shape
    ts = _tile(s, 512)
    row = pl.BlockSpec((1, ts, d), lambda bb, i: (bb, i, 0))
    return pl.pallas_call(
        _post_kernel,
        out_shape=jax.ShapeDtypeStruct((b, s, d), F32),
        grid=(b, s // ts),
        in_specs=[row, row, pl.BlockSpec((1, d), lambda bb, i: (0, 0)), _mod_spec(d, k_gate)],
        out_specs=row,
        compiler_params=_cparams(("arbitrary", "arbitrary")),
        name="post_norm",
    )(y, x, gpost.reshape(1, d), mod3)


def _rope_proj_kernel(*refs, full_rope_tiles, partial_chunks, n_casts):
    a_ref, w_ref, cos_ref, sin_ref = refs[:4]
    cast_in = refs[4:4 + n_casts]
    o_ref = refs[4 + n_casts]
    cast_out = refs[5 + n_casts:]
    for src, dst in zip(cast_in, cast_out):
        dst[...] = src[...].astype(dst.dtype)
    j = pl.program_id(1)
    acc = jnp.dot(a_ref[...], w_ref[...], preferred_element_type=F32)
    cos, sin = cos_ref[...], sin_ref[...]
    head = j <= full_rope_tiles
    tail = j < full_rope_tiles
    tables = {True: (jnp.where(head, cos, 1.0), jnp.where(head, sin, 0.0)),
              False: (jnp.where(tail, cos, 1.0), jnp.where(tail, sin, 0.0))}
    for c in range(acc.shape[1] // LANES):
        cols = slice(c * LANES, (c + 1) * LANES)
        cos_c, sin_c = tables[c < partial_chunks]
        blk = acc[:, cols]
        o_ref[:, cols] = (blk * cos_c + pltpu.roll(blk, HEAD_DIM // 2, 1) * sin_c).astype(o_ref.dtype)


def _rope_proj(h, w, cos_full, sin_signed, *, n_rope, seq, casts=()):
    t, d = h.shape
    n = w.shape[1]
    tm = _tile(seq, 1024)
    tn = _tile(n, 1024)
    assert n_rope % LANES == 0 and tn % LANES == 0
    n_seq_tiles = seq // tm
    ni, nj = t // tm, n // tn
    cast_in_specs, cast_out_specs, cast_args, cast_shapes = _cast_plan(casts, ni, nj)
    kern = functools.partial(_rope_proj_kernel, full_rope_tiles=n_rope // tn,
                             partial_chunks=(n_rope % tn) // LANES, n_casts=len(casts))
    outs = pl.pallas_call(
        kern,
        out_shape=[jax.ShapeDtypeStruct((t, n), BF16)] + cast_shapes,
        grid=(ni, nj),
        in_specs=[
            pl.BlockSpec((tm, d), lambda i, j: (i, 0)),
            pl.BlockSpec((d, tn), lambda i, j: (0, j)),
            pl.BlockSpec((tm, HEAD_DIM), lambda i, j: (i % n_seq_tiles, 0)),
            pl.BlockSpec((tm, HEAD_DIM), lambda i, j: (i % n_seq_tiles, 0)),
        ] + cast_in_specs,
        out_specs=[pl.BlockSpec((tm, tn), lambda i, j: (i, j))] + cast_out_specs,
        compiler_params=_cparams(("arbitrary", "arbitrary")),
        name="swa_proj",
    )(h, w, cos_full, sin_signed, *cast_args)
    return outs[0], [o.reshape(-1, o.shape[-1]) for o in outs[1:]]


def _swa_kernel(sink_ref, q_ref, kp_ref, k_ref, vp_ref, v_ref, o_ref, p_ref, *, tq):
    i = pl.program_id(1)
    kvh = pl.program_id(2)
    w = SWA_WINDOW
    rows = SWA_GROUP * w
    qi = lax.broadcasted_iota(jnp.int32, (rows, 2 * w), 0) % w
    kj = lax.broadcasted_iota(jnp.int32, (rows, 2 * w), 1)
    band = jnp.logical_and(kj > qi, kj <= qi + w)
    first = jnp.logical_and(band, jnp.logical_or(kj >= w, i > 0))
    sink_fill = jnp.concatenate(
        [jnp.full((w, 2 * w), sink_ref[kvh * SWA_GROUP + g], F32) for g in range(SWA_GROUP)], axis=0)
    masked_fill = jnp.where(kj == 0, sink_fill, -jnp.inf)
    key_row = lax.broadcasted_iota(jnp.int32, (2 * w, HEAD_DIM), 0)
    ones = jnp.ones((2 * w, HEAD_DIM), BF16)
    scale = HEAD_DIM ** -0.5
    n_sub = tq // w
    def probs_stage(s):
        q = jnp.concatenate(
            [q_ref[s * w:(s + 1) * w, g * HEAD_DIM:(g + 1) * HEAD_DIM] for g in range(SWA_GROUP)],
            axis=0)
        if s == 0:
            k = jnp.concatenate([kp_ref[...], k_ref[0:w, :]], axis=0)
            valid = first
        else:
            k = k_ref[(s - 1) * w:(s + 1) * w, :]
            valid = band
        sc = lax.dot_general(q, k, (((1,), (1,)), ((), ())), preferred_element_type=F32) * scale
        sc = jnp.where(valid, sc, masked_fill)
        p_ref[s] = jnp.exp(sc - jnp.max(sc, axis=-1, keepdims=True)).astype(BF16)

    def values_stage(s):
        if s == 0:
            v = jnp.concatenate([vp_ref[...], v_ref[0:w, :]], axis=0)
        else:
            v = v_ref[(s - 1) * w:(s + 1) * w, :]
        v = jnp.where(key_row == 0, jnp.zeros_like(v), v)
        ov = jnp.dot(p_ref[s], jnp.concatenate([v, ones], axis=1), preferred_element_type=F32)
        o = ov[:, :HEAD_DIM] / ov[:, HEAD_DIM:]
        for g in range(SWA_GROUP):
            o_ref[s * w:(s + 1) * w, g * HEAD_DIM:(g + 1) * HEAD_DIM] = (
                o[g * w:(g + 1) * w, :].astype(o_ref.dtype))

    for s in range(n_sub):
        probs_stage(s)
        if s >= 1:
            values_stage(s - 1)
    values_stage(n_sub - 1)


def _swa(proj, sinks, *, batch, seq, q_heads, k_col, v_col):
    kv_heads = q_heads // SWA_GROUP
    w = SWA_WINDOW
    tq = _tile(seq, 1024)
    nq = seq // tq
    sub = tq // w
    gw = SWA_GROUP * HEAD_DIM
    kb, vb = k_col // HEAD_DIM, v_col // HEAD_DIM

    def prev_rows(b, i):
        return b * (seq // w) + jnp.maximum(i * sub - 1, 0)

    grid_spec = pltpu.PrefetchScalarGridSpec(
        num_scalar_prefetch=1,
        grid=(batch, nq, kv_heads),
        in_specs=[
            pl.BlockSpec((tq, gw), lambda b, i, h, s: (b * nq + i, h)),
            pl.BlockSpec((w, HEAD_DIM), lambda b, i, h, s: (prev_rows(b, i), kb + h)),
            pl.BlockSpec((tq, HEAD_DIM), lambda b, i, h, s: (b * nq + i, kb + h)),
            pl.BlockSpec((w, HEAD_DIM), lambda b, i, h, s: (prev_rows(b, i), vb + h)),
            pl.BlockSpec((tq, HEAD_DIM), lambda b, i, h, s: (b * nq + i, vb + h)),
        ],
        out_specs=pl.BlockSpec((tq, gw), lambda b, i, h, s: (b * nq + i, h)),
        scratch_shapes=[pltpu.VMEM((sub, SWA_GROUP * w, 2 * w), BF16)],
    )
    return pl.pallas_call(
        functools.partial(_swa_kernel, tq=tq),
        out_shape=jax.ShapeDtypeStruct((batch * seq, q_heads * HEAD_DIM), BF16),
        grid_spec=grid_spec,
        compiler_params=_cparams(("arbitrary", "arbitrary", "arbitrary")),
        name="swa_attention",
    )(sinks, proj, proj, proj, proj, proj)


def _sb_kernel(q_ref, kc_ref, kp_ref, vc_ref, vp_ref, kv_hbm, o_ref, hilo0_ref, hilo1_ref, lb0_ref,
               lb1_ref, acc_ref, r_ref, kbuf_ref, vbuf_ref, sem, *, tq, hp, seq, k_col, v_col):
    hilo_refs = (hilo0_ref, hilo1_ref)
    lb_refs = (lb0_ref, lb1_ref)
    i = pl.program_id(2)
    row = lax.broadcasted_iota(jnp.int32, (tq, tq), 0)
    col = lax.broadcasted_iota(jnp.int32, (tq, tq), 1)
    strict_lower = col < row
    neg_later = jnp.where(row > col, -1.0, 0.0).astype(BF16)
    neg_later = jnp.concatenate([neg_later, neg_later], axis=0)
    z_scale = HEAD_DIM ** -0.5 * LOG2E
    sign_bit = jnp.uint32(0x80000000)

    def logits_stage(slot, k_ref, diag, heads=range(hp)):
        for h in heads:
            cols = slice(h * HEAD_DIM, (h + 1) * HEAD_DIM)
            q = q_ref[:, cols]
            k = k_ref[:, cols]
            z = lax.dot_general(q, k, (((1,), (1,)), ((), ())),
                                preferred_element_type=F32) * z_scale
            neg_abs = lax.bitcast_convert_type(
                lax.bitcast_convert_type(z, jnp.uint32) | sign_bit, F32)
            softplus = jnp.maximum(z, 0.0) + jnp.log(1.0 + jnp.exp2(neg_abs)) * LOG2E
            log_beta = z - softplus
            if diag:
                lb_refs[slot][h] = jnp.where(strict_lower, log_beta, -jnp.inf)
                softplus = jnp.where(strict_lower, softplus, 0.0)
                r_ref[h] = -jnp.sum(softplus, axis=-1, keepdims=True)
            else:
                r_sum = r_ref[h]
                lb_refs[slot][h] = log_beta + r_sum
                r_ref[h] = r_sum - jnp.sum(softplus, axis=-1, keepdims=True)
            hi = softplus.astype(BF16)
            lo = (softplus - hi.astype(F32)).astype(BF16)
            hilo_refs[slot][h] = jnp.concatenate([hi, lo], axis=1)

    def weights_stage(slot, v_ref, first, heads=range(hp)):
        for h in heads:
            cols = slice(h * HEAD_DIM, (h + 1) * HEAD_DIM)
            v = v_ref[:, cols]
            suffix = jnp.dot(hilo_refs[slot][h], neg_later, preferred_element_type=F32)
            att = jnp.exp2(lb_refs[slot][h] + suffix)
            pv = jnp.dot(att.astype(BF16), v, preferred_element_type=F32)
            if first:
                acc_ref[h] = pv
            else:
                acc_ref[h] += pv

    def any_row_alive():
        top = functools.reduce(
            jnp.maximum, [jnp.max(r_ref[h], axis=0, keepdims=True) for h in range(hp)])
        return top[0, 0] > SB_UNDERFLOW_LOG2

    def staggered(logits_stages, weights_stages):
        lag = 2
        for k, stage in enumerate(logits_stages):
            stage()
            if k >= lag:
                weights_stages[k - lag]()
        for stage in weights_stages[max(len(logits_stages) - lag, 0):]:
            stage()

    def per_head(stage, *args):
        return [functools.partial(stage, *args, heads=(h,)) for h in range(hp)]

    @pl.when(i == 0)
    def _():
        staggered(per_head(logits_stage, 0, kc_ref, True), per_head(weights_stage, 0, vc_ref, True))

    @pl.when(i > 0)
    def _():
        staggered(
            per_head(logits_stage, 0, kc_ref, True) + per_head(logits_stage, 1, kp_ref, False),
            per_head(weights_stage, 0, vc_ref, True) + per_head(weights_stage, 1, vp_ref, False))

        def cond(state):
            n, alive = state
            return jnp.logical_and(n <= i, alive)

        def body(state):
            n, _ = state
            row0 = pl.multiple_of(pl.program_id(0) * seq + (i - n) * tq, tq)
            col0 = pl.program_id(1) * (hp * HEAD_DIM)
            copies = [
                pltpu.make_async_copy(
                    kv_hbm.at[pl.ds(row0, tq), pl.ds(pl.multiple_of(c + col0, LANES), hp * HEAD_DIM)],
                    buf, sem.at[s])
                for s, (c, buf) in enumerate(((k_col, kbuf_ref), (v_col, vbuf_ref)))]
            for cp in copies:
                cp.start()
            for cp in copies:
                cp.wait()
            staggered(per_head(logits_stage, 1, kbuf_ref, False),
                      per_head(weights_stage, 1, vbuf_ref, False))
            return n + 1, any_row_alive()

        lax.while_loop(cond, body, (jnp.int32(2), any_row_alive()))

    for h in range(hp):
        o_ref[:, h * HEAD_DIM:(h + 1) * HEAD_DIM] = acc_ref[h].astype(o_ref.dtype)


def _sb(proj, *, batch, seq, heads, q_col, k_col, v_col):
    tq = _tile(seq, 256)
    nq = seq // tq
    hp = next(c for c in (16, 8, 4, 2, 1) if heads % c == 0)
    hw = hp * HEAD_DIM
    assert q_col % hw == 0 and k_col % hw == 0 and v_col % hw == 0
    qb, kb, vb = q_col // hw, k_col // hw, v_col // hw

    def cur(col):
        return pl.BlockSpec((tq, hw), lambda b, h, i: (b * nq + i, col + h))

    def prev(col):
        return pl.BlockSpec((tq, hw), lambda b, h, i: (b * nq + jnp.maximum(i - 1, 0), col + h))

    return pl.pallas_call(
        functools.partial(_sb_kernel, tq=tq, hp=hp, seq=seq, k_col=k_col, v_col=v_col),
        out_shape=jax.ShapeDtypeStruct((batch * seq, heads * HEAD_DIM), BF16),
        grid=(batch, heads // hp, nq),
        in_specs=[cur(qb), cur(kb), prev(kb), cur(vb), prev(vb),
                  pl.BlockSpec(memory_space=pl.ANY)],
        out_specs=pl.BlockSpec((tq, hw), lambda b, h, i: (b * nq + i, h)),
        scratch_shapes=[pltpu.VMEM((hp, tq, 2 * tq), BF16), pltpu.VMEM((hp, tq, 2 * tq), BF16),
                        pltpu.VMEM((hp, tq, tq), F32), pltpu.VMEM((hp, tq, tq), F32),
                        pltpu.VMEM((hp, tq, HEAD_DIM), F32), pltpu.VMEM((hp, tq, 1), F32),
                        pltpu.VMEM((tq, hw), BF16), pltpu.VMEM((tq, hw), BF16),
                        pltpu.SemaphoreType.DMA((2,))],
        compiler_params=_cparams(("arbitrary", "arbitrary", "arbitrary")),
        name="sb_attention",
    )(proj, proj, proj, proj, proj, proj)


def _merge_kernel(a1_ref, a2_ref, w1_ref, w2_ref, ga_ref, gb_ref, o_ref):
    y1 = jnp.dot(a1_ref[...], w1_ref[...], preferred_element_type=F32)
    y2 = jnp.dot(a2_ref[...], w2_ref[...], preferred_element_type=F32)
    o = ga_ref[...].astype(F32) * y1 + gb_ref[...].astype(F32) * y2
    o_ref[...] = o.astype(o_ref.dtype)


def _merge(a_swa, a_sb, w_swa, w_sb, gates):
    t, k1 = a_swa.shape
    k2 = a_sb.shape[1]
    d = w_swa.shape[1]
    tm = _tile(t, 1024)
    tn = _tile(d, 1024)
    ga, gb = 0, d // tn
    return pl.pallas_call(
        _merge_kernel,
        out_shape=jax.ShapeDtypeStruct((t, d), BF16),
        grid=(t // tm, d // tn),
        in_specs=[
            pl.BlockSpec((tm, k1), lambda i, j: (i, 0)),
            pl.BlockSpec((tm, k2), lambda i, j: (i, 0)),
            pl.BlockSpec((k1, tn), lambda i, j: (0, j)),
            pl.BlockSpec((k2, tn), lambda i, j: (0, j)),
            pl.BlockSpec((tm, tn), lambda i, j: (i, ga + j)),
            pl.BlockSpec((tm, tn), lambda i, j: (i, gb + j)),
        ],
        out_specs=pl.BlockSpec((tm, tn), lambda i, j: (i, j)),
        compiler_params=_cparams(("arbitrary", "arbitrary")),
        name="branch_merge",
    )(a_swa, a_sb, w_swa, w_sb, gates, gates)


def _mm_kernel(*refs, epilogue, has_bias, n_casts):
    a_ref, w_ref = refs[:2]
    b_ref = refs[2] if has_bias else None
    first_cast = 2 + has_bias
    cast_in = refs[first_cast:first_cast + n_casts]
    o_ref = refs[first_cast + n_casts]
    cast_out = refs[first_cast + n_casts + 1:]
    for src, dst in zip(cast_in, cast_out):
        dst[...] = src[...].astype(dst.dtype)
    acc = jnp.dot(a_ref[...], w_ref[...], preferred_element_type=F32)
    if has_bias:
        acc = acc + b_ref[...]
    if epilogue == "relu_sq":
        acc = jnp.square(jnp.maximum(acc, 0.0))
    elif epilogue == "sigmoid":
        acc = 0.5 * jnp.tanh(0.5 * acc) + 0.5
    o_ref[...] = acc.astype(o_ref.dtype)


def _cast_plan(casts, ni, nj):
    steps = ni * nj
    in_specs, out_specs, args, shapes = [], [], [], []
    for c in casts:
        c, col0, cols = c if isinstance(c, tuple) else (c, 0, c.shape[1])
        rows = c.shape[0]
        if col0 == 0 and cols == c.shape[1] and rows % (steps * BF16_SUBLANES) == 0:
            slab = rows // steps
            spec = pl.BlockSpec((1, slab, cols), lambda i, j: (i * nj + j, 0, 0))
            in_specs.append(spec)
            out_specs.append(spec)
            args.append(c.reshape(steps, slab, cols))
            shapes.append(jax.ShapeDtypeStruct((steps, slab, cols), BF16))
        else:
            slab = cols // steps
            assert cols % (steps * LANES) == 0 and col0 % slab == 0, (c.shape, col0, cols, steps)
            first = col0 // slab
            in_specs.append(pl.BlockSpec((rows, slab), lambda i, j: (0, first + i * nj + j)))
            out_specs.append(pl.BlockSpec((rows, slab), lambda i, j: (0, i * nj + j)))
            args.append(c)
            shapes.append(jax.ShapeDtypeStruct((rows, cols), BF16))
    return in_specs, out_specs, args, shapes


def _mm(a, w, out_dtype, *, bias=None, epilogue=None, casts=(), tn_pref=1024, name):
    t, k = a.shape
    n = w.shape[1]
    tm = _tile(t, 1024)
    tn = tn_pref if n % tn_pref == 0 else _tile(n, 1024)
    ni, nj = t // tm, n // tn
    in_specs = [pl.BlockSpec((tm, k), lambda i, j: (i, 0)),
                pl.BlockSpec((k, tn), lambda i, j: (0, j))]
    args = [a, w]
    if bias is not None:
        in_specs.append(pl.BlockSpec((1, tn), lambda i, j: (0, j)))
        args.append(bias.reshape(1, n))
    cast_in_specs, cast_out_specs, cast_args, cast_shapes = _cast_plan(casts, ni, nj)
    in_specs += cast_in_specs
    args += cast_args
    out_specs = [pl.BlockSpec((tm, tn), lambda i, j: (i, j))] + cast_out_specs
    out_shape = [jax.ShapeDtypeStruct((t, n), out_dtype)] + cast_shapes
    outs = pl.pallas_call(
        functools.partial(_mm_kernel, epilogue=epilogue, has_bias=bias is not None,
                          n_casts=len(casts)),
        out_shape=out_shape,
        grid=(ni, nj),
        in_specs=in_specs,
        out_specs=out_specs,
        compiler_params=_cparams(("arbitrary", "arbitrary")),
        name=name,
    )(*args)
    return outs[0], [o.reshape(-1, o.shape[-1]) for o in outs[1:]]


def _mm_ksplit_kernel(a_ref, w_ref, o_ref):
    @pl.when(pl.program_id(2) == 0)
    def _():
        o_ref[...] = jnp.zeros_like(o_ref)

    o_ref[...] += jnp.dot(a_ref[...], w_ref[...], preferred_element_type=F32)


def _mm_ksplit(a, w, *, name):
    t, k = a.shape
    n = w.shape[1]
    tm = _tile(t, 1024)
    tn = _tile(n, 1024)
    tk = _tile(k, 4096)
    return pl.pallas_call(
        _mm_ksplit_kernel,
        out_shape=jax.ShapeDtypeStruct((t, n), F32),
        grid=(t // tm, n // tn, k // tk),
        in_specs=[pl.BlockSpec((tm, tk), lambda i, j, kk: (i, kk)),
                  pl.BlockSpec((tk, tn), lambda i, j, kk: (kk, j))],
        out_specs=pl.BlockSpec((tm, tn), lambda i, j, kk: (i, j)),
        compiler_params=_cparams(("arbitrary", "arbitrary", "arbitrary")),
        name=name,
    )(a, w)


def _rope_tables(seq):
    inv_freq = 1.0 / (ROPE_THETA ** (jnp.arange(0, HEAD_DIM, 2, dtype=F32) / HEAD_DIM))
    ang = jnp.arange(seq, dtype=F32)[:, None] * inv_freq[None, :]
    cos, sin = jnp.cos(ang), jnp.sin(ang)
    return jnp.concatenate([cos, cos], axis=-1), jnp.concatenate([-sin, sin], axis=-1)


def kernel(x, c, w_ada, b_ada, g_pre_mix, g_post_mix, g_pre_mlp, g_post_mlp, w_in, attn_sinks,
           w_branch_gate, b_branch_gate, w_proj_swa, w_proj_sb, w_out, w_ff_in, w_ff_out):
    batch, seq, d = x.shape
    depth = w_in.shape[0]
    d_in = w_in.shape[2]
    swa_q_heads = attn_sinks.shape[1]
    swa_q_w = w_proj_swa.shape[1]
    sb_w = w_proj_sb.shape[1]
    swa_kv_w = (d_in - swa_q_w - 3 * sb_w) // 2
    sb_heads = sb_w // HEAD_DIM
    assert swa_q_w == swa_q_heads * HEAD_DIM and swa_kv_w * SWA_GROUP == swa_q_w
    assert seq % SWA_WINDOW == 0
    t = batch * seq

    cos_full, sin_signed = _rope_tables(seq)
    pad_rows = (-batch) % 8
    c_pad = jnp.pad(c, ((0, pad_rows), (0, 0)))

    swa_w = swa_q_w + 2 * swa_kv_w
    n_rope = swa_q_w + swa_kv_w

    for l in range(depth):
        mod = _ada_mod(c_pad, w_ada[l], b_ada[l])[:batch]
        mod3 = mod.reshape(batch * N_MOD, 1, d)

        h = _prenorm(x, g_pre_mix[l], mod3, 0, 1).reshape(t, d)
        proj_swa, (w_sb_in_bf,) = _rope_proj(
            h, w_in[l][:, :swa_w].astype(BF16), cos_full, sin_signed, n_rope=n_rope, seq=seq,
            casts=((w_in[l], swa_w, d_in - swa_w),))
        proj_sb, (w_gate_bf,) = _mm(h, w_sb_in_bf, BF16, casts=(w_branch_gate[l],),
                                    tn_pref=768, name="sb_proj")
        gates, (w_swa_bf, w_sb_bf, w_out_bf, w_ff_in_bf) = _mm(
            h, w_gate_bf, BF16, bias=b_branch_gate[l], epilogue="sigmoid",
            casts=(w_proj_swa[l], w_proj_sb[l], w_out[l], w_ff_in[l]), name="gate_proj")
        a_swa = _swa(proj_swa, attn_sinks[l], batch=batch, seq=seq, q_heads=swa_q_heads,
                     k_col=swa_q_w, v_col=n_rope)
        a_sb = _sb(proj_sb, batch=batch, seq=seq, heads=sb_heads,
                   q_col=0, k_col=sb_w, v_col=2 * sb_w)
        merged = _merge(a_swa, a_sb, w_swa_bf, w_sb_bf, gates)
        mixed, _ = _mm(merged, w_out_bf, F32, name="out_proj")
        act, x1, (w_ff_out_bf,) = _mlp_in(
            mixed, x.reshape(t, d), g_post_mix[l], g_pre_mlp[l], mod3, w_ff_in_bf, seq=seq,
            k_gate=2, k_shift=3, k_scale=4, casts=(w_ff_out[l],))
        f = _mm_ksplit(act, w_ff_out_bf, name="ff_out").reshape(batch, seq, d)
        x = _post(f, x1.reshape(batch, seq, d), g_post_mlp[l], mod3, 5)
    return x
```

```python
import functools

import jax
import jax.numpy as jnp
from jax import lax
from jax.experimental import pallas as pl
from jax.experimental.pallas import tpu as pltpu

F32 = jnp.float32
BF16 = jnp.bfloat16

HEAD_DIM = 128
LANES = 128
BF16_SUBLANES = 16
SWA_WINDOW = 128
SWA_GROUP = 4
ROPE_THETA = 10000.0
NORM_EPS = 1e-6
LOG2E = 1.4426950408889634
SB_UNDERFLOW_LOG2 = -152.0
N_MOD = 6
MIB = 1024 * 1024
VMEM_LIMIT_BYTES = 56 * MIB


def _tile(dim, pref):
    t = min(dim, pref)
    while dim % t:
        t //= 2
    return t


def _cparams(semantics):
    return pltpu.CompilerParams(dimension_semantics=semantics, vmem_limit_bytes=VMEM_LIMIT_BYTES)


def _ada_kernel(c_ref, w_ref, b_ref, o_ref):
    c = c_ref[...]
    act = (c * jax.nn.sigmoid(c)).astype(BF16)
    o_ref[...] = jnp.dot(act, w_ref[...].astype(BF16), preferred_element_type=F32) + b_ref[...]


def _ada_mod(c_pad, w_ada, b_ada):
    rows, d = c_pad.shape
    n = w_ada.shape[1]
    tn = _tile(n, 1024)
    return pl.pallas_call(
        _ada_kernel,
        out_shape=jax.ShapeDtypeStruct((rows, n), F32),
        grid=(n // tn,),
        in_specs=[
            pl.BlockSpec((rows, d), lambda j: (0, 0)),
            pl.BlockSpec((d, tn), lambda j: (0, j)),
            pl.BlockSpec((1, tn), lambda j: (0, j)),
        ],
        out_specs=pl.BlockSpec((rows, tn), lambda j: (0, j)),
        compiler_params=_cparams(("arbitrary",)),
        name="ada_mod",
    )(c_pad, w_ada, b_ada.reshape(1, n))


def _rms(x, g):
    y = x * lax.rsqrt(jnp.mean(x * x, axis=-1, keepdims=True) + NORM_EPS)
    return y * g


def _prenorm_kernel(x_ref, g_ref, scale_ref, shift_ref, h_ref):
    h = _rms(x_ref[0], g_ref[...]) * (1.0 + scale_ref[0]) + shift_ref[0]
    h_ref[0] = h.astype(h_ref.dtype)


def _mod_spec(d, k):
    return pl.BlockSpec((1, 1, d), lambda b, i: (b * N_MOD + k, 0, 0))


def _prenorm(x, g, mod3, k_shift, k_scale):
    b, s, d = x.shape
    ts = _tile(s, 512)
    row = pl.BlockSpec((1, ts, d), lambda bb, i: (bb, i, 0))
    return pl.pallas_call(
        _prenorm_kernel,
        out_shape=jax.ShapeDtypeStruct((b, s, d), BF16),
        grid=(b, s // ts),
        in_specs=[row, pl.BlockSpec((1, d), lambda bb, i: (0, 0)),
                  _mod_spec(d, k_scale), _mod_spec(d, k_shift)],
        out_specs=row,
        compiler_params=_cparams(("arbitrary", "arbitrary")),
        name="prenorm",
    )(x, g.reshape(1, d), mod3, mod3)


def _post_pre_math(y, x, gpost, gate, gpre, scale, shift):
    x1 = x + _rms(y, gate * gpost)
    h = _rms(x1, gpre * (1.0 + scale)) + shift
    return x1, h


def _first_h2_kernel(y_ref, x_ref, gpost_ref, gate_ref, gpre_ref, scale_ref, shift_ref, h_ref):
    _, h = _post_pre_math(y_ref[...], x_ref[...], gpost_ref[...], gate_ref[0], gpre_ref[...],
                          scale_ref[0], shift_ref[0])
    h_ref[...] = h.astype(h_ref.dtype)


def _first_h2(y, x, gpost, gpre, mod3, rows, k_gate, k_shift, k_scale):
    d = x.shape[1]
    ts = _tile(rows, 256)
    row = pl.BlockSpec((ts, d), lambda i: (i, 0))
    vec = pl.BlockSpec((1, d), lambda i: (0, 0))

    def mod(k):
        return pl.BlockSpec((1, 1, d), lambda i: (k, 0, 0))

    return pl.pallas_call(
        _first_h2_kernel,
        out_shape=jax.ShapeDtypeStruct((rows, d), BF16),
        grid=(rows // ts,),
        in_specs=[row, row, vec, mod(k_gate), vec, mod(k_scale), mod(k_shift)],
        out_specs=row,
        compiler_params=_cparams(("arbitrary",)),
        name="first_h2",
    )(y, x, gpost.reshape(1, d), mod3, gpre.reshape(1, d), mod3, mod3)


def _mlp_in_kernel(*refs, chunk, n_casts):
    (y_ref, x_ref, gpost_ref, gate_ref, gpre_ref, scale_ref, shift_ref, h_first_hbm,
     w_ref) = refs[:9]
    cast_in = refs[9:9 + n_casts]
    act_ref, x1_ref = refs[9 + n_casts:11 + n_casts]
    cast_out = refs[11 + n_casts:11 + 2 * n_casts]
    h_even, h_odd, sem = refs[11 + 2 * n_casts:]
    i = pl.program_id(0)
    j = pl.program_id(1)

    @pl.when(jnp.logical_and(i == 0, j == 0))
    def _():
        first = pltpu.make_async_copy(h_first_hbm, h_even, sem)
        first.start()
        first.wait()

    for src, dst in zip(cast_in, cast_out):
        dst[...] = src[...].astype(dst.dtype)

    def step(h_cur, h_next):
        x1, h = _post_pre_math(y_ref[...], x_ref[...], gpost_ref[...], gate_ref[0], gpre_ref[...],
                               scale_ref[0], shift_ref[0])
        x1_ref[...] = x1
        h_next[pl.ds(pl.multiple_of(j * chunk, chunk), chunk), :] = h.astype(h_next.dtype)
        acc = jnp.dot(h_cur[...], w_ref[...], preferred_element_type=F32)
        act_ref[...] = jnp.square(jnp.maximum(acc, 0.0)).astype(act_ref.dtype)

    @pl.when(i % 2 == 0)
    def _():
        step(h_even, h_odd)

    @pl.when(i % 2 == 1)
    def _():
        step(h_odd, h_even)


def _mlp_in(y, x, gpost, gpre, mod3, w, *, seq, k_gate, k_shift, k_scale, casts=()):
    t, d = x.shape
    n = w.shape[1]
    tm = _tile(seq, 1024)
    tn = _tile(n, 1024)
    ni, nj = t // tm, n // tn
    chunk = tm // nj
    assert tm % nj == 0 and chunk % BF16_SUBLANES == 0
    blocks_per_batch = seq // tm

    def nxt(i):
        return (i + 1) % ni

    rows = pl.BlockSpec((chunk, d), lambda i, j: (nxt(i) * nj + j, 0))
    vec = pl.BlockSpec((1, d), lambda i, j: (0, 0))

    def mod(k):
        return pl.BlockSpec((1, 1, d),
                            lambda i, j: ((nxt(i) // blocks_per_batch) * N_MOD + k, 0, 0))

    h_first = _first_h2(y, x, gpost, gpre, mod3, tm, k_gate, k_shift, k_scale)
    cast_in_specs, cast_out_specs, cast_args, cast_shapes = _cast_plan(casts, ni, nj)
    outs = pl.pallas_call(
        functools.partial(_mlp_in_kernel, chunk=chunk, n_casts=len(casts)),
        out_shape=[jax.ShapeDtypeStruct((t, n), BF16), jax.ShapeDtypeStruct((t, d), F32)]
        + cast_shapes,
        grid=(ni, nj),
        in_specs=[rows, rows, vec, mod(k_gate), vec, mod(k_scale), mod(k_shift),
                  pl.BlockSpec(memory_space=pl.ANY),
                  pl.BlockSpec((d, tn), lambda i, j: (0, j))] + cast_in_specs,
        out_specs=[pl.BlockSpec((tm, tn), lambda i, j: (i, j)), rows] + cast_out_specs,
        scratch_shapes=[pltpu.VMEM((tm, d), BF16), pltpu.VMEM((tm, d), BF16),
                        pltpu.SemaphoreType.DMA(())],
        compiler_params=_cparams(("arbitrary", "arbitrary")),
        name="ff_in",
    )(y, x, gpost.reshape(1, d), mod3, gpre.reshape(1, d), mod3, mod3, h_first, w, *cast_args)
    return outs[0], outs[1], [o.reshape(-1, o.shape[-1]) for o in outs[2:]]


def _post_kernel(y_ref, x_ref, gpost_ref, gate_ref, o_ref):
    o_ref[0] = x_ref[0] + gate_ref[0] * _rms(y_ref[0], gpost_ref[...])


def _post(y, x, gpost, mod3, k_gate):
    b, s, d = x.shape
    ts = _tile(s, 512)
    row = pl.BlockSpec((1, ts, d), lambda bb, i: (bb, i, 0))
    return pl.pallas_call(
        _post_kernel,
        out_shape=jax.ShapeDtypeStruct((b, s, d), F32),
        grid=(b, s // ts),
        in_specs=[row, row, pl.BlockSpec((1, d), lambda bb, i: (0, 0)), _mod_spec(d, k_gate)],
        out_specs=row,
        compiler_params=_cparams(("arbitrary", "arbitrary")),
        name="post_norm",
    )(y, x, gpost.reshape(1, d), mod3)


def _rope_proj_kernel(*refs, full_rope_tiles, partial_chunks, n_casts):
    a_ref, w_ref, cos_ref, sin_ref = refs[:4]
    cast_in = refs[4:4 + n_casts]
    o_ref = refs[4 + n_casts]
    cast_out = refs[5 + n_casts:]
    for src, dst in zip(cast_in, cast_out):
        dst[...] = src[...].astype(dst.dtype)
    j = pl.program_id(1)
    acc = jnp.dot(a_ref[...], w_ref[...], preferred_element_type=F32)
    cos, sin = cos_ref[...], sin_ref[...]
    head = j <= full_rope_tiles
    tail = j < full_rope_tiles
    tables = {True: (jnp.where(head, cos, 1.0), jnp.where(head, sin, 0.0)),
              False: (jnp.where(tail, cos, 1.0), jnp.where(tail, sin, 0.0))}
    for c in range(acc.shape[1] // LANES):
        cols = slice(c * LANES, (c + 1) * LANES)
        cos_c, sin_c = tables[c < partial_chunks]
        blk = acc[:, cols]
        o_ref[:, cols] = (blk * cos_c + pltpu.roll(blk, HEAD_DIM // 2, 1) * sin_c).astype(o_ref.dtype)


def _rope_proj(h, w, cos_full, sin_signed, *, n_rope, seq, casts=()):
    t, d = h.shape
    n = w.shape[1]
    tm = _tile(seq, 1024)
    tn = _tile(n, 1024)
    assert n_rope % LANES == 0 and tn % LANES == 0
    n_seq_tiles = seq // tm
    ni, nj = t // tm, n // tn
    cast_in_specs, cast_out_specs, cast_args, cast_shapes = _cast_plan(casts, ni, nj)
    kern = functools.partial(_rope_proj_kernel, full_rope_tiles=n_rope // tn,
                             partial_chunks=(n_rope % tn) // LANES, n_casts=len(casts))
    outs = pl.pallas_call(
        kern,
        out_shape=[jax.ShapeDtypeStruct((t, n), BF16)] + cast_shapes,
        grid=(ni, nj),
        in_specs=[
            pl.BlockSpec((tm, d), lambda i, j: (i, 0)),
            pl.BlockSpec((d, tn), lambda i, j: (0, j)),
            pl.BlockSpec((tm, HEAD_DIM), lambda i, j: (i % n_seq_tiles, 0)),
            pl.BlockSpec((tm, HEAD_DIM), lambda i, j: (i % n_seq_tiles, 0)),
        ] + cast_in_specs,
        out_specs=[pl.BlockSpec((tm, tn), lambda i, j: (i, j))] + cast_out_specs,
        compiler_params=_cparams(("arbitrary", "arbitrary")),
        name="swa_proj",
    )(h, w, cos_full, sin_signed, *cast_args)
    return outs[0], [o.reshape(-1, o.shape[-1]) for o in outs[1:]]


def _swa_kernel(sink_ref, q_ref, kp_ref, k_ref, vp_ref, v_ref, o_ref, p_ref, *, tq):
    i = pl.program_id(1)
    kvh = pl.program_id(2)
    w = SWA_WINDOW
    rows = SWA_GROUP * w
    qi = lax.broadcasted_iota(jnp.int32, (rows, 2 * w), 0) % w
    kj = lax.broadcasted_iota(jnp.int32, (rows, 2 * w), 1)
    band = jnp.logical_and(kj > qi, kj <= qi + w)
    first = jnp.logical_and(band, jnp.logical_or(kj >= w, i > 0))
    sink_fill = jnp.concatenate(
        [jnp.full((w, 2 * w), sink_ref[kvh * SWA_GROUP + g], F32) for g in range(SWA_GROUP)], axis=0)
    masked_fill = jnp.where(kj == 0, sink_fill, -jnp.inf)
    key_row = lax.broadcasted_iota(jnp.int32, (2 * w, HEAD_DIM), 0)
    ones = jnp.ones((2 * w, HEAD_DIM), BF16)
    scale = HEAD_DIM ** -0.5
    n_sub = tq // w
    def probs_stage(s):
        q = jnp.concatenate(
            [q_ref[s * w:(s + 1) * w, g * HEAD_DIM:(g + 1) * HEAD_DIM] for g in range(SWA_GROUP)],
            axis=0)
        if s == 0:
            k = jnp.concatenate([kp_ref[...], k_ref[0:w, :]], axis=0)
            valid = first
        else:
            k = k_ref[(s - 1) * w:(s + 1) * w, :]
            valid = band
        sc = lax.dot_general(q, k, (((1,), (1,)), ((), ())), preferred_element_type=F32) * scale
        sc = jnp.where(valid, sc, masked_fill)
        p_ref[s] = jnp.exp(sc - jnp.max(sc, axis=-1, keepdims=True)).astype(BF16)

    def values_stage(s):
        if s == 0:
            v = jnp.concatenate([vp_ref[...], v_ref[0:w, :]], axis=0)
        else:
            v = v_ref[(s - 1) * w:(s + 1) * w, :]
        v = jnp.where(key_row == 0, jnp.zeros_like(v), v)
        ov = jnp.dot(p_ref[s], jnp.concatenate([v, ones], axis=1), preferred_element_type=F32)
        o = ov[:, :HEAD_DIM] / ov[:, HEAD_DIM:]
        for g in range(SWA_GROUP):
            o_ref[s * w:(s + 1) * w, g * HEAD_DIM:(g + 1) * HEAD_DIM] = (
                o[g * w:(g + 1) * w, :].astype(o_ref.dtype))

    for s in range(n_sub):
        probs_stage(s)
        if s >= 1:
            values_stage(s - 1)
    values_stage(n_sub - 1)


def _swa(proj, sinks, *, batch, seq, q_heads, k_col, v_col):
    kv_heads = q_heads // SWA_GROUP
    w = SWA_WINDOW
    tq = _tile(seq, 4096)
    nq = seq // tq
    sub = tq // w
    gw = SWA_GROUP * HEAD_DIM
    kb, vb = k_col // HEAD_DIM, v_col // HEAD_DIM

    def prev_rows(b, i):
        return b * (seq // w) + jnp.maximum(i * sub - 1, 0)

    grid_spec = pltpu.PrefetchScalarGridSpec(
        num_scalar_prefetch=1,
        grid=(batch, nq, kv_heads),
        in_specs=[
            pl.BlockSpec((tq, gw), lambda b, i, h, s: (b * nq + i, h)),
            pl.BlockSpec((w, HEAD_DIM), lambda b, i, h, s: (prev_rows(b, i), kb + h)),
            pl.BlockSpec((tq, HEAD_DIM), lambda b, i, h, s: (b * nq + i, kb + h)),
            pl.BlockSpec((w, HEAD_DIM), lambda b, i, h, s: (prev_rows(b, i), vb + h)),
            pl.BlockSpec((tq, HEAD_DIM), lambda b, i, h, s: (b * nq + i, vb + h)),
        ],
        out_specs=pl.BlockSpec((tq, gw), lambda b, i, h, s: (b * nq + i, h)),
        scratch_shapes=[pltpu.VMEM((sub, SWA_GROUP * w, 2 * w), BF16)],
    )
    return pl.pallas_call(
        functools.partial(_swa_kernel, tq=tq),
        out_shape=jax.ShapeDtypeStruct((batch * seq, q_heads * HEAD_DIM), BF16),
        grid_spec=grid_spec,
        compiler_params=_cparams(("arbitrary", "arbitrary", "arbitrary")),
        name="swa_attention",
    )(sinks, proj, proj, proj, proj, proj)


def _sb_kernel(q_ref, kc_ref, kp_ref, vc_ref, vp_ref, kv_hbm, o_ref, hilo0_ref, hilo1_ref, lb0_ref,
               lb1_ref, acc_ref, r_ref, kbuf_ref, vbuf_ref, sem, *, tq, hp, seq, k_col, v_col):
    hilo_refs = (hilo0_ref, hilo1_ref)
    lb_refs = (lb0_ref, lb1_ref)
    i = pl.program_id(2)
    row = lax.broadcasted_iota(jnp.int32, (tq, tq), 0)
    col = lax.broadcasted_iota(jnp.int32, (tq, tq), 1)
    strict_lower = col < row
    neg_later = jnp.where(row > col, -1.0, 0.0).astype(BF16)
    neg_later = jnp.concatenate([neg_later, neg_later], axis=0)
    z_scale = HEAD_DIM ** -0.5 * LOG2E
    sign_bit = jnp.uint32(0x80000000)

    def logits_stage(slot, k_ref, diag, heads=range(hp)):
        for h in heads:
            cols = slice(h * HEAD_DIM, (h + 1) * HEAD_DIM)
            q = q_ref[:, cols]
            k = k_ref[:, cols]
            z = lax.dot_general(q, k, (((1,), (1,)), ((), ())),
                                preferred_element_type=F32) * z_scale
            neg_abs = lax.bitcast_convert_type(
                lax.bitcast_convert_type(z, jnp.uint32) | sign_bit, F32)
            softplus = jnp.maximum(z, 0.0) + jnp.log(1.0 + jnp.exp2(neg_abs)) * LOG2E
            log_beta = z - softplus
            if diag:
                lb_refs[slot][h] = jnp.where(strict_lower, log_beta, -jnp.inf)
                softplus = jnp.where(strict_lower, softplus, 0.0)
                r_ref[h] = -jnp.sum(softplus, axis=-1, keepdims=True)
            else:
                r_sum = r_ref[h]
                lb_refs[slot][h] = log_beta + r_sum
                r_ref[h] = r_sum - jnp.sum(softplus, axis=-1, keepdims=True)
            hi = softplus.astype(BF16)
            lo = (softplus - hi.astype(F32)).astype(BF16)
            hilo_refs[slot][h] = jnp.concatenate([hi, lo], axis=1)

    def weights_stage(slot, v_ref, first, heads=range(hp)):
        for h in heads:
            cols = slice(h * HEAD_DIM, (h + 1) * HEAD_DIM)
            v = v_ref[:, cols]
            suffix = jnp.dot(hilo_refs[slot][h], neg_later, preferred_element_type=F32)
            att = jnp.exp2(lb_refs[slot][h] + suffix)
            pv = jnp.dot(att.astype(BF16), v, preferred_element_type=F32)
            if first:
                acc_ref[h] = pv
            else:
                acc_ref[h] += pv

    def any_row_alive():
        top = functools.reduce(
            jnp.maximum, [jnp.max(r_ref[h], axis=0, keepdims=True) for h in range(hp)])
        return top[0, 0] > SB_UNDERFLOW_LOG2

    def staggered(logits_stages, weights_stages):
        lag = 2
        for k, stage in enumerate(logits_stages):
            stage()
            if k >= lag:
                weights_stages[k - lag]()
        for stage in weights_stages[max(len(logits_stages) - lag, 0):]:
            stage()

    def per_head(stage, *args):
        return [functools.partial(stage, *args, heads=(h,)) for h in range(hp)]

    @pl.when(i == 0)
    def _():
        staggered(per_head(logits_stage, 0, kc_ref, True), per_head(weights_stage, 0, vc_ref, True))

    @pl.when(i > 0)
    def _():
        staggered(
            per_head(logits_stage, 0, kc_ref, True) + per_head(logits_stage, 1, kp_ref, False),
            per_head(weights_stage, 0, vc_ref, True) + per_head(weights_stage, 1, vp_ref, False))

        def cond(state):
            n, alive = state
            return jnp.logical_and(n <= i, alive)

        def body(state):
            n, _ = state
            row0 = pl.multiple_of(pl.program_id(0) * seq + (i - n) * tq, tq)
            col0 = pl.program_id(1) * (hp * HEAD_DIM)
            copies = [
                pltpu.make_async_copy(
                    kv_hbm.at[pl.ds(row0, tq), pl.ds(pl.multiple_of(c + col0, LANES), hp * HEAD_DIM)],
                    buf, sem.at[s])
                for s, (c, buf) in enumerate(((k_col, kbuf_ref), (v_col, vbuf_ref)))]
            for cp in copies:
                cp.start()
            for cp in copies:
                cp.wait()
            staggered(per_head(logits_stage, 1, kbuf_ref, False),
                      per_head(weights_stage, 1, vbuf_ref, False))
            return n + 1, any_row_alive()

        lax.while_loop(cond, body, (jnp.int32(2), any_row_alive()))

    for h in range(hp):
        o_ref[:, h * HEAD_DIM:(h + 1) * HEAD_DIM] = acc_ref[h].astype(o_ref.dtype)


def _sb(proj, *, batch, seq, heads, q_col, k_col, v_col):
    tq = _tile(seq, 256)
    nq = seq // tq
    hp = next(c for c in (16, 8, 4, 2, 1) if heads % c == 0)
    hw = hp * HEAD_DIM
    assert q_col % hw == 0 and k_col % hw == 0 and v_col % hw == 0
    qb, kb, vb = q_col // hw, k_col // hw, v_col // hw

    def cur(col):
        return pl.BlockSpec((tq, hw), lambda b, h, i: (b * nq + i, col + h))

    def prev(col):
        return pl.BlockSpec((tq, hw), lambda b, h, i: (b * nq + jnp.maximum(i - 1, 0), col + h))

    return pl.pallas_call(
        functools.partial(_sb_kernel, tq=tq, hp=hp, seq=seq, k_col=k_col, v_col=v_col),
        out_shape=jax.ShapeDtypeStruct((batch * seq, heads * HEAD_DIM), BF16),
        grid=(batch, heads // hp, nq),
        in_specs=[cur(qb), cur(kb), prev(kb), cur(vb), prev(vb),
                  pl.BlockSpec(memory_space=pl.ANY)],
        out_specs=pl.BlockSpec((tq, hw), lambda b, h, i: (b * nq + i, h)),
        scratch_shapes=[pltpu.VMEM((hp, tq, 2 * tq), BF16), pltpu.VMEM((hp, tq, 2 * tq), BF16),
                        pltpu.VMEM((hp, tq, tq), F32), pltpu.VMEM((hp, tq, tq), F32),
                        pltpu.VMEM((hp, tq, HEAD_DIM), F32), pltpu.VMEM((hp, tq, 1), F32),
                        pltpu.VMEM((tq, hw), BF16), pltpu.VMEM((tq, hw), BF16),
                        pltpu.SemaphoreType.DMA((2,))],
        compiler_params=_cparams(("arbitrary", "arbitrary", "arbitrary")),
        name="sb_attention",
    )(proj, proj, proj, proj, proj, proj)


def _merge_kernel(a1_ref, a2_ref, w1_ref, w2_ref, ga_ref, gb_ref, o_ref):
    y1 = jnp.dot(a1_ref[...], w1_ref[...], preferred_element_type=F32)
    y2 = jnp.dot(a2_ref[...], w2_ref[...], preferred_element_type=F32)
    o = ga_ref[...].astype(F32) * y1 + gb_ref[...].astype(F32) * y2
    o_ref[...] = o.astype(o_ref.dtype)


def _merge(a_swa, a_sb, w_swa, w_sb, gates):
    t, k1 = a_swa.shape
    k2 = a_sb.shape[1]
    d = w_swa.shape[1]
    tm = _tile(t, 1024)
    tn = _tile(d, 1024)
    ga, gb = 0, d // tn
    return pl.pallas_call(
        _merge_kernel,
        out_shape=jax.ShapeDtypeStruct((t, d), BF16),
        grid=(t // tm, d // tn),
        in_specs=[
            pl.BlockSpec((tm, k1), lambda i, j: (i, 0)),
            pl.BlockSpec((tm, k2), lambda i, j: (i, 0)),
            pl.BlockSpec((k1, tn), lambda i, j: (0, j)),
            pl.BlockSpec((k2, tn), lambda i, j: (0, j)),
            pl.BlockSpec((tm, tn), lambda i, j: (i, ga + j)),
            pl.BlockSpec((tm, tn), lambda i, j: (i, gb + j)),
        ],
        out_specs=pl.BlockSpec((tm, tn), lambda i, j: (i, j)),
        compiler_params=_cparams(("arbitrary", "arbitrary")),
        name="branch_merge",
    )(a_swa, a_sb, w_swa, w_sb, gates, gates)


def _mm_kernel(*refs, epilogue, has_bias, n_casts):
    a_ref, w_ref = refs[:2]
    b_ref = refs[2] if has_bias else None
    first_cast = 2 + has_bias
    cast_in = refs[first_cast:first_cast + n_casts]
    o_ref = refs[first_cast + n_casts]
    cast_out = refs[first_cast + n_casts + 1:]
    for src, dst in zip(cast_in, cast_out):
        dst[...] = src[...].astype(dst.dtype)
    acc = jnp.dot(a_ref[...], w_ref[...], preferred_element_type=F32)
    if has_bias:
        acc = acc + b_ref[...]
    if epilogue == "relu_sq":
        acc = jnp.square(jnp.maximum(acc, 0.0))
    elif epilogue == "sigmoid":
        acc = 0.5 * jnp.tanh(0.5 * acc) + 0.5
    o_ref[...] = acc.astype(o_ref.dtype)


def _cast_plan(casts, ni, nj):
    steps = ni * nj
    in_specs, out_specs, args, shapes = [], [], [], []
    for c in casts:
        c, col0, cols = c if isinstance(c, tuple) else (c, 0, c.shape[1])
        rows = c.shape[0]
        if col0 == 0 and cols == c.shape[1] and rows % (steps * BF16_SUBLANES) == 0:
            slab = rows // steps
            spec = pl.BlockSpec((1, slab, cols), lambda i, j: (i * nj + j, 0, 0))
            in_specs.append(spec)
            out_specs.append(spec)
            args.append(c.reshape(steps, slab, cols))
            shapes.append(jax.ShapeDtypeStruct((steps, slab, cols), BF16))
        else:
            slab = cols // steps
            assert cols % (steps * LANES) == 0 and col0 % slab == 0, (c.shape, col0, cols, steps)
            first = col0 // slab
            in_specs.append(pl.BlockSpec((rows, slab), lambda i, j: (0, first + i * nj + j)))
            out_specs.append(pl.BlockSpec((rows, slab), lambda i, j: (0, i * nj + j)))
            args.append(c)
            shapes.append(jax.ShapeDtypeStruct((rows, cols), BF16))
    return in_specs, out_specs, args, shapes


def _mm(a, w, out_dtype, *, bias=None, epilogue=None, casts=(), tn_pref=1024, name):
    t, k = a.shape
    n = w.shape[1]
    tm = _tile(t, 1024)
    tn = tn_pref if n % tn_pref == 0 else _tile(n, 1024)
    ni, nj = t // tm, n // tn
    in_specs = [pl.BlockSpec((tm, k), lambda i, j: (i, 0)),
                pl.BlockSpec((k, tn), lambda i, j: (0, j))]
    args = [a, w]
    if bias is not None:
        in_specs.append(pl.BlockSpec((1, tn), lambda i, j: (0, j)))
        args.append(bias.reshape(1, n))
    cast_in_specs, cast_out_specs, cast_args, cast_shapes = _cast_plan(casts, ni, nj)
    in_specs += cast_in_specs
    args += cast_args
    out_specs = [pl.BlockSpec((tm, tn), lambda i, j: (i, j))] + cast_out_specs
    out_shape = [jax.ShapeDtypeStruct((t, n), out_dtype)] + cast_shapes
    outs = pl.pallas_call(
        functools.partial(_mm_kernel, epilogue=epilogue, has_bias=bias is not None,
                          n_casts=len(casts)),
        out_shape=out_shape,
        grid=(ni, nj),
        in_specs=in_specs,
        out_specs=out_specs,
        compiler_params=_cparams(("arbitrary", "arbitrary")),
        name=name,
    )(*args)
    return outs[0], [o.reshape(-1, o.shape[-1]) for o in outs[1:]]


def _mm_ksplit_kernel(a_ref, w_ref, o_ref):
    @pl.when(pl.program_id(2) == 0)
    def _():
        o_ref[...] = jnp.zeros_like(o_ref)

    o_ref[...] += jnp.dot(a_ref[...], w_ref[...], preferred_element_type=F32)


def _mm_ksplit(a, w, *, name):
    t, k = a.shape
    n = w.shape[1]
    tm = _tile(t, 1024)
    tn = _tile(n, 1024)
    tk = _tile(k, 4096)
    return pl.pallas_call(
        _mm_ksplit_kernel,
        out_shape=jax.ShapeDtypeStruct((t, n), F32),
        grid=(t // tm, n // tn, k // tk),
        in_specs=[pl.BlockSpec((tm, tk), lambda i, j, kk: (i, kk)),
                  pl.BlockSpec((tk, tn), lambda i, j, kk: (kk, j))],
        out_specs=pl.BlockSpec((tm, tn), lambda i, j, kk: (i, j)),
        compiler_params=_cparams(("arbitrary", "arbitrary", "arbitrary")),
        name=name,
    )(a, w)


def _rope_tables(seq):
    inv_freq = 1.0 / (ROPE_THETA ** (jnp.arange(0, HEAD_DIM, 2, dtype=F32) / HEAD_DIM))
    ang = jnp.arange(seq, dtype=F32)[:, None] * inv_freq[None, :]
    cos, sin = jnp.cos(ang), jnp.sin(ang)
    return jnp.concatenate([cos, cos], axis=-1), jnp.concatenate([-sin, sin], axis=-1)


def kernel(x, c, w_ada, b_ada, g_pre_mix, g_post_mix, g_pre_mlp, g_post_mlp, w_in, attn_sinks,
           w_branch_gate, b_branch_gate, w_proj_swa, w_proj_sb, w_out, w_ff_in, w_ff_out):
    batch, seq, d = x.shape
    depth = w_in.shape[0]
    d_in = w_in.shape[2]
    swa_q_heads = attn_sinks.shape[1]
    swa_q_w = w_proj_swa.shape[1]
    sb_w = w_proj_sb.shape[1]
    swa_kv_w = (d_in - swa_q_w - 3 * sb_w) // 2
    sb_heads = sb_w // HEAD_DIM
    assert swa_q_w == swa_q_heads * HEAD_DIM and swa_kv_w * SWA_GROUP == swa_q_w
    assert seq % SWA_WINDOW == 0
    t = batch * seq

    cos_full, sin_signed = _rope_tables(seq)
    pad_rows = (-batch) % 8
    c_pad = jnp.pad(c, ((0, pad_rows), (0, 0)))

    swa_w = swa_q_w + 2 * swa_kv_w
    n_rope = swa_q_w + swa_kv_w

    for l in range(depth):
        mod = _ada_mod(c_pad, w_ada[l], b_ada[l])[:batch]
        mod3 = mod.reshape(batch * N_MOD, 1, d)

        h = _prenorm(x, g_pre_mix[l], mod3, 0, 1).reshape(t, d)
        proj_swa, (w_sb_in_bf,) = _rope_proj(
            h, w_in[l][:, :swa_w].astype(BF16), cos_full, sin_signed, n_rope=n_rope, seq=seq,
            casts=((w_in[l], swa_w, d_in - swa_w),))
        proj_sb, (w_gate_bf,) = _mm(h, w_sb_in_bf, BF16, casts=(w_branch_gate[l],),
                                    tn_pref=768, name="sb_proj")
        gates, (w_swa_bf, w_sb_bf, w_out_bf, w_ff_in_bf) = _mm(
            h, w_gate_bf, BF16, bias=b_branch_gate[l], epilogue="sigmoid",
            casts=(w_proj_swa[l], w_proj_sb[l], w_out[l], w_ff_in[l]), name="gate_proj")
        a_swa = _swa(proj_swa, attn_sinks[l], batch=batch, seq=seq, q_heads=swa_q_heads,
                     k_col=swa_q_w, v_col=n_rope)
        a_sb = _sb(proj_sb, batch=batch, seq=seq, heads=sb_heads,
                   q_col=0, k_col=sb_w, v_col=2 * sb_w)
        merged = _merge(a_swa, a_sb, w_swa_bf, w_sb_bf, gates)
        mixed, _ = _mm(merged, w_out_bf, F32, name="out_proj")
        act, x1, (w_ff_out_bf,) = _mlp_in(
            mixed, x.reshape(t, d), g_post_mix[l], g_pre_mlp[l], mod3, w_ff_in_bf, seq=seq,
            k_gate=2, k_shift=3, k_scale=4, casts=(w_ff_out[l],))
        f = _mm_ksplit(act, w_ff_out_bf, name="ff_out").reshape(batch, seq, d)
        x = _post(f, x1.reshape(batch, seq, d), g_post_mlp[l], mod3, 5)
    return x
```

```python
import functools

import jax
import jax.numpy as jnp
from jax import lax
from jax.experimental import pallas as pl
from jax.experimental.pallas import tpu as pltpu

F32 = jnp.float32
BF16 = jnp.bfloat16

HEAD_DIM = 128
LANES = 128
BF16_SUBLANES = 16
SWA_WINDOW = 128
SWA_GROUP = 4
ROPE_THETA = 10000.0
NORM_EPS = 1e-6
LOG2E = 1.4426950408889634
SB_UNDERFLOW_LOG2 = -152.0
N_MOD = 6
MIB = 1024 * 1024
VMEM_LIMIT_BYTES = 56 * MIB


def _tile(dim, pref):
    t = min(dim, pref)
    while dim % t:
        t //= 2
    return t


def _cparams(semantics):
    return pltpu.CompilerParams(dimension_semantics=semantics, vmem_limit_bytes=VMEM_LIMIT_BYTES)


def _ada_kernel(c_ref, w_ref, b_ref, o_ref):
    c = c_ref[...]
    act = (c * jax.nn.sigmoid(c)).astype(BF16)
    o_ref[...] = jnp.dot(act, w_ref[...].astype(BF16), preferred_element_type=F32) + b_ref[...]


def _ada_mod(c_pad, w_ada, b_ada):
    rows, d = c_pad.shape
    n = w_ada.shape[1]
    tn = _tile(n, 1024)
    return pl.pallas_call(
        _ada_kernel,
        out_shape=jax.ShapeDtypeStruct((rows, n), F32),
        grid=(n // tn,),
        in_specs=[
            pl.BlockSpec((rows, d), lambda j: (0, 0)),
            pl.BlockSpec((d, tn), lambda j: (0, j)),
            pl.BlockSpec((1, tn), lambda j: (0, j)),
        ],
        out_specs=pl.BlockSpec((rows, tn), lambda j: (0, j)),
        compiler_params=_cparams(("arbitrary",)),
        name="ada_mod",
    )(c_pad, w_ada, b_ada.reshape(1, n))


def _rms(x, g):
    y = x * lax.rsqrt(jnp.mean(x * x, axis=-1, keepdims=True) + NORM_EPS)
    return y * g


def _prenorm_kernel(x_ref, g_ref, scale_ref, shift_ref, h_ref):
    h = _rms(x_ref[0], g_ref[...]) * (1.0 + scale_ref[0]) + shift_ref[0]
    h_ref[0] = h.astype(h_ref.dtype)


def _mod_spec(d, k):
    return pl.BlockSpec((1, 1, d), lambda b, i: (b * N_MOD + k, 0, 0))


def _prenorm(x, g, mod3, k_shift, k_scale):
    b, s, d = x.shape
    ts = _tile(s, 512)
    row = pl.BlockSpec((1, ts, d), lambda bb, i: (bb, i, 0))
    return pl.pallas_call(
        _prenorm_kernel,
        out_shape=jax.ShapeDtypeStruct((b, s, d), BF16),
        grid=(b, s // ts),
        in_specs=[row, pl.BlockSpec((1, d), lambda bb, i: (0, 0)),
                  _mod_spec(d, k_scale), _mod_spec(d, k_shift)],
        out_specs=row,
        compiler_params=_cparams(("arbitrary", "arbitrary")),
        name="prenorm",
    )(x, g.reshape(1, d), mod3, mod3)


def _post_pre_math(y, x, gpost, gate, gpre, scale, shift):
    x1 = x + _rms(y, gate * gpost)
    h = _rms(x1, gpre * (1.0 + scale)) + shift
    return x1, h


def _first_h2_kernel(y_ref, x_ref, gpost_ref, gate_ref, gpre_ref, scale_ref, shift_ref, h_ref):
    _, h = _post_pre_math(y_ref[...], x_ref[...], gpost_ref[...], gate_ref[0], gpre_ref[...],
                          scale_ref[0], shift_ref[0])
    h_ref[...] = h.astype(h_ref.dtype)


def _first_h2(y, x, gpost, gpre, mod3, rows, k_gate, k_shift, k_scale):
    d = x.shape[1]
    ts = _tile(rows, 256)
    row = pl.BlockSpec((ts, d), lambda i: (i, 0))
    vec = pl.BlockSpec((1, d), lambda i: (0, 0))

    def mod(k):
        return pl.BlockSpec((1, 1, d), lambda i: (k, 0, 0))

    return pl.pallas_call(
        _first_h2_kernel,
        out_shape=jax.ShapeDtypeStruct((rows, d), BF16),
        grid=(rows // ts,),
        in_specs=[row, row, vec, mod(k_gate), vec, mod(k_scale), mod(k_shift)],
        out_specs=row,
        compiler_params=_cparams(("arbitrary",)),
        name="first_h2",
    )(y, x, gpost.reshape(1, d), mod3, gpre.reshape(1, d), mod3, mod3)


def _mlp_in_kernel(*refs, chunk, n_casts):
    (y_ref, x_ref, gpost_ref, gate_ref, gpre_ref, scale_ref, shift_ref, h_first_hbm,
     w_ref) = refs[:9]
    cast_in = refs[9:9 + n_casts]
    act_ref, x1_ref = refs[9 + n_casts:11 + n_casts]
    cast_out = refs[11 + n_casts:11 + 2 * n_casts]
    h_even, h_odd, sem = refs[11 + 2 * n_casts:]
    i = pl.program_id(0)
    j = pl.program_id(1)

    @pl.when(jnp.logical_and(i == 0, j == 0))
    def _():
        first = pltpu.make_async_copy(h_first_hbm, h_even, sem)
        first.start()
        first.wait()

    for src, dst in zip(cast_in, cast_out):
        dst[...] = src[...].astype(dst.dtype)

    def step(h_cur, h_next):
        x1, h = _post_pre_math(y_ref[...], x_ref[...], gpost_ref[...], gate_ref[0], gpre_ref[...],
                               scale_ref[0], shift_ref[0])
        x1_ref[...] = x1
        h_next[pl.ds(pl.multiple_of(j * chunk, chunk), chunk), :] = h.astype(h_next.dtype)
        acc = jnp.dot(h_cur[...], w_ref[...], preferred_element_type=F32)
        act_ref[...] = jnp.square(jnp.maximum(acc, 0.0)).astype(act_ref.dtype)

    @pl.when(i % 2 == 0)
    def _():
        step(h_even, h_odd)

    @pl.when(i % 2 == 1)
    def _():
        step(h_odd, h_even)


def _mlp_in(y, x, gpost, gpre, mod3, w, *, seq, k_gate, k_shift, k_scale, casts=()):
    t, d = x.shape
    n = w.shape[1]
    tm = _tile(seq, 1024)
    tn = _tile(n, 1024)
    ni, nj = t // tm, n // tn
    chunk = tm // nj
    assert tm % nj == 0 and chunk % BF16_SUBLANES == 0
    blocks_per_batch = seq // tm

    def nxt(i):
        return (i + 1) % ni

    rows = pl.BlockSpec((chunk, d), lambda i, j: (nxt(i) * nj + j, 0))
    vec = pl.BlockSpec((1, d), lambda i, j: (0, 0))

    def mod(k):
        return pl.BlockSpec((1, 1, d),
                            lambda i, j: ((nxt(i) // blocks_per_batch) * N_MOD + k, 0, 0))

    h_first = _first_h2(y, x, gpost, gpre, mod3, tm, k_gate, k_shift, k_scale)
    cast_in_specs, cast_out_specs, cast_args, cast_shapes = _cast_plan(casts, ni, nj)
    outs = pl.pallas_call(
        functools.partial(_mlp_in_kernel, chunk=chunk, n_casts=len(casts)),
        out_shape=[jax.ShapeDtypeStruct((t, n), BF16), jax.ShapeDtypeStruct((t, d), F32)]
        + cast_shapes,
        grid=(ni, nj),
        in_specs=[rows, rows, vec, mod(k_gate), vec, mod(k_scale), mod(k_shift),
                  pl.BlockSpec(memory_space=pl.ANY),
                  pl.BlockSpec((d, tn), lambda i, j: (0, j))] + cast_in_specs,
        out_specs=[pl.BlockSpec((tm, tn), lambda i, j: (i, j)), rows] + cast_out_specs,
        scratch_shapes=[pltpu.VMEM((tm, d), BF16), pltpu.VMEM((tm, d), BF16),
                        pltpu.SemaphoreType.DMA(())],
        compiler_params=_cparams(("arbitrary", "arbitrary")),
        name="ff_in",
    )(y, x, gpost.reshape(1, d), mod3, gpre.reshape(1, d), mod3, mod3, h_first, w, *cast_args)
    return outs[0], outs[1], [o.reshape(-1, o.shape[-1]) for o in outs[2:]]


def _post_kernel(y_ref, x_ref, gpost_ref, gate_ref, o_ref):
    o_ref[0] = x_ref[0] + gate_ref[0] * _rms(y_ref[0], gpost_ref[...])


def _post(y, x, gpost, mod3, k_gate):
    b, s, d = x.shape
    ts = _tile(s, 512)
    row = pl.BlockSpec((1, ts, d), lambda bb, i: (bb, i, 0))
    return pl.pallas_call(
        _post_kernel,
        out_shape=jax.ShapeDtypeStruct((b, s, d), F32),
        grid=(b, s // ts),
        in_specs=[row, row, pl.BlockSpec((1, d), lambda bb, i: (0, 0)), _mod_spec(d, k_gate)],
        out_specs=row,
        compiler_params=_cparams(("arbitrary", "arbitrary")),
        name="post_norm",
    )(y, x, gpost.reshape(1, d), mod3)


def _rope_proj_kernel(*refs, full_rope_tiles, partial_chunks, n_casts):
    a_ref, w_ref, cos_ref, sin_ref = refs[:4]
    cast_in = refs[4:4 + n_casts]
    o_ref = refs[4 + n_casts]
    cast_out = refs[5 + n_casts:]
    for src, dst in zip(cast_in, cast_out):
        dst[...] = src[...].astype(dst.dtype)
    j = pl.program_id(1)
    acc = jnp.dot(a_ref[...], w_ref[...], preferred_element_type=F32)
    cos, sin = cos_ref[...], sin_ref[...]
    head = j <= full_rope_tiles
    tail = j < full_rope_tiles
    tables = {True: (jnp.where(head, cos, 1.0), jnp.where(head, sin, 0.0)),
              False: (jnp.where(tail, cos, 1.0), jnp.where(tail, sin, 0.0))}
    for c in range(acc.shape[1] // LANES):
        cols = slice(c * LANES, (c + 1) * LANES)
        cos_c, sin_c = tables[c < partial_chunks]
        blk = acc[:, cols]
        o_ref[:, cols] = (blk * cos_c + pltpu.roll(blk, HEAD_DIM // 2, 1) * sin_c).astype(o_ref.dtype)


def _rope_proj(h, w, cos_full, sin_signed, *, n_rope, seq, casts=()):
    t, d = h.shape
    n = w.shape[1]
    tm = _tile(seq, 1024)
    tn = _tile(n, 1024)
    assert n_rope % LANES == 0 and tn % LANES == 0
    n_seq_tiles = seq // tm
    ni, nj = t // tm, n // tn
    cast_in_specs, cast_out_specs, cast_args, cast_shapes = _cast_plan(casts, ni, nj)
    kern = functools.partial(_rope_proj_kernel, full_rope_tiles=n_rope // tn,
                             partial_chunks=(n_rope % tn) // LANES, n_casts=len(casts))
    outs = pl.pallas_call(
        kern,
        out_shape=[jax.ShapeDtypeStruct((t, n), BF16)] + cast_shapes,
        grid=(ni, nj),
        in_specs=[
            pl.BlockSpec((tm, d), lambda i, j: (i, 0)),
            pl.BlockSpec((d, tn), lambda i, j: (0, j)),
            pl.BlockSpec((tm, HEAD_DIM), lambda i, j: (i % n_seq_tiles, 0)),
            pl.BlockSpec((tm, HEAD_DIM), lambda i, j: (i % n_seq_tiles, 0)),
        ] + cast_in_specs,
        out_specs=[pl.BlockSpec((tm, tn), lambda i, j: (i, j))] + cast_out_specs,
        compiler_params=_cparams(("arbitrary", "arbitrary")),
        name="swa_proj",
    )(h, w, cos_full, sin_signed, *cast_args)
    return outs[0], [o.reshape(-1, o.shape[-1]) for o in outs[1:]]


def _swa_kernel(sink_ref, q_ref, kp_ref, k_ref, vp_ref, v_ref, o_ref, p_ref, *, tq):
    i = pl.program_id(1)
    kvh = pl.program_id(2)
    w = SWA_WINDOW
    rows = SWA_GROUP * w
    qi = lax.broadcasted_iota(jnp.int32, (rows, 2 * w), 0) % w
    kj = lax.broadcasted_iota(jnp.int32, (rows, 2 * w), 1)
    band = jnp.logical_and(kj > qi, kj <= qi + w)
    first = jnp.logical_and(band, jnp.logical_or(kj >= w, i > 0))
    sink_fill = jnp.concatenate(
        [jnp.full((w, 2 * w), sink_ref[kvh * SWA_GROUP + g], F32) for g in range(SWA_GROUP)], axis=0)
    masked_fill = jnp.where(kj == 0, sink_fill, -jnp.inf)
    key_row = lax.broadcasted_iota(jnp.int32, (2 * w, HEAD_DIM), 0)
    ones = jnp.ones((2 * w, HEAD_DIM), BF16)
    scale = HEAD_DIM ** -0.5
    n_sub = tq // w
    def probs_stage(s):
        q = jnp.concatenate(
            [q_ref[s * w:(s + 1) * w, g * HEAD_DIM:(g + 1) * HEAD_DIM] for g in range(SWA_GROUP)],
            axis=0)
        if s == 0:
            k = jnp.concatenate([kp_ref[...], k_ref[0:w, :]], axis=0)
            valid = first
        else:
            k = k_ref[(s - 1) * w:(s + 1) * w, :]
            valid = band
        sc = lax.dot_general(q, k, (((1,), (1,)), ((), ())), preferred_element_type=F32) * scale
        sc = jnp.where(valid, sc, masked_fill)
        p_ref[s] = jnp.exp(sc - jnp.max(sc, axis=-1, keepdims=True)).astype(BF16)

    def values_stage(s):
        if s == 0:
            v = jnp.concatenate([vp_ref[...], v_ref[0:w, :]], axis=0)
        else:
            v = v_ref[(s - 1) * w:(s + 1) * w, :]
        v = jnp.where(key_row == 0, jnp.zeros_like(v), v)
        ov = jnp.dot(p_ref[s], jnp.concatenate([v, ones], axis=1), preferred_element_type=F32)
        o = ov[:, :HEAD_DIM] / ov[:, HEAD_DIM:]
        for g in range(SWA_GROUP):
            o_ref[s * w:(s + 1) * w, g * HEAD_DIM:(g + 1) * HEAD_DIM] = (
                o[g * w:(g + 1) * w, :].astype(o_ref.dtype))

    for s in range(n_sub):
        probs_stage(s)
        if s >= 1:
            values_stage(s - 1)
    values_stage(n_sub - 1)


def _swa(proj, sinks, *, batch, seq, q_heads, k_col, v_col):
    kv_heads = q_heads // SWA_GROUP
    w = SWA_WINDOW
    tq = _tile(seq, 4096)
    nq = seq // tq
    sub = tq // w
    gw = SWA_GROUP * HEAD_DIM
    kb, vb = k_col // HEAD_DIM, v_col // HEAD_DIM

    def prev_rows(b, i):
        return b * (seq // w) + jnp.maximum(i * sub - 1, 0)

    grid_spec = pltpu.PrefetchScalarGridSpec(
        num_scalar_prefetch=1,
        grid=(batch, nq, kv_heads),
        in_specs=[
            pl.BlockSpec((tq, gw), lambda b, i, h, s: (b * nq + i, h)),
            pl.BlockSpec((w, HEAD_DIM), lambda b, i, h, s: (prev_rows(b, i), kb + h)),
            pl.BlockSpec((tq, HEAD_DIM), lambda b, i, h, s: (b * nq + i, kb + h)),
            pl.BlockSpec((w, HEAD_DIM), lambda b, i, h, s: (prev_rows(b, i), vb + h)),
            pl.BlockSpec((tq, HEAD_DIM), lambda b, i, h, s: (b * nq + i, vb + h)),
        ],
        out_specs=pl.BlockSpec((tq, gw), lambda b, i, h, s: (b * nq + i, h)),
        scratch_shapes=[pltpu.VMEM((sub, SWA_GROUP * w, 2 * w), BF16)],
    )
    return pl.pallas_call(
        functools.partial(_swa_kernel, tq=tq),
        out_shape=jax.ShapeDtypeStruct((batch * seq, q_heads * HEAD_DIM), BF16),
        grid_spec=grid_spec,
        compiler_params=_cparams(("arbitrary", "arbitrary", "arbitrary")),
        name="swa_attention",
    )(sinks, proj, proj, proj, proj, proj)


def _sb_kernel(q_ref, kc_ref, kp_ref, vc_ref, vp_ref, kv_hbm, o_ref, hilo0_ref, hilo1_ref, lb0_ref,
               lb1_ref, kt0_ref, kt1_ref, acc_ref, r_ref, kbuf_ref, vbuf_ref, sem,
               *, tq, hp, seq, k_col, v_col):
    hilo_refs = (hilo0_ref, hilo1_ref)
    lb_refs = (lb0_ref, lb1_ref)
    kt_refs = (kt0_ref, kt1_ref)
    i = pl.program_id(2)
    row = lax.broadcasted_iota(jnp.int32, (tq, tq), 0)
    col = lax.broadcasted_iota(jnp.int32, (tq, tq), 1)
    strict_lower = col < row
    neg_later = jnp.where(row > col, -1.0, 0.0).astype(BF16)
    neg_later = jnp.concatenate([neg_later, neg_later], axis=0)
    z_scale = HEAD_DIM ** -0.5 * LOG2E
    sign_bit = jnp.uint32(0x80000000)

    def logits_stage(slot, k_ref, diag, heads=range(hp)):
        for h in heads:
            cols = slice(h * HEAD_DIM, (h + 1) * HEAD_DIM)
            q = q_ref[:, cols]
            kt_refs[slot][h] = k_ref[:, cols].T
            z = jnp.dot(q, kt_refs[slot][h], preferred_element_type=F32) * z_scale
            neg_abs = lax.bitcast_convert_type(
                lax.bitcast_convert_type(z, jnp.uint32) | sign_bit, F32)
            softplus = jnp.maximum(z, 0.0) + jnp.log(1.0 + jnp.exp2(neg_abs)) * LOG2E
            log_beta = z - softplus
            if diag:
                lb_refs[slot][h] = jnp.where(strict_lower, log_beta, -jnp.inf)
                softplus = jnp.where(strict_lower, softplus, 0.0)
                r_ref[h] = -jnp.sum(softplus, axis=-1, keepdims=True)
            else:
                r_sum = r_ref[h]
                lb_refs[slot][h] = log_beta + r_sum
                r_ref[h] = r_sum - jnp.sum(softplus, axis=-1, keepdims=True)
            hi = softplus.astype(BF16)
            lo = (softplus - hi.astype(F32)).astype(BF16)
            hilo_refs[slot][h] = jnp.concatenate([hi, lo], axis=1)

    def weights_stage(slot, v_ref, first, heads=range(hp)):
        for h in heads:
            cols = slice(h * HEAD_DIM, (h + 1) * HEAD_DIM)
            v = v_ref[:, cols]
            suffix = jnp.dot(hilo_refs[slot][h], neg_later, preferred_element_type=F32)
            att = jnp.exp2(lb_refs[slot][h] + suffix)
            pv = jnp.dot(att.astype(BF16), v, preferred_element_type=F32)
            if first:
                acc_ref[h] = pv
            else:
                acc_ref[h] += pv

    def any_row_alive():
        top = functools.reduce(
            jnp.maximum, [jnp.max(r_ref[h], axis=0, keepdims=True) for h in range(hp)])
        return top[0, 0] > SB_UNDERFLOW_LOG2

    def staggered(logits_stages, weights_stages):
        lag = 2
        for k, stage in enumerate(logits_stages):
            stage()
            if k >= lag:
                weights_stages[k - lag]()
        for stage in weights_stages[max(len(logits_stages) - lag, 0):]:
            stage()

    def per_head(stage, *args):
        return [functools.partial(stage, *args, heads=(h,)) for h in range(hp)]

    @pl.when(i == 0)
    def _():
        staggered(per_head(logits_stage, 0, kc_ref, True), per_head(weights_stage, 0, vc_ref, True))

    @pl.when(i > 0)
    def _():
        staggered(
            per_head(logits_stage, 0, kc_ref, True) + per_head(logits_stage, 1, kp_ref, False),
            per_head(weights_stage, 0, vc_ref, True) + per_head(weights_stage, 1, vp_ref, False))

        def cond(state):
            n, alive = state
            return jnp.logical_and(n <= i, alive)

        def body(state):
            n, _ = state
            row0 = pl.multiple_of(pl.program_id(0) * seq + (i - n) * tq, tq)
            col0 = pl.program_id(1) * (hp * HEAD_DIM)
            copies = [
                pltpu.make_async_copy(
                    kv_hbm.at[pl.ds(row0, tq), pl.ds(pl.multiple_of(c + col0, LANES), hp * HEAD_DIM)],
                    buf, sem.at[s])
                for s, (c, buf) in enumerate(((k_col, kbuf_ref), (v_col, vbuf_ref)))]
            for cp in copies:
                cp.start()
            for cp in copies:
                cp.wait()
            staggered(per_head(logits_stage, 1, kbuf_ref, False),
                      per_head(weights_stage, 1, vbuf_ref, False))
            return n + 1, any_row_alive()

        lax.while_loop(cond, body, (jnp.int32(2), any_row_alive()))

    for h in range(hp):
        o_ref[:, h * HEAD_DIM:(h + 1) * HEAD_DIM] = acc_ref[h].astype(o_ref.dtype)


def _sb(proj, *, batch, seq, heads, q_col, k_col, v_col):
    tq = _tile(seq, 256)
    nq = seq // tq
    hp = next(c for c in (16, 8, 4, 2, 1) if heads % c == 0)
    hw = hp * HEAD_DIM
    assert q_col % hw == 0 and k_col % hw == 0 and v_col % hw == 0
    qb, kb, vb = q_col // hw, k_col // hw, v_col // hw

    def cur(col):
        return pl.BlockSpec((tq, hw), lambda b, h, i: (b * nq + i, col + h))

    def prev(col):
        return pl.BlockSpec((tq, hw), lambda b, h, i: (b * nq + jnp.maximum(i - 1, 0), col + h))

    return pl.pallas_call(
        functools.partial(_sb_kernel, tq=tq, hp=hp, seq=seq, k_col=k_col, v_col=v_col),
        out_shape=jax.ShapeDtypeStruct((batch * seq, heads * HEAD_DIM), BF16),
        grid=(batch, heads // hp, nq),
        in_specs=[cur(qb), cur(kb), prev(kb), cur(vb), prev(vb),
                  pl.BlockSpec(memory_space=pl.ANY)],
        out_specs=pl.BlockSpec((tq, hw), lambda b, h, i: (b * nq + i, h)),
        scratch_shapes=[pltpu.VMEM((hp, tq, 2 * tq), BF16), pltpu.VMEM((hp, tq, 2 * tq), BF16),
                        pltpu.VMEM((hp, tq, tq), F32), pltpu.VMEM((hp, tq, tq), F32),
                        pltpu.VMEM((hp, HEAD_DIM, tq), BF16), pltpu.VMEM((hp, HEAD_DIM, tq), BF16),
                        pltpu.VMEM((hp, tq, HEAD_DIM), F32), pltpu.VMEM((hp, tq, 1), F32),
                        pltpu.VMEM((tq, hw), BF16), pltpu.VMEM((tq, hw), BF16),
                        pltpu.SemaphoreType.DMA((2,))],
        compiler_params=_cparams(("arbitrary", "arbitrary", "arbitrary")),
        name="sb_attention",
    )(proj, proj, proj, proj, proj, proj)


def _merge_kernel(a1_ref, a2_ref, w1_ref, w2_ref, ga_ref, gb_ref, o_ref):
    y1 = jnp.dot(a1_ref[...], w1_ref[...], preferred_element_type=F32)
    y2 = jnp.dot(a2_ref[...], w2_ref[...], preferred_element_type=F32)
    o = ga_ref[...].astype(F32) * y1 + gb_ref[...].astype(F32) * y2
    o_ref[...] = o.astype(o_ref.dtype)


def _merge(a_swa, a_sb, w_swa, w_sb, gates):
    t, k1 = a_swa.shape
    k2 = a_sb.shape[1]
    d = w_swa.shape[1]
    tm = _tile(t, 1024)
    tn = _tile(d, 1024)
    ga, gb = 0, d // tn
    return pl.pallas_call(
        _merge_kernel,
        out_shape=jax.ShapeDtypeStruct((t, d), BF16),
        grid=(t // tm, d // tn),
        in_specs=[
            pl.BlockSpec((tm, k1), lambda i, j: (i, 0)),
            pl.BlockSpec((tm, k2), lambda i, j: (i, 0)),
            pl.BlockSpec((k1, tn), lambda i, j: (0, j)),
            pl.BlockSpec((k2, tn), lambda i, j: (0, j)),
            pl.BlockSpec((tm, tn), lambda i, j: (i, ga + j)),
            pl.BlockSpec((tm, tn), lambda i, j: (i, gb + j)),
        ],
        out_specs=pl.BlockSpec((tm, tn), lambda i, j: (i, j)),
        compiler_params=_cparams(("arbitrary", "arbitrary")),
        name="branch_merge",
    )(a_swa, a_sb, w_swa, w_sb, gates, gates)


def _mm_kernel(*refs, epilogue, has_bias, n_casts):
    a_ref, w_ref = refs[:2]
    b_ref = refs[2] if has_bias else None
    first_cast = 2 + has_bias
    cast_in = refs[first_cast:first_cast + n_casts]
    o_ref = refs[first_cast + n_casts]
    cast_out = refs[first_cast + n_casts + 1:]
    for src, dst in zip(cast_in, cast_out):
        dst[...] = src[...].astype(dst.dtype)
    acc = jnp.dot(a_ref[...], w_ref[...], preferred_element_type=F32)
    if has_bias:
        acc = acc + b_ref[...]
    if epilogue == "relu_sq":
        acc = jnp.square(jnp.maximum(acc, 0.0))
    elif epilogue == "sigmoid":
        acc = 0.5 * jnp.tanh(0.5 * acc) + 0.5
    o_ref[...] = acc.astype(o_ref.dtype)


def _cast_plan(casts, ni, nj):
    steps = ni * nj
    in_specs, out_specs, args, shapes = [], [], [], []
    for c in casts:
        c, col0, cols = c if isinstance(c, tuple) else (c, 0, c.shape[1])
        rows = c.shape[0]
        if col0 == 0 and cols == c.shape[1] and rows % (steps * BF16_SUBLANES) == 0:
            slab = rows // steps
            spec = pl.BlockSpec((1, slab, cols), lambda i, j: (i * nj + j, 0, 0))
            in_specs.append(spec)
            out_specs.append(spec)
            args.append(c.reshape(steps, slab, cols))
            shapes.append(jax.ShapeDtypeStruct((steps, slab, cols), BF16))
        else:
            slab = cols // steps
            assert cols % (steps * LANES) == 0 and col0 % slab == 0, (c.shape, col0, cols, steps)
            first = col0 // slab
            in_specs.append(pl.BlockSpec((rows, slab), lambda i, j: (0, first + i * nj + j)))
            out_specs.append(pl.BlockSpec((rows, slab), lambda i, j: (0, i * nj + j)))
            args.append(c)
            shapes.append(jax.ShapeDtypeStruct((rows, cols), BF16))
    return in_specs, out_specs, args, shapes


def _mm(a, w, out_dtype, *, bias=None, epilogue=None, casts=(), tn_pref=1024, name):
    t, k = a.shape
    n = w.shape[1]
    tm = _tile(t, 1024)
    tn = tn_pref if n % tn_pref == 0 else _tile(n, 1024)
    ni, nj = t // tm, n // tn
    in_specs = [pl.BlockSpec((tm, k), lambda i, j: (i, 0)),
                pl.BlockSpec((k, tn), lambda i, j: (0, j))]
    args = [a, w]
    if bias is not None:
        in_specs.append(pl.BlockSpec((1, tn), lambda i, j: (0, j)))
        args.append(bias.reshape(1, n))
    cast_in_specs, cast_out_specs, cast_args, cast_shapes = _cast_plan(casts, ni, nj)
    in_specs += cast_in_specs
    args += cast_args
    out_specs = [pl.BlockSpec((tm, tn), lambda i, j: (i, j))] + cast_out_specs
    out_shape = [jax.ShapeDtypeStruct((t, n), out_dtype)] + cast_shapes
    outs = pl.pallas_call(
        functools.partial(_mm_kernel, epilogue=epilogue, has_bias=bias is not None,
                          n_casts=len(casts)),
        out_shape=out_shape,
        grid=(ni, nj),
        in_specs=in_specs,
        out_specs=out_specs,
        compiler_params=_cparams(("arbitrary", "arbitrary")),
        name=name,
    )(*args)
    return outs[0], [o.reshape(-1, o.shape[-1]) for o in outs[1:]]


def _mm_ksplit_kernel(a_ref, w_ref, o_ref):
    @pl.when(pl.program_id(2) == 0)
    def _():
        o_ref[...] = jnp.zeros_like(o_ref)

    o_ref[...] += jnp.dot(a_ref[...], w_ref[...], preferred_element_type=F32)


def _mm_ksplit(a, w, *, name):
    t, k = a.shape
    n = w.shape[1]
    tm = _tile(t, 1024)
    tn = _tile(n, 1024)
    tk = _tile(k, 4096)
    return pl.pallas_call(
        _mm_ksplit_kernel,
        out_shape=jax.ShapeDtypeStruct((t, n), F32),
        grid=(t // tm, n // tn, k // tk),
        in_specs=[pl.BlockSpec((tm, tk), lambda i, j, kk: (i, kk)),
                  pl.BlockSpec((tk, tn), lambda i, j, kk: (kk, j))],
        out_specs=pl.BlockSpec((tm, tn), lambda i, j, kk: (i, j)),
        compiler_params=_cparams(("arbitrary", "arbitrary", "arbitrary")),
        name=name,
    )(a, w)


def _rope_tables(seq):
    inv_freq = 1.0 / (ROPE_THETA ** (jnp.arange(0, HEAD_DIM, 2, dtype=F32) / HEAD_DIM))
    ang = jnp.arange(seq, dtype=F32)[:, None] * inv_freq[None, :]
    cos, sin = jnp.cos(ang), jnp.sin(ang)
    return jnp.concatenate([cos, cos], axis=-1), jnp.concatenate([-sin, sin], axis=-1)


def kernel(x, c, w_ada, b_ada, g_pre_mix, g_post_mix, g_pre_mlp, g_post_mlp, w_in, attn_sinks,
           w_branch_gate, b_branch_gate, w_proj_swa, w_proj_sb, w_out, w_ff_in, w_ff_out):
    batch, seq, d = x.shape
    depth = w_in.shape[0]
    d_in = w_in.shape[2]
    swa_q_heads = attn_sinks.shape[1]
    swa_q_w = w_proj_swa.shape[1]
    sb_w = w_proj_sb.shape[1]
    swa_kv_w = (d_in - swa_q_w - 3 * sb_w) // 2
    sb_heads = sb_w // HEAD_DIM
    assert swa_q_w == swa_q_heads * HEAD_DIM and swa_kv_w * SWA_GROUP == swa_q_w
    assert seq % SWA_WINDOW == 0
    t = batch * seq

    cos_full, sin_signed = _rope_tables(seq)
    pad_rows = (-batch) % 8
    c_pad = jnp.pad(c, ((0, pad_rows), (0, 0)))

    swa_w = swa_q_w + 2 * swa_kv_w
    n_rope = swa_q_w + swa_kv_w

    for l in range(depth):
        mod = _ada_mod(c_pad, w_ada[l], b_ada[l])[:batch]
        mod3 = mod.reshape(batch * N_MOD, 1, d)

        h = _prenorm(x, g_pre_mix[l], mod3, 0, 1).reshape(t, d)
        proj_swa, (w_sb_in_bf,) = _rope_proj(
            h, w_in[l][:, :swa_w].astype(BF16), cos_full, sin_signed, n_rope=n_rope, seq=seq,
            casts=((w_in[l], swa_w, d_in - swa_w),))
        proj_sb, (w_gate_bf,) = _mm(h, w_sb_in_bf, BF16, casts=(w_branch_gate[l],),
                                    tn_pref=768, name="sb_proj")
        gates, (w_swa_bf, w_sb_bf, w_out_bf, w_ff_in_bf) = _mm(
            h, w_gate_bf, BF16, bias=b_branch_gate[l], epilogue="sigmoid",
            casts=(w_proj_swa[l], w_proj_sb[l], w_out[l], w_ff_in[l]), name="gate_proj")
        a_swa = _swa(proj_swa, attn_sinks[l], batch=batch, seq=seq, q_heads=swa_q_heads,
                     k_col=swa_q_w, v_col=n_rope)
        a_sb = _sb(proj_sb, batch=batch, seq=seq, heads=sb_heads,
                   q_col=0, k_col=sb_w, v_col=2 * sb_w)
        merged = _merge(a_swa, a_sb, w_swa_bf, w_sb_bf, gates)
        mixed, _ = _mm(merged, w_out_bf, F32, name="out_proj")
        act, x1, (w_ff_out_bf,) = _mlp_in(
            mixed, x.reshape(t, d), g_post_mix[l], g_pre_mlp[l], mod3, w_ff_in_bf, seq=seq,
            k_gate=2, k_shift=3, k_scale=4, casts=(w_ff_out[l],))
        f = _mm_ksplit(act, w_ff_out_bf, name="ff_out").reshape(batch, seq, d)
        x = _post(f, x1.reshape(batch, seq, d), g_post_mlp[l], mod3, 5)
    return x
```

```python
import functools

import jax
import jax.numpy as jnp
from jax import lax
from jax.experimental import pallas as pl
from jax.experimental.pallas import tpu as pltpu

F32 = jnp.float32
BF16 = jnp.bfloat16

HEAD_DIM = 128
LANES = 128
BF16_SUBLANES = 16
SWA_WINDOW = 128
SWA_GROUP = 4
ROPE_THETA = 10000.0
NORM_EPS = 1e-6
LOG2E = 1.4426950408889634
SB_UNDERFLOW_LOG2 = -152.0
N_MOD = 6
MIB = 1024 * 1024
VMEM_LIMIT_BYTES = 56 * MIB

MM_TILE = 1024
MM_K_TILE = 4096
SB_PROJ_TN = 768
NORM_ROWS = 512
SWA_STEP_ROWS = 4096
SB_BLOCK = 256


def _tile(dim, pref):
    t = min(dim, pref)
    while dim % t:
        t //= 2
    return t


def _cparams(semantics):
    return pltpu.CompilerParams(dimension_semantics=semantics, vmem_limit_bytes=VMEM_LIMIT_BYTES)


def _ada_kernel(c_ref, w_ref, b_ref, o_ref):
    c = c_ref[...]
    act = (c * jax.nn.sigmoid(c)).astype(BF16)
    o_ref[...] = jnp.dot(act, w_ref[...].astype(BF16), preferred_element_type=F32) + b_ref[...]


def _ada_mod(c_pad, w_ada, b_ada):
    rows, d = c_pad.shape
    n = w_ada.shape[1]
    tn = _tile(n, MM_TILE)
    return pl.pallas_call(
        _ada_kernel,
        out_shape=jax.ShapeDtypeStruct((rows, n), F32),
        grid=(n // tn,),
        in_specs=[
            pl.BlockSpec((rows, d), lambda j: (0, 0)),
            pl.BlockSpec((d, tn), lambda j: (0, j)),
            pl.BlockSpec((1, tn), lambda j: (0, j)),
        ],
        out_specs=pl.BlockSpec((rows, tn), lambda j: (0, j)),
        compiler_params=_cparams(("arbitrary",)),
        name="ada_mod",
    )(c_pad, w_ada, b_ada.reshape(1, n))


def _rms(x, g):
    y = x * lax.rsqrt(jnp.mean(x * x, axis=-1, keepdims=True) + NORM_EPS)
    return y * g


def _prenorm_kernel(x_ref, g_ref, scale_ref, shift_ref, h_ref):
    h = _rms(x_ref[0], g_ref[...]) * (1.0 + scale_ref[0]) + shift_ref[0]
    h_ref[0] = h.astype(h_ref.dtype)


def _mod_spec(d, k):
    return pl.BlockSpec((1, 1, d), lambda b, i: (b * N_MOD + k, 0, 0))


def _prenorm(x, g, mod3, k_shift, k_scale):
    b, s, d = x.shape
    ts = _tile(s, NORM_ROWS)
    row = pl.BlockSpec((1, ts, d), lambda bb, i: (bb, i, 0))
    return pl.pallas_call(
        _prenorm_kernel,
        out_shape=jax.ShapeDtypeStruct((b, s, d), BF16),
        grid=(b, s // ts),
        in_specs=[row, pl.BlockSpec((1, d), lambda bb, i: (0, 0)),
                  _mod_spec(d, k_scale), _mod_spec(d, k_shift)],
        out_specs=row,
        compiler_params=_cparams(("arbitrary", "arbitrary")),
        name="prenorm",
    )(x, g.reshape(1, d), mod3, mod3)


def _post_pre_math(y, x, gpost, gate, gpre, scale, shift):
    x1 = x + _rms(y, gate * gpost)
    h = _rms(x1, gpre * (1.0 + scale)) + shift
    return x1, h


def _first_h2_kernel(y_ref, x_ref, gpost_ref, gate_ref, gpre_ref, scale_ref, shift_ref, h_ref):
    _, h = _post_pre_math(y_ref[...], x_ref[...], gpost_ref[...], gate_ref[0], gpre_ref[...],
                          scale_ref[0], shift_ref[0])
    h_ref[...] = h.astype(h_ref.dtype)


def _first_h2(y, x, gpost, gpre, mod3, rows, k_gate, k_shift, k_scale):
    d = x.shape[1]
    ts = _tile(rows, NORM_ROWS)
    row = pl.BlockSpec((ts, d), lambda i: (i, 0))
    vec = pl.BlockSpec((1, d), lambda i: (0, 0))

    def mod(k):
        return pl.BlockSpec((1, 1, d), lambda i: (k, 0, 0))

    return pl.pallas_call(
        _first_h2_kernel,
        out_shape=jax.ShapeDtypeStruct((rows, d), BF16),
        grid=(rows // ts,),
        in_specs=[row, row, vec, mod(k_gate), vec, mod(k_scale), mod(k_shift)],
        out_specs=row,
        compiler_params=_cparams(("arbitrary",)),
        name="first_h2",
    )(y, x, gpost.reshape(1, d), mod3, gpre.reshape(1, d), mod3, mod3)


def _mlp_in_kernel(*refs, chunk, n_casts):
    (y_ref, x_ref, gpost_ref, gate_ref, gpre_ref, scale_ref, shift_ref, h_first_hbm,
     w_ref) = refs[:9]
    cast_in = refs[9:9 + n_casts]
    act_ref, x1_ref = refs[9 + n_casts:11 + n_casts]
    cast_out = refs[11 + n_casts:11 + 2 * n_casts]
    h_even, h_odd, sem = refs[11 + 2 * n_casts:]
    i = pl.program_id(0)
    j = pl.program_id(1)

    @pl.when(jnp.logical_and(i == 0, j == 0))
    def _():
        first = pltpu.make_async_copy(h_first_hbm, h_even, sem)
        first.start()
        first.wait()

    for src, dst in zip(cast_in, cast_out):
        dst[...] = src[...].astype(dst.dtype)

    def step(h_cur, h_next):
        x1, h = _post_pre_math(y_ref[...], x_ref[...], gpost_ref[...], gate_ref[0], gpre_ref[...],
                               scale_ref[0], shift_ref[0])
        x1_ref[...] = x1
        h_next[pl.ds(pl.multiple_of(j * chunk, chunk), chunk), :] = h.astype(h_next.dtype)
        acc = jnp.dot(h_cur[...], w_ref[...], preferred_element_type=F32)
        act_ref[...] = jnp.square(jnp.maximum(acc, 0.0)).astype(act_ref.dtype)

    @pl.when(i % 2 == 0)
    def _():
        step(h_even, h_odd)

    @pl.when(i % 2 == 1)
    def _():
        step(h_odd, h_even)


def _mlp_in(y, x, gpost, gpre, mod3, w, *, seq, k_gate, k_shift, k_scale, casts=()):
    t, d = x.shape
    n = w.shape[1]
    tm = _tile(seq, MM_TILE)
    tn = _tile(n, MM_TILE)
    ni, nj = t // tm, n // tn
    chunk = tm // nj
    assert tm % nj == 0 and chunk % BF16_SUBLANES == 0
    blocks_per_batch = seq // tm

    def nxt(i):
        return (i + 1) % ni

    rows = pl.BlockSpec((chunk, d), lambda i, j: (nxt(i) * nj + j, 0))
    vec = pl.BlockSpec((1, d), lambda i, j: (0, 0))

    def mod(k):
        return pl.BlockSpec((1, 1, d),
                            lambda i, j: ((nxt(i) // blocks_per_batch) * N_MOD + k, 0, 0))

    h_first = _first_h2(y, x, gpost, gpre, mod3, tm, k_gate, k_shift, k_scale)
    cast_in_specs, cast_out_specs, cast_args, cast_shapes = _cast_plan(casts, ni, nj)
    outs = pl.pallas_call(
        functools.partial(_mlp_in_kernel, chunk=chunk, n_casts=len(casts)),
        out_shape=[jax.ShapeDtypeStruct((t, n), BF16), jax.ShapeDtypeStruct((t, d), F32)]
        + cast_shapes,
        grid=(ni, nj),
        in_specs=[rows, rows, vec, mod(k_gate), vec, mod(k_scale), mod(k_shift),
                  pl.BlockSpec(memory_space=pl.ANY),
                  pl.BlockSpec((d, tn), lambda i, j: (0, j))] + cast_in_specs,
        out_specs=[pl.BlockSpec((tm, tn), lambda i, j: (i, j)), rows] + cast_out_specs,
        scratch_shapes=[pltpu.VMEM((tm, d), BF16), pltpu.VMEM((tm, d), BF16),
                        pltpu.SemaphoreType.DMA(())],
        compiler_params=_cparams(("arbitrary", "arbitrary")),
        name="ff_in",
    )(y, x, gpost.reshape(1, d), mod3, gpre.reshape(1, d), mod3, mod3, h_first, w, *cast_args)
    return outs[0], outs[1], [o.reshape(-1, o.shape[-1]) for o in outs[2:]]


def _post_kernel(y_ref, x_ref, gpost_ref, gate_ref, o_ref):
    o_ref[0] = x_ref[0] + gate_ref[0] * _rms(y_ref[0], gpost_ref[...])


def _post(y, x, gpost, mod3, k_gate):
    b, s, d = x.shape
    ts = _tile(s, NORM_ROWS)
    row = pl.BlockSpec((1, ts, d), lambda bb, i: (bb, i, 0))
    return pl.pallas_call(
        _post_kernel,
        out_shape=jax.ShapeDtypeStruct((b, s, d), F32),
        grid=(b, s // ts),
        in_specs=[row, row, pl.BlockSpec((1, d), lambda bb, i: (0, 0)), _mod_spec(d, k_gate)],
        out_specs=row,
        compiler_params=_cparams(("arbitrary", "arbitrary")),
        name="post_norm",
    )(y, x, gpost.reshape(1, d), mod3)


def _rope_proj_kernel(*refs, full_rope_tiles, partial_chunks, n_casts):
    a_ref, w_ref, cos_ref, sin_ref = refs[:4]
    cast_in = refs[4:4 + n_casts]
    o_ref = refs[4 + n_casts]
    cast_out = refs[5 + n_casts:]
    for src, dst in zip(cast_in, cast_out):
        dst[...] = src[...].astype(dst.dtype)
    j = pl.program_id(1)
    acc = jnp.dot(a_ref[...], w_ref[...], preferred_element_type=F32)
    cos, sin = cos_ref[...], sin_ref[...]
    head = j <= full_rope_tiles
    tail = j < full_rope_tiles
    tables = {True: (jnp.where(head, cos, 1.0), jnp.where(head, sin, 0.0)),
              False: (jnp.where(tail, cos, 1.0), jnp.where(tail, sin, 0.0))}
    for c in range(acc.shape[1] // LANES):
        cols = slice(c * LANES, (c + 1) * LANES)
        cos_c, sin_c = tables[c < partial_chunks]
        blk = acc[:, cols]
        o_ref[:, cols] = (blk * cos_c + pltpu.roll(blk, HEAD_DIM // 2, 1) * sin_c).astype(o_ref.dtype)


def _rope_proj(h, w, cos_full, sin_signed, *, n_rope, seq, casts=()):
    t, d = h.shape
    n = w.shape[1]
    tm = _tile(seq, MM_TILE)
    tn = _tile(n, MM_TILE)
    assert n_rope % LANES == 0 and tn % LANES == 0
    n_seq_tiles = seq // tm
    ni, nj = t // tm, n // tn
    cast_in_specs, cast_out_specs, cast_args, cast_shapes = _cast_plan(casts, ni, nj)
    kern = functools.partial(_rope_proj_kernel, full_rope_tiles=n_rope // tn,
                             partial_chunks=(n_rope % tn) // LANES, n_casts=len(casts))
    outs = pl.pallas_call(
        kern,
        out_shape=[jax.ShapeDtypeStruct((t, n), BF16)] + cast_shapes,
        grid=(ni, nj),
        in_specs=[
            pl.BlockSpec((tm, d), lambda i, j: (i, 0)),
            pl.BlockSpec((d, tn), lambda i, j: (0, j)),
            pl.BlockSpec((tm, HEAD_DIM), lambda i, j: (i % n_seq_tiles, 0)),
            pl.BlockSpec((tm, HEAD_DIM), lambda i, j: (i % n_seq_tiles, 0)),
        ] + cast_in_specs,
        out_specs=[pl.BlockSpec((tm, tn), lambda i, j: (i, j))] + cast_out_specs,
        compiler_params=_cparams(("arbitrary", "arbitrary")),
        name="swa_proj",
    )(h, w, cos_full, sin_signed, *cast_args)
    return outs[0], [o.reshape(-1, o.shape[-1]) for o in outs[1:]]


def _swa_kernel(sink_ref, q_ref, kp_ref, k_ref, vp_ref, v_ref, o_ref, p_ref, *, tq):
    i = pl.program_id(1)
    kvh = pl.program_id(2)
    w = SWA_WINDOW
    rows = SWA_GROUP * w
    qi = lax.broadcasted_iota(jnp.int32, (rows, 2 * w), 0) % w
    kj = lax.broadcasted_iota(jnp.int32, (rows, 2 * w), 1)
    band = jnp.logical_and(kj > qi, kj <= qi + w)
    first = jnp.logical_and(band, jnp.logical_or(kj >= w, i > 0))
    sink_fill = jnp.concatenate(
        [jnp.full((w, 2 * w), sink_ref[kvh * SWA_GROUP + g], F32) for g in range(SWA_GROUP)], axis=0)
    masked_fill = jnp.where(kj == 0, sink_fill, -jnp.inf)
    key_row = lax.broadcasted_iota(jnp.int32, (2 * w, HEAD_DIM), 0)
    ones = jnp.ones((2 * w, HEAD_DIM), BF16)
    scale = HEAD_DIM ** -0.5
    n_sub = tq // w
    def probs_stage(s):
        q = jnp.concatenate(
            [q_ref[s * w:(s + 1) * w, g * HEAD_DIM:(g + 1) * HEAD_DIM] for g in range(SWA_GROUP)],
            axis=0)
        if s == 0:
            k = jnp.concatenate([kp_ref[...], k_ref[0:w, :]], axis=0)
            valid = first
        else:
            k = k_ref[(s - 1) * w:(s + 1) * w, :]
            valid = band
        sc = lax.dot_general(q, k, (((1,), (1,)), ((), ())), preferred_element_type=F32) * scale
        sc = jnp.where(valid, sc, masked_fill)
        p_ref[s] = jnp.exp(sc - jnp.max(sc, axis=-1, keepdims=True)).astype(BF16)

    def values_stage(s):
        if s == 0:
            v = jnp.concatenate([vp_ref[...], v_ref[0:w, :]], axis=0)
        else:
            v = v_ref[(s - 1) * w:(s + 1) * w, :]
        v = jnp.where(key_row == 0, jnp.zeros_like(v), v)
        ov = jnp.dot(p_ref[s], jnp.concatenate([v, ones], axis=1), preferred_element_type=F32)
        o = ov[:, :HEAD_DIM] / ov[:, HEAD_DIM:]
        for g in range(SWA_GROUP):
            o_ref[s * w:(s + 1) * w, g * HEAD_DIM:(g + 1) * HEAD_DIM] = (
                o[g * w:(g + 1) * w, :].astype(o_ref.dtype))

    for s in range(n_sub):
        probs_stage(s)
        if s >= 1:
            values_stage(s - 1)
    values_stage(n_sub - 1)


def _swa(proj, sinks, *, batch, seq, q_heads, k_col, v_col):
    kv_heads = q_heads // SWA_GROUP
    w = SWA_WINDOW
    tq = _tile(seq, SWA_STEP_ROWS)
    nq = seq // tq
    sub = tq // w
    gw = SWA_GROUP * HEAD_DIM
    kb, vb = k_col // HEAD_DIM, v_col // HEAD_DIM

    def prev_rows(b, i):
        return b * (seq // w) + jnp.maximum(i * sub - 1, 0)

    grid_spec = pltpu.PrefetchScalarGridSpec(
        num_scalar_prefetch=1,
        grid=(batch, nq, kv_heads),
        in_specs=[
            pl.BlockSpec((tq, gw), lambda b, i, h, s: (b * nq + i, h)),
            pl.BlockSpec((w, HEAD_DIM), lambda b, i, h, s: (prev_rows(b, i), kb + h)),
            pl.BlockSpec((tq, HEAD_DIM), lambda b, i, h, s: (b * nq + i, kb + h)),
            pl.BlockSpec((w, HEAD_DIM), lambda b, i, h, s: (prev_rows(b, i), vb + h)),
            pl.BlockSpec((tq, HEAD_DIM), lambda b, i, h, s: (b * nq + i, vb + h)),
        ],
        out_specs=pl.BlockSpec((tq, gw), lambda b, i, h, s: (b * nq + i, h)),
        scratch_shapes=[pltpu.VMEM((sub, SWA_GROUP * w, 2 * w), BF16)],
    )
    return pl.pallas_call(
        functools.partial(_swa_kernel, tq=tq),
        out_shape=jax.ShapeDtypeStruct((batch * seq, q_heads * HEAD_DIM), BF16),
        grid_spec=grid_spec,
        compiler_params=_cparams(("arbitrary", "arbitrary", "arbitrary")),
        name="swa_attention",
    )(sinks, proj, proj, proj, proj, proj)


def _sb_kernel(q_ref, kc_ref, kp_ref, vc_ref, vp_ref, kv_hbm, o_ref, hilo0_ref, hilo1_ref, lb0_ref,
               lb1_ref, acc_ref, r_ref, kbuf_ref, vbuf_ref, sem, *, tq, hp, seq, k_col, v_col):
    hilo_refs = (hilo0_ref, hilo1_ref)
    lb_refs = (lb0_ref, lb1_ref)
    i = pl.program_id(2)
    row = lax.broadcasted_iota(jnp.int32, (tq, tq), 0)
    col = lax.broadcasted_iota(jnp.int32, (tq, tq), 1)
    strict_lower = col < row
    neg_later = jnp.where(row > col, -1.0, 0.0).astype(BF16)
    neg_later = jnp.concatenate([neg_later, neg_later], axis=0)
    z_scale = HEAD_DIM ** -0.5 * LOG2E
    sign_bit = jnp.uint32(0x80000000)

    def logits_stage(slot, k_ref, diag, heads=range(hp)):
        for h in heads:
            cols = slice(h * HEAD_DIM, (h + 1) * HEAD_DIM)
            q = q_ref[:, cols]
            k = k_ref[:, cols]
            z = lax.dot_general(q, k, (((1,), (1,)), ((), ())),
                                preferred_element_type=F32) * z_scale
            neg_abs = lax.bitcast_convert_type(
                lax.bitcast_convert_type(z, jnp.uint32) | sign_bit, F32)
            softplus = jnp.maximum(z, 0.0) + jnp.log(1.0 + jnp.exp2(neg_abs)) * LOG2E
            log_beta = z - softplus
            if diag:
                lb_refs[slot][h] = jnp.where(strict_lower, log_beta, -jnp.inf)
                softplus = jnp.where(strict_lower, softplus, 0.0)
                r_ref[h] = -jnp.sum(softplus, axis=-1, keepdims=True)
            else:
                r_sum = r_ref[h]
                lb_refs[slot][h] = log_beta + r_sum
                r_ref[h] = r_sum - jnp.sum(softplus, axis=-1, keepdims=True)
            hi = softplus.astype(BF16)
            lo = (softplus - hi.astype(F32)).astype(BF16)
            hilo_refs[slot][h] = jnp.concatenate([hi, lo], axis=1)

    def weights_stage(slot, v_ref, first, heads=range(hp)):
        for h in heads:
            cols = slice(h * HEAD_DIM, (h + 1) * HEAD_DIM)
            v = v_ref[:, cols]
            suffix = jnp.dot(hilo_refs[slot][h], neg_later, preferred_element_type=F32)
            att = jnp.exp2(lb_refs[slot][h] + suffix)
            pv = jnp.dot(att.astype(BF16), v, preferred_element_type=F32)
            if first:
                acc_ref[h] = pv
            else:
                acc_ref[h] += pv

    def any_row_alive():
        top = functools.reduce(
            jnp.maximum, [jnp.max(r_ref[h], axis=0, keepdims=True) for h in range(hp)])
        return top[0, 0] > SB_UNDERFLOW_LOG2

    def staggered(logits_stages, weights_stages):
        lag = 2
        for k, stage in enumerate(logits_stages):
            stage()
            if k >= lag:
                weights_stages[k - lag]()
        for stage in weights_stages[max(len(logits_stages) - lag, 0):]:
            stage()

    def per_head(stage, *args):
        return [functools.partial(stage, *args, heads=(h,)) for h in range(hp)]

    @pl.when(i == 0)
    def _():
        staggered(per_head(logits_stage, 0, kc_ref, True), per_head(weights_stage, 0, vc_ref, True))

    @pl.when(i > 0)
    def _():
        staggered(
            per_head(logits_stage, 0, kc_ref, True) + per_head(logits_stage, 1, kp_ref, False),
            per_head(weights_stage, 0, vc_ref, True) + per_head(weights_stage, 1, vp_ref, False))

        def cond(state):
            n, alive = state
            return jnp.logical_and(n <= i, alive)

        def body(state):
            n, _ = state
            row0 = pl.multiple_of(pl.program_id(0) * seq + (i - n) * tq, tq)
            col0 = pl.program_id(1) * (hp * HEAD_DIM)
            copies = [
                pltpu.make_async_copy(
                    kv_hbm.at[pl.ds(row0, tq), pl.ds(pl.multiple_of(c + col0, LANES), hp * HEAD_DIM)],
                    buf, sem.at[s])
                for s, (c, buf) in enumerate(((k_col, kbuf_ref), (v_col, vbuf_ref)))]
            for cp in copies:
                cp.start()
            for cp in copies:
                cp.wait()
            staggered(per_head(logits_stage, 1, kbuf_ref, False),
                      per_head(weights_stage, 1, vbuf_ref, False))
            return n + 1, any_row_alive()

        lax.while_loop(cond, body, (jnp.int32(2), any_row_alive()))

    for h in range(hp):
        o_ref[:, h * HEAD_DIM:(h + 1) * HEAD_DIM] = acc_ref[h].astype(o_ref.dtype)


def _sb(proj, *, batch, seq, heads, q_col, k_col, v_col):
    tq = _tile(seq, SB_BLOCK)
    nq = seq // tq
    hp = next(c for c in (16, 8, 4, 2, 1) if heads % c == 0)
    hw = hp * HEAD_DIM
    assert q_col % hw == 0 and k_col % hw == 0 and v_col % hw == 0
    qb, kb, vb = q_col // hw, k_col // hw, v_col // hw

    def cur(col):
        return pl.BlockSpec((tq, hw), lambda b, h, i: (b * nq + i, col + h))

    def prev(col):
        return pl.BlockSpec((tq, hw), lambda b, h, i: (b * nq + jnp.maximum(i - 1, 0), col + h))

    return pl.pallas_call(
        functools.partial(_sb_kernel, tq=tq, hp=hp, seq=seq, k_col=k_col, v_col=v_col),
        out_shape=jax.ShapeDtypeStruct((batch * seq, heads * HEAD_DIM), BF16),
        grid=(batch, heads // hp, nq),
        in_specs=[cur(qb), cur(kb), prev(kb), cur(vb), prev(vb),
                  pl.BlockSpec(memory_space=pl.ANY)],
        out_specs=pl.BlockSpec((tq, hw), lambda b, h, i: (b * nq + i, h)),
        scratch_shapes=[pltpu.VMEM((hp, tq, 2 * tq), BF16), pltpu.VMEM((hp, tq, 2 * tq), BF16),
                        pltpu.VMEM((hp, tq, tq), F32), pltpu.VMEM((hp, tq, tq), F32),
                        pltpu.VMEM((hp, tq, HEAD_DIM), F32), pltpu.VMEM((hp, tq, 1), F32),
                        pltpu.VMEM((tq, hw), BF16), pltpu.VMEM((tq, hw), BF16),
                        pltpu.SemaphoreType.DMA((2,))],
        compiler_params=_cparams(("arbitrary", "arbitrary", "arbitrary")),
        name="sb_attention",
    )(proj, proj, proj, proj, proj, proj)


def _merge_kernel(a1_ref, a2_ref, w1_ref, w2_ref, ga_ref, gb_ref, o_ref):
    y1 = jnp.dot(a1_ref[...], w1_ref[...], preferred_element_type=F32)
    y2 = jnp.dot(a2_ref[...], w2_ref[...], preferred_element_type=F32)
    o = ga_ref[...].astype(F32) * y1 + gb_ref[...].astype(F32) * y2
    o_ref[...] = o.astype(o_ref.dtype)


def _merge(a_swa, a_sb, w_swa, w_sb, gates):
    t, k1 = a_swa.shape
    k2 = a_sb.shape[1]
    d = w_swa.shape[1]
    tm = _tile(t, MM_TILE)
    tn = _tile(d, MM_TILE)
    ga, gb = 0, d // tn
    return pl.pallas_call(
        _merge_kernel,
        out_shape=jax.ShapeDtypeStruct((t, d), BF16),
        grid=(t // tm, d // tn),
        in_specs=[
            pl.BlockSpec((tm, k1), lambda i, j: (i, 0)),
            pl.BlockSpec((tm, k2), lambda i, j: (i, 0)),
            pl.BlockSpec((k1, tn), lambda i, j: (0, j)),
            pl.BlockSpec((k2, tn), lambda i, j: (0, j)),
            pl.BlockSpec((tm, tn), lambda i, j: (i, ga + j)),
            pl.BlockSpec((tm, tn), lambda i, j: (i, gb + j)),
        ],
        out_specs=pl.BlockSpec((tm, tn), lambda i, j: (i, j)),
        compiler_params=_cparams(("arbitrary", "arbitrary")),
        name="branch_merge",
    )(a_swa, a_sb, w_swa, w_sb, gates, gates)


def _mm_kernel(*refs, epilogue, has_bias, n_casts):
    a_ref, w_ref = refs[:2]
    b_ref = refs[2] if has_bias else None
    first_cast = 2 + has_bias
    cast_in = refs[first_cast:first_cast + n_casts]
    o_ref = refs[first_cast + n_casts]
    cast_out = refs[first_cast + n_casts + 1:]
    for src, dst in zip(cast_in, cast_out):
        dst[...] = src[...].astype(dst.dtype)
    acc = jnp.dot(a_ref[...], w_ref[...], preferred_element_type=F32)
    if has_bias:
        acc = acc + b_ref[...]
    if epilogue == "relu_sq":
        acc = jnp.square(jnp.maximum(acc, 0.0))
    elif epilogue == "sigmoid":
        acc = 0.5 * jnp.tanh(0.5 * acc) + 0.5
    o_ref[...] = acc.astype(o_ref.dtype)


def _cast_plan(casts, ni, nj):
    steps = ni * nj
    in_specs, out_specs, args, shapes = [], [], [], []
    for c in casts:
        c, col0, cols = c if isinstance(c, tuple) else (c, 0, c.shape[1])
        rows = c.shape[0]
        if col0 == 0 and cols == c.shape[1] and rows % (steps * BF16_SUBLANES) == 0:
            slab = rows // steps
            spec = pl.BlockSpec((1, slab, cols), lambda i, j: (i * nj + j, 0, 0))
            in_specs.append(spec)
            out_specs.append(spec)
            args.append(c.reshape(steps, slab, cols))
            shapes.append(jax.ShapeDtypeStruct((steps, slab, cols), BF16))
        else:
            slab = cols // steps
            assert cols % (steps * LANES) == 0 and col0 % slab == 0, (c.shape, col0, cols, steps)
            first = col0 // slab
            in_specs.append(pl.BlockSpec((rows, slab), lambda i, j: (0, first + i * nj + j)))
            out_specs.append(pl.BlockSpec((rows, slab), lambda i, j: (0, i * nj + j)))
            args.append(c)
            shapes.append(jax.ShapeDtypeStruct((rows, cols), BF16))
    return in_specs, out_specs, args, shapes


def _mm(a, w, out_dtype, *, bias=None, epilogue=None, casts=(), tn_pref=MM_TILE, name):
    t, k = a.shape
    n = w.shape[1]
    tm = _tile(t, MM_TILE)
    tn = tn_pref if n % tn_pref == 0 else _tile(n, MM_TILE)
    ni, nj = t // tm, n // tn
    in_specs = [pl.BlockSpec((tm, k), lambda i, j: (i, 0)),
                pl.BlockSpec((k, tn), lambda i, j: (0, j))]
    args = [a, w]
    if bias is not None:
        in_specs.append(pl.BlockSpec((1, tn), lambda i, j: (0, j)))
        args.append(bias.reshape(1, n))
    cast_in_specs, cast_out_specs, cast_args, cast_shapes = _cast_plan(casts, ni, nj)
    in_specs += cast_in_specs
    args += cast_args
    out_specs = [pl.BlockSpec((tm, tn), lambda i, j: (i, j))] + cast_out_specs
    out_shape = [jax.ShapeDtypeStruct((t, n), out_dtype)] + cast_shapes
    outs = pl.pallas_call(
        functools.partial(_mm_kernel, epilogue=epilogue, has_bias=bias is not None,
                          n_casts=len(casts)),
        out_shape=out_shape,
        grid=(ni, nj),
        in_specs=in_specs,
        out_specs=out_specs,
        compiler_params=_cparams(("arbitrary", "arbitrary")),
        name=name,
    )(*args)
    return outs[0], [o.reshape(-1, o.shape[-1]) for o in outs[1:]]


def _mm_ksplit_kernel(a_ref, w_ref, o_ref):
    @pl.when(pl.program_id(2) == 0)
    def _():
        o_ref[...] = jnp.zeros_like(o_ref)

    o_ref[...] += jnp.dot(a_ref[...], w_ref[...], preferred_element_type=F32)


def _mm_ksplit(a, w, *, name):
    t, k = a.shape
    n = w.shape[1]
    tm = _tile(t, MM_TILE)
    tn = _tile(n, MM_TILE)
    tk = _tile(k, MM_K_TILE)
    return pl.pallas_call(
        _mm_ksplit_kernel,
        out_shape=jax.ShapeDtypeStruct((t, n), F32),
        grid=(t // tm, n // tn, k // tk),
        in_specs=[pl.BlockSpec((tm, tk), lambda i, j, kk: (i, kk)),
                  pl.BlockSpec((tk, tn), lambda i, j, kk: (kk, j))],
        out_specs=pl.BlockSpec((tm, tn), lambda i, j, kk: (i, j)),
        compiler_params=_cparams(("arbitrary", "arbitrary", "arbitrary")),
        name=name,
    )(a, w)


def _rope_tables(seq):
    inv_freq = 1.0 / (ROPE_THETA ** (jnp.arange(0, HEAD_DIM, 2, dtype=F32) / HEAD_DIM))
    ang = jnp.arange(seq, dtype=F32)[:, None] * inv_freq[None, :]
    cos, sin = jnp.cos(ang), jnp.sin(ang)
    return jnp.concatenate([cos, cos], axis=-1), jnp.concatenate([-sin, sin], axis=-1)


def kernel(x, c, w_ada, b_ada, g_pre_mix, g_post_mix, g_pre_mlp, g_post_mlp, w_in, attn_sinks,
           w_branch_gate, b_branch_gate, w_proj_swa, w_proj_sb, w_out, w_ff_in, w_ff_out):
    batch, seq, d = x.shape
    depth = w_in.shape[0]
    d_in = w_in.shape[2]
    swa_q_heads = attn_sinks.shape[1]
    swa_q_w = w_proj_swa.shape[1]
    sb_w = w_proj_sb.shape[1]
    swa_kv_w = (d_in - swa_q_w - 3 * sb_w) // 2
    sb_heads = sb_w // HEAD_DIM
    assert swa_q_w == swa_q_heads * HEAD_DIM and swa_kv_w * SWA_GROUP == swa_q_w
    assert seq % SWA_WINDOW == 0
    t = batch * seq

    cos_full, sin_signed = _rope_tables(seq)
    pad_rows = (-batch) % 8
    c_pad = jnp.pad(c, ((0, pad_rows), (0, 0)))

    swa_w = swa_q_w + 2 * swa_kv_w
    n_rope = swa_q_w + swa_kv_w

    for l in range(depth):
        mod = _ada_mod(c_pad, w_ada[l], b_ada[l])[:batch]
        mod3 = mod.reshape(batch * N_MOD, 1, d)

        h = _prenorm(x, g_pre_mix[l], mod3, 0, 1).reshape(t, d)
        proj_swa, (w_sb_in_bf,) = _rope_proj(
            h, w_in[l][:, :swa_w].astype(BF16), cos_full, sin_signed, n_rope=n_rope, seq=seq,
            casts=((w_in[l], swa_w, d_in - swa_w),))
        proj_sb, (w_gate_bf,) = _mm(h, w_sb_in_bf, BF16, casts=(w_branch_gate[l],),
                                    tn_pref=SB_PROJ_TN, name="sb_proj")
        gates, (w_swa_bf, w_sb_bf, w_out_bf, w_ff_in_bf) = _mm(
            h, w_gate_bf, BF16, bias=b_branch_gate[l], epilogue="sigmoid",
            casts=(w_proj_swa[l], w_proj_sb[l], w_out[l], w_ff_in[l]), name="gate_proj")
        a_swa = _swa(proj_swa, attn_sinks[l], batch=batch, seq=seq, q_heads=swa_q_heads,
                     k_col=swa_q_w, v_col=n_rope)
        a_sb = _sb(proj_sb, batch=batch, seq=seq, heads=sb_heads,
                   q_col=0, k_col=sb_w, v_col=2 * sb_w)
        merged = _merge(a_swa, a_sb, w_swa_bf, w_sb_bf, gates)
        mixed, _ = _mm(merged, w_out_bf, F32, name="out_proj")
        act, x1, (w_ff_out_bf,) = _mlp_in(
            mixed, x.reshape(t, d), g_post_mix[l], g_pre_mlp[l], mod3, w_ff_in_bf, seq=seq,
            k_gate=2, k_shift=3, k_scale=4, casts=(w_ff_out[l],))
        f = _mm_ksplit(act, w_ff_out_bf, name="ff_out").reshape(batch, seq, d)
        x = _post(f, x1.reshape(batch, seq, d), g_post_mlp[l], mod3, 5)
    return x
```

```python
import functools

import jax
import jax.numpy as jnp
from jax import lax
from jax.experimental import pallas as pl
from jax.experimental.pallas import tpu as pltpu

F32 = jnp.float32
BF16 = jnp.bfloat16

HEAD_DIM = 128
LANES = 128
BF16_SUBLANES = 16
SWA_WINDOW = 128
SWA_GROUP = 4
ROPE_THETA = 10000.0
NORM_EPS = 1e-6
LOG2E = 1.4426950408889634
SB_UNDERFLOW_LOG2 = -152.0
N_MOD = 6
MIB = 1024 * 1024
VMEM_LIMIT_BYTES = 56 * MIB

MM_TILE = 1024
MM_K_TILE = 4096
SB_PROJ_TN = 768
NORM_ROWS = 512
SWA_STEP_ROWS = 4096
SB_BLOCK = 256


def _tile(dim, pref):
    t = min(dim, pref)
    while dim % t:
        t //= 2
    return t


def _cparams(semantics):
    return pltpu.CompilerParams(dimension_semantics=semantics, vmem_limit_bytes=VMEM_LIMIT_BYTES)


def _ada_kernel(c_ref, w_ref, b_ref, o_ref):
    c = c_ref[...]
    act = (c * jax.nn.sigmoid(c)).astype(BF16)
    o_ref[...] = jnp.dot(act, w_ref[...].astype(BF16), preferred_element_type=F32) + b_ref[...]


def _ada_mod(c_pad, w_ada, b_ada):
    rows, d = c_pad.shape
    n = w_ada.shape[1]
    tn = _tile(n, MM_TILE)
    return pl.pallas_call(
        _ada_kernel,
        out_shape=jax.ShapeDtypeStruct((rows, n), F32),
        grid=(n // tn,),
        in_specs=[
            pl.BlockSpec((rows, d), lambda j: (0, 0)),
            pl.BlockSpec((d, tn), lambda j: (0, j)),
            pl.BlockSpec((1, tn), lambda j: (0, j)),
        ],
        out_specs=pl.BlockSpec((rows, tn), lambda j: (0, j)),
        compiler_params=_cparams(("arbitrary",)),
        name="ada_mod",
    )(c_pad, w_ada, b_ada.reshape(1, n))


def _rms(x, g):
    y = x * lax.rsqrt(jnp.mean(x * x, axis=-1, keepdims=True) + NORM_EPS)
    return y * g


def _prenorm_kernel(x_ref, g_ref, scale_ref, shift_ref, h_ref):
    h = _rms(x_ref[0], g_ref[...]) * (1.0 + scale_ref[0]) + shift_ref[0]
    h_ref[0] = h.astype(h_ref.dtype)


def _mod_spec(d, k):
    return pl.BlockSpec((1, 1, d), lambda b, i: (b * N_MOD + k, 0, 0))


def _prenorm(x, g, mod3, k_shift, k_scale):
    b, s, d = x.shape
    ts = _tile(s, NORM_ROWS)
    row = pl.BlockSpec((1, ts, d), lambda bb, i: (bb, i, 0))
    return pl.pallas_call(
        _prenorm_kernel,
        out_shape=jax.ShapeDtypeStruct((b, s, d), BF16),
        grid=(b, s // ts),
        in_specs=[row, pl.BlockSpec((1, d), lambda bb, i: (0, 0)),
                  _mod_spec(d, k_scale), _mod_spec(d, k_shift)],
        out_specs=row,
        compiler_params=_cparams(("arbitrary", "arbitrary")),
        name="prenorm",
    )(x, g.reshape(1, d), mod3, mod3)


def _post_pre_math(y, x, gpost, gate, gpre, scale, shift):
    x1 = x + _rms(y, gate * gpost)
    h = _rms(x1, gpre * (1.0 + scale)) + shift
    return x1, h


def _first_h2_kernel(y_ref, x_ref, gpost_ref, gate_ref, gpre_ref, scale_ref, shift_ref, h_ref):
    _, h = _post_pre_math(y_ref[...], x_ref[...], gpost_ref[...], gate_ref[0], gpre_ref[...],
                          scale_ref[0], shift_ref[0])
    h_ref[...] = h.astype(h_ref.dtype)


def _first_h2(y, x, gpost, gpre, mod3, rows, k_gate, k_shift, k_scale):
    d = x.shape[1]
    ts = _tile(rows, NORM_ROWS)
    row = pl.BlockSpec((ts, d), lambda i: (i, 0))
    vec = pl.BlockSpec((1, d), lambda i: (0, 0))

    def mod(k):
        return pl.BlockSpec((1, 1, d), lambda i: (k, 0, 0))

    return pl.pallas_call(
        _first_h2_kernel,
        out_shape=jax.ShapeDtypeStruct((rows, d), BF16),
        grid=(rows // ts,),
        in_specs=[row, row, vec, mod(k_gate), vec, mod(k_scale), mod(k_shift)],
        out_specs=row,
        compiler_params=_cparams(("arbitrary",)),
        name="first_h2",
    )(y, x, gpost.reshape(1, d), mod3, gpre.reshape(1, d), mod3, mod3)


def _mlp_in_kernel(*refs, chunk, n_casts):
    (y_ref, x_ref, gpost_ref, gate_ref, gpre_ref, scale_ref, shift_ref, h_first_hbm,
     w_ref) = refs[:9]
    cast_in = refs[9:9 + n_casts]
    act_ref, x1_ref = refs[9 + n_casts:11 + n_casts]
    cast_out = refs[11 + n_casts:11 + 2 * n_casts]
    h_even, h_odd, sem = refs[11 + 2 * n_casts:]
    i = pl.program_id(0)
    j = pl.program_id(1)

    @pl.when(jnp.logical_and(i == 0, j == 0))
    def _():
        first = pltpu.make_async_copy(h_first_hbm, h_even, sem)
        first.start()
        first.wait()

    for src, dst in zip(cast_in, cast_out):
        dst[...] = src[...].astype(dst.dtype)

    def step(h_cur, h_next):
        x1, h = _post_pre_math(y_ref[...], x_ref[...], gpost_ref[...], gate_ref[0], gpre_ref[...],
                               scale_ref[0], shift_ref[0])
        x1_ref[...] = x1
        h_next[pl.ds(pl.multiple_of(j * chunk, chunk), chunk), :] = h.astype(h_next.dtype)
        acc = jnp.dot(h_cur[...], w_ref[...], preferred_element_type=F32)
        act_ref[...] = jnp.square(jnp.maximum(acc, 0.0)).astype(act_ref.dtype)

    @pl.when(i % 2 == 0)
    def _():
        step(h_even, h_odd)

    @pl.when(i % 2 == 1)
    def _():
        step(h_odd, h_even)


def _mlp_in(y, x, gpost, gpre, mod3, w, *, seq, k_gate, k_shift, k_scale, casts=()):
    t, d = x.shape
    n = w.shape[1]
    tm = _tile(seq, MM_TILE)
    tn = _tile(n, MM_TILE)
    ni, nj = t // tm, n // tn
    chunk = tm // nj
    assert tm % nj == 0 and chunk % BF16_SUBLANES == 0
    blocks_per_batch = seq // tm

    def nxt(i):
        return (i + 1) % ni

    rows = pl.BlockSpec((chunk, d), lambda i, j: (nxt(i) * nj + j, 0))
    vec = pl.BlockSpec((1, d), lambda i, j: (0, 0))

    def mod(k):
        return pl.BlockSpec((1, 1, d),
                            lambda i, j: ((nxt(i) // blocks_per_batch) * N_MOD + k, 0, 0))

    h_first = _first_h2(y, x, gpost, gpre, mod3, tm, k_gate, k_shift, k_scale)
    cast_in_specs, cast_out_specs, cast_args, cast_shapes = _cast_plan(casts, ni, nj)
    outs = pl.pallas_call(
        functools.partial(_mlp_in_kernel, chunk=chunk, n_casts=len(casts)),
        out_shape=[jax.ShapeDtypeStruct((t, n), BF16), jax.ShapeDtypeStruct((t, d), F32)]
        + cast_shapes,
        grid=(ni, nj),
        in_specs=[rows, rows, vec, mod(k_gate), vec, mod(k_scale), mod(k_shift),
                  pl.BlockSpec(memory_space=pl.ANY),
                  pl.BlockSpec((d, tn), lambda i, j: (0, j))] + cast_in_specs,
        out_specs=[pl.BlockSpec((tm, tn), lambda i, j: (i, j)), rows] + cast_out_specs,
        scratch_shapes=[pltpu.VMEM((tm, d), BF16), pltpu.VMEM((tm, d), BF16),
                        pltpu.SemaphoreType.DMA(())],
        compiler_params=_cparams(("arbitrary", "arbitrary")),
        name="ff_in",
    )(y, x, gpost.reshape(1, d), mod3, gpre.reshape(1, d), mod3, mod3, h_first, w, *cast_args)
    return outs[0], outs[1], [o.reshape(-1, o.shape[-1]) for o in outs[2:]]


def _post_kernel(y_ref, x_ref, gpost_ref, gate_ref, o_ref):
    o_ref[0] = x_ref[0] + gate_ref[0] * _rms(y_ref[0], gpost_ref[...])


def _post(y, x, gpost, mod3, k_gate):
    b, s, d = x.shape
    ts = _tile(s, NORM_ROWS)
    row = pl.BlockSpec((1, ts, d), lambda bb, i: (bb, i, 0))
    return pl.pallas_call(
        _post_kernel,
        out_shape=jax.ShapeDtypeStruct((b, s, d), F32),
        grid=(b, s // ts),
        in_specs=[row, row, pl.BlockSpec((1, d), lambda bb, i: (0, 0)), _mod_spec(d, k_gate)],
        out_specs=row,
        compiler_params=_cparams(("arbitrary", "arbitrary")),
        name="post_norm",
    )(y, x, gpost.reshape(1, d), mod3)


def _rope_proj_kernel(*refs, full_rope_tiles, partial_chunks, n_casts):
    a_ref, w_ref, cos_ref, sin_ref = refs[:4]
    cast_in = refs[4:4 + n_casts]
    o_ref = refs[4 + n_casts]
    cast_out = refs[5 + n_casts:]
    for src, dst in zip(cast_in, cast_out):
        dst[...] = src[...].astype(dst.dtype)
    j = pl.program_id(1)
    acc = jnp.dot(a_ref[...], w_ref[...], preferred_element_type=F32)
    cos, sin = cos_ref[...], sin_ref[...]
    head = j <= full_rope_tiles
    tail = j < full_rope_tiles
    tables = {True: (jnp.where(head, cos, 1.0), jnp.where(head, sin, 0.0)),
              False: (jnp.where(tail, cos, 1.0), jnp.where(tail, sin, 0.0))}
    for c in range(acc.shape[1] // LANES):
        cols = slice(c * LANES, (c + 1) * LANES)
        cos_c, sin_c = tables[c < partial_chunks]
        blk = acc[:, cols]
        o_ref[:, cols] = (blk * cos_c + pltpu.roll(blk, HEAD_DIM // 2, 1) * sin_c).astype(o_ref.dtype)


def _rope_proj(h, w, cos_full, sin_signed, *, n_rope, seq, casts=()):
    t, d = h.shape
    n = w.shape[1]
    tm = _tile(seq, MM_TILE)
    tn = _tile(n, MM_TILE)
    assert n_rope % LANES == 0 and tn % LANES == 0
    n_seq_tiles = seq // tm
    ni, nj = t // tm, n // tn
    cast_in_specs, cast_out_specs, cast_args, cast_shapes = _cast_plan(casts, ni, nj)
    kern = functools.partial(_rope_proj_kernel, full_rope_tiles=n_rope // tn,
                             partial_chunks=(n_rope % tn) // LANES, n_casts=len(casts))
    outs = pl.pallas_call(
        kern,
        out_shape=[jax.ShapeDtypeStruct((t, n), BF16)] + cast_shapes,
        grid=(ni, nj),
        in_specs=[
            pl.BlockSpec((tm, d), lambda i, j: (i, 0)),
            pl.BlockSpec((d, tn), lambda i, j: (0, j)),
            pl.BlockSpec((tm, HEAD_DIM), lambda i, j: (i % n_seq_tiles, 0)),
            pl.BlockSpec((tm, HEAD_DIM), lambda i, j: (i % n_seq_tiles, 0)),
        ] + cast_in_specs,
        out_specs=[pl.BlockSpec((tm, tn), lambda i, j: (i, j))] + cast_out_specs,
        compiler_params=_cparams(("arbitrary", "arbitrary")),
        name="swa_proj",
    )(h, w, cos_full, sin_signed, *cast_args)
    return outs[0], [o.reshape(-1, o.shape[-1]) for o in outs[1:]]


def _swa_kernel(sink_ref, q_ref, kp_ref, k_ref, vp_ref, v_ref, o_ref, p_ref, *, tq):
    i = pl.program_id(1)
    kvh = pl.program_id(2)
    w = SWA_WINDOW
    rows = SWA_GROUP * w
    qi = lax.broadcasted_iota(jnp.int32, (rows, 2 * w), 0) % w
    kj = lax.broadcasted_iota(jnp.int32, (rows, 2 * w), 1)
    band = jnp.logical_and(kj > qi, kj <= qi + w)
    first = jnp.logical_and(band, jnp.logical_or(kj >= w, i > 0))
    sink_fill = jnp.concatenate(
        [jnp.full((w, 2 * w), sink_ref[kvh * SWA_GROUP + g], F32) for g in range(SWA_GROUP)], axis=0)
    masked_fill = jnp.where(kj == 0, sink_fill * LOG2E, -jnp.inf)
    key_row = lax.broadcasted_iota(jnp.int32, (2 * w, HEAD_DIM), 0)
    ones = jnp.ones((2 * w, HEAD_DIM), BF16)
    scale = HEAD_DIM ** -0.5 * LOG2E
    n_sub = tq // w
    def probs_stage(s):
        q = jnp.concatenate(
            [q_ref[s * w:(s + 1) * w, g * HEAD_DIM:(g + 1) * HEAD_DIM] for g in range(SWA_GROUP)],
            axis=0)
        if s == 0:
            k = jnp.concatenate([kp_ref[...], k_ref[0:w, :]], axis=0)
            valid = first
        else:
            k = k_ref[(s - 1) * w:(s + 1) * w, :]
            valid = band
        sc = lax.dot_general(q, k, (((1,), (1,)), ((), ())), preferred_element_type=F32) * scale
        sc = jnp.where(valid, sc, masked_fill)
        p_ref[s] = jnp.exp2(sc - jnp.max(sc, axis=-1, keepdims=True)).astype(BF16)

    def values_stage(s):
        if s == 0:
            v = jnp.concatenate([vp_ref[...], v_ref[0:w, :]], axis=0)
        else:
            v = v_ref[(s - 1) * w:(s + 1) * w, :]
        v = jnp.where(key_row == 0, jnp.zeros_like(v), v)
        ov = jnp.dot(p_ref[s], jnp.concatenate([v, ones], axis=1), preferred_element_type=F32)
        o = ov[:, :HEAD_DIM] / ov[:, HEAD_DIM:]
        for g in range(SWA_GROUP):
            o_ref[s * w:(s + 1) * w, g * HEAD_DIM:(g + 1) * HEAD_DIM] = (
                o[g * w:(g + 1) * w, :].astype(o_ref.dtype))

    for s in range(n_sub):
        probs_stage(s)
        if s >= 1:
            values_stage(s - 1)
    values_stage(n_sub - 1)


def _swa(proj, sinks, *, batch, seq, q_heads, k_col, v_col):
    kv_heads = q_heads // SWA_GROUP
    w = SWA_WINDOW
    tq = _tile(seq, SWA_STEP_ROWS)
    nq = seq // tq
    sub = tq // w
    gw = SWA_GROUP * HEAD_DIM
    kb, vb = k_col // HEAD_DIM, v_col // HEAD_DIM

    def prev_rows(b, i):
        return b * (seq // w) + jnp.maximum(i * sub - 1, 0)

    grid_spec = pltpu.PrefetchScalarGridSpec(
        num_scalar_prefetch=1,
        grid=(batch, nq, kv_heads),
        in_specs=[
            pl.BlockSpec((tq, gw), lambda b, i, h, s: (b * nq + i, h)),
            pl.BlockSpec((w, HEAD_DIM), lambda b, i, h, s: (prev_rows(b, i), kb + h)),
            pl.BlockSpec((tq, HEAD_DIM), lambda b, i, h, s: (b * nq + i, kb + h)),
            pl.BlockSpec((w, HEAD_DIM), lambda b, i, h, s: (prev_rows(b, i), vb + h)),
            pl.BlockSpec((tq, HEAD_DIM), lambda b, i, h, s: (b * nq + i, vb + h)),
        ],
        out_specs=pl.BlockSpec((tq, gw), lambda b, i, h, s: (b * nq + i, h)),
        scratch_shapes=[pltpu.VMEM((sub, SWA_GROUP * w, 2 * w), BF16)],
    )
    return pl.pallas_call(
        functools.partial(_swa_kernel, tq=tq),
        out_shape=jax.ShapeDtypeStruct((batch * seq, q_heads * HEAD_DIM), BF16),
        grid_spec=grid_spec,
        compiler_params=_cparams(("arbitrary", "arbitrary", "arbitrary")),
        name="swa_attention",
    )(sinks, proj, proj, proj, proj, proj)


def _sb_kernel(q_ref, kc_ref, kp_ref, vc_ref, vp_ref, kv_hbm, o_ref, hilo0_ref, hilo1_ref, lb0_ref,
               lb1_ref, acc_ref, r_ref, kbuf_ref, vbuf_ref, sem, *, tq, hp, seq, k_col, v_col):
    hilo_refs = (hilo0_ref, hilo1_ref)
    lb_refs = (lb0_ref, lb1_ref)
    i = pl.program_id(2)
    row = lax.broadcasted_iota(jnp.int32, (tq, tq), 0)
    col = lax.broadcasted_iota(jnp.int32, (tq, tq), 1)
    strict_lower = col < row
    neg_later = jnp.where(row > col, -1.0, 0.0).astype(BF16)
    neg_later = jnp.concatenate([neg_later, neg_later], axis=0)
    z_scale = HEAD_DIM ** -0.5 * LOG2E
    sign_bit = jnp.uint32(0x80000000)

    def logits_stage(slot, k_ref, diag, heads=range(hp)):
        for h in heads:
            cols = slice(h * HEAD_DIM, (h + 1) * HEAD_DIM)
            q = q_ref[:, cols]
            k = k_ref[:, cols]
            z = lax.dot_general(q, k, (((1,), (1,)), ((), ())),
                                preferred_element_type=F32) * z_scale
            neg_abs = lax.bitcast_convert_type(
                lax.bitcast_convert_type(z, jnp.uint32) | sign_bit, F32)
            softplus = jnp.maximum(z, 0.0) + jnp.log(1.0 + jnp.exp2(neg_abs)) * LOG2E
            log_beta = z - softplus
            if diag:
                lb_refs[slot][h] = jnp.where(strict_lower, log_beta, -jnp.inf)
                softplus = jnp.where(strict_lower, softplus, 0.0)
                r_ref[h] = -jnp.sum(softplus, axis=-1, keepdims=True)
            else:
                r_sum = r_ref[h]
                lb_refs[slot][h] = log_beta + r_sum
                r_ref[h] = r_sum - jnp.sum(softplus, axis=-1, keepdims=True)
            hi = softplus.astype(BF16)
            lo = (softplus - hi.astype(F32)).astype(BF16)
            hilo_refs[slot][h] = jnp.concatenate([hi, lo], axis=1)

    def weights_stage(slot, v_ref, first, heads=range(hp)):
        for h in heads:
            cols = slice(h * HEAD_DIM, (h + 1) * HEAD_DIM)
            v = v_ref[:, cols]
            suffix = jnp.dot(hilo_refs[slot][h], neg_later, preferred_element_type=F32)
            att = jnp.exp2(lb_refs[slot][h] + suffix)
            pv = jnp.dot(att.astype(BF16), v, preferred_element_type=F32)
            if first:
                acc_ref[h] = pv
            else:
                acc_ref[h] += pv

    def any_row_alive():
        top = functools.reduce(
            jnp.maximum, [jnp.max(r_ref[h], axis=0, keepdims=True) for h in range(hp)])
        return top[0, 0] > SB_UNDERFLOW_LOG2

    def staggered(logits_stages, weights_stages):
        lag = 2
        for k, stage in enumerate(logits_stages):
            stage()
            if k >= lag:
                weights_stages[k - lag]()
        for stage in weights_stages[max(len(logits_stages) - lag, 0):]:
            stage()

    def per_head(stage, *args):
        return [functools.partial(stage, *args, heads=(h,)) for h in range(hp)]

    @pl.when(i == 0)
    def _():
        staggered(per_head(logits_stage, 0, kc_ref, True), per_head(weights_stage, 0, vc_ref, True))

    @pl.when(i > 0)
    def _():
        staggered(
            per_head(logits_stage, 0, kc_ref, True) + per_head(logits_stage, 1, kp_ref, False),
            per_head(weights_stage, 0, vc_ref, True) + per_head(weights_stage, 1, vp_ref, False))

        def cond(state):
            n, alive = state
            return jnp.logical_and(n <= i, alive)

        def body(state):
            n, _ = state
            row0 = pl.multiple_of(pl.program_id(0) * seq + (i - n) * tq, tq)
            col0 = pl.program_id(1) * (hp * HEAD_DIM)
            copies = [
                pltpu.make_async_copy(
                    kv_hbm.at[pl.ds(row0, tq), pl.ds(pl.multiple_of(c + col0, LANES), hp * HEAD_DIM)],
                    buf, sem.at[s])
                for s, (c, buf) in enumerate(((k_col, kbuf_ref), (v_col, vbuf_ref)))]
            for cp in copies:
                cp.start()
            for cp in copies:
                cp.wait()
            staggered(per_head(logits_stage, 1, kbuf_ref, False),
                      per_head(weights_stage, 1, vbuf_ref, False))
            return n + 1, any_row_alive()

        lax.while_loop(cond, body, (jnp.int32(2), any_row_alive()))

    for h in range(hp):
        o_ref[:, h * HEAD_DIM:(h + 1) * HEAD_DIM] = acc_ref[h].astype(o_ref.dtype)


def _sb(proj, *, batch, seq, heads, q_col, k_col, v_col):
    tq = _tile(seq, SB_BLOCK)
    nq = seq // tq
    hp = next(c for c in (16, 8, 4, 2, 1) if heads % c == 0)
    hw = hp * HEAD_DIM
    assert q_col % hw == 0 and k_col % hw == 0 and v_col % hw == 0
    qb, kb, vb = q_col // hw, k_col // hw, v_col // hw

    def cur(col):
        return pl.BlockSpec((tq, hw), lambda b, h, i: (b * nq + i, col + h))

    def prev(col):
        return pl.BlockSpec((tq, hw), lambda b, h, i: (b * nq + jnp.maximum(i - 1, 0), col + h))

    return pl.pallas_call(
        functools.partial(_sb_kernel, tq=tq, hp=hp, seq=seq, k_col=k_col, v_col=v_col),
        out_shape=jax.ShapeDtypeStruct((batch * seq, heads * HEAD_DIM), BF16),
        grid=(batch, heads // hp, nq),
        in_specs=[cur(qb), cur(kb), prev(kb), cur(vb), prev(vb),
                  pl.BlockSpec(memory_space=pl.ANY)],
        out_specs=pl.BlockSpec((tq, hw), lambda b, h, i: (b * nq + i, h)),
        scratch_shapes=[pltpu.VMEM((hp, tq, 2 * tq), BF16), pltpu.VMEM((hp, tq, 2 * tq), BF16),
                        pltpu.VMEM((hp, tq, tq), F32), pltpu.VMEM((hp, tq, tq), F32),
                        pltpu.VMEM((hp, tq, HEAD_DIM), F32), pltpu.VMEM((hp, tq, 1), F32),
                        pltpu.VMEM((tq, hw), BF16), pltpu.VMEM((tq, hw), BF16),
                        pltpu.SemaphoreType.DMA((2,))],
        compiler_params=_cparams(("arbitrary", "arbitrary", "arbitrary")),
        name="sb_attention",
    )(proj, proj, proj, proj, proj, proj)


def _merge_kernel(a1_ref, a2_ref, w1_ref, w2_ref, ga_ref, gb_ref, o_ref):
    y1 = jnp.dot(a1_ref[...], w1_ref[...], preferred_element_type=F32)
    y2 = jnp.dot(a2_ref[...], w2_ref[...], preferred_element_type=F32)
    o = ga_ref[...].astype(F32) * y1 + gb_ref[...].astype(F32) * y2
    o_ref[...] = o.astype(o_ref.dtype)


def _merge(a_swa, a_sb, w_swa, w_sb, gates):
    t, k1 = a_swa.shape
    k2 = a_sb.shape[1]
    d = w_swa.shape[1]
    tm = _tile(t, MM_TILE)
    tn = _tile(d, MM_TILE)
    ga, gb = 0, d // tn
    return pl.pallas_call(
        _merge_kernel,
        out_shape=jax.ShapeDtypeStruct((t, d), BF16),
        grid=(t // tm, d // tn),
        in_specs=[
            pl.BlockSpec((tm, k1), lambda i, j: (i, 0)),
            pl.BlockSpec((tm, k2), lambda i, j: (i, 0)),
            pl.BlockSpec((k1, tn), lambda i, j: (0, j)),
            pl.BlockSpec((k2, tn), lambda i, j: (0, j)),
            pl.BlockSpec((tm, tn), lambda i, j: (i, ga + j)),
            pl.BlockSpec((tm, tn), lambda i, j: (i, gb + j)),
        ],
        out_specs=pl.BlockSpec((tm, tn), lambda i, j: (i, j)),
        compiler_params=_cparams(("arbitrary", "arbitrary")),
        name="branch_merge",
    )(a_swa, a_sb, w_swa, w_sb, gates, gates)


def _mm_kernel(*refs, epilogue, has_bias, n_casts):
    a_ref, w_ref = refs[:2]
    b_ref = refs[2] if has_bias else None
    first_cast = 2 + has_bias
    cast_in = refs[first_cast:first_cast + n_casts]
    o_ref = refs[first_cast + n_casts]
    cast_out = refs[first_cast + n_casts + 1:]
    for src, dst in zip(cast_in, cast_out):
        dst[...] = src[...].astype(dst.dtype)
    acc = jnp.dot(a_ref[...], w_ref[...], preferred_element_type=F32)
    if has_bias:
        acc = acc + b_ref[...]
    if epilogue == "relu_sq":
        acc = jnp.square(jnp.maximum(acc, 0.0))
    elif epilogue == "sigmoid":
        acc = 0.5 * jnp.tanh(0.5 * acc) + 0.5
    o_ref[...] = acc.astype(o_ref.dtype)


def _cast_plan(casts, ni, nj):
    steps = ni * nj
    in_specs, out_specs, args, shapes = [], [], [], []
    for c in casts:
        c, col0, cols = c if isinstance(c, tuple) else (c, 0, c.shape[1])
        rows = c.shape[0]
        if col0 == 0 and cols == c.shape[1] and rows % (steps * BF16_SUBLANES) == 0:
            slab = rows // steps
            spec = pl.BlockSpec((1, slab, cols), lambda i, j: (i * nj + j, 0, 0))
            in_specs.append(spec)
            out_specs.append(spec)
            args.append(c.reshape(steps, slab, cols))
            shapes.append(jax.ShapeDtypeStruct((steps, slab, cols), BF16))
        else:
            slab = cols // steps
            assert cols % (steps * LANES) == 0 and col0 % slab == 0, (c.shape, col0, cols, steps)
            first = col0 // slab
            in_specs.append(pl.BlockSpec((rows, slab), lambda i, j: (0, first + i * nj + j)))
            out_specs.append(pl.BlockSpec((rows, slab), lambda i, j: (0, i * nj + j)))
            args.append(c)
            shapes.append(jax.ShapeDtypeStruct((rows, cols), BF16))
    return in_specs, out_specs, args, shapes


def _mm(a, w, out_dtype, *, bias=None, epilogue=None, casts=(), tn_pref=MM_TILE, name):
    t, k = a.shape
    n = w.shape[1]
    tm = _tile(t, MM_TILE)
    tn = tn_pref if n % tn_pref == 0 else _tile(n, MM_TILE)
    ni, nj = t // tm, n // tn
    in_specs = [pl.BlockSpec((tm, k), lambda i, j: (i, 0)),
                pl.BlockSpec((k, tn), lambda i, j: (0, j))]
    args = [a, w]
    if bias is not None:
        in_specs.append(pl.BlockSpec((1, tn), lambda i, j: (0, j)))
        args.append(bias.reshape(1, n))
    cast_in_specs, cast_out_specs, cast_args, cast_shapes = _cast_plan(casts, ni, nj)
    in_specs += cast_in_specs
    args += cast_args
    out_specs = [pl.BlockSpec((tm, tn), lambda i, j: (i, j))] + cast_out_specs
    out_shape = [jax.ShapeDtypeStruct((t, n), out_dtype)] + cast_shapes
    outs = pl.pallas_call(
        functools.partial(_mm_kernel, epilogue=epilogue, has_bias=bias is not None,
                          n_casts=len(casts)),
        out_shape=out_shape,
        grid=(ni, nj),
        in_specs=in_specs,
        out_specs=out_specs,
        compiler_params=_cparams(("arbitrary", "arbitrary")),
        name=name,
    )(*args)
    return outs[0], [o.reshape(-1, o.shape[-1]) for o in outs[1:]]


def _mm_ksplit_kernel(a_ref, w_ref, o_ref):
    @pl.when(pl.program_id(2) == 0)
    def _():
        o_ref[...] = jnp.zeros_like(o_ref)

    o_ref[...] += jnp.dot(a_ref[...], w_ref[...], preferred_element_type=F32)


def _mm_ksplit(a, w, *, name):
    t, k = a.shape
    n = w.shape[1]
    tm = _tile(t, MM_TILE)
    tn = _tile(n, MM_TILE)
    tk = _tile(k, MM_K_TILE)
    return pl.pallas_call(
        _mm_ksplit_kernel,
        out_shape=jax.ShapeDtypeStruct((t, n), F32),
        grid=(t // tm, n // tn, k // tk),
        in_specs=[pl.BlockSpec((tm, tk), lambda i, j, kk: (i, kk)),
                  pl.BlockSpec((tk, tn), lambda i, j, kk: (kk, j))],
        out_specs=pl.BlockSpec((tm, tn), lambda i, j, kk: (i, j)),
        compiler_params=_cparams(("arbitrary", "arbitrary", "arbitrary")),
        name=name,
    )(a, w)


def _rope_tables(seq):
    inv_freq = 1.0 / (ROPE_THETA ** (jnp.arange(0, HEAD_DIM, 2, dtype=F32) / HEAD_DIM))
    ang = jnp.arange(seq, dtype=F32)[:, None] * inv_freq[None, :]
    cos, sin = jnp.cos(ang), jnp.sin(ang)
    return jnp.concatenate([cos, cos], axis=-1), jnp.concatenate([-sin, sin], axis=-1)


def kernel(x, c, w_ada, b_ada, g_pre_mix, g_post_mix, g_pre_mlp, g_post_mlp, w_in, attn_sinks,
           w_branch_gate, b_branch_gate, w_proj_swa, w_proj_sb, w_out, w_ff_in, w_ff_out):
    batch, seq, d = x.shape
    depth = w_in.shape[0]
    d_in = w_in.shape[2]
    swa_q_heads = attn_sinks.shape[1]
    swa_q_w = w_proj_swa.shape[1]
    sb_w = w_proj_sb.shape[1]
    swa_kv_w = (d_in - swa_q_w - 3 * sb_w) // 2
    sb_heads = sb_w // HEAD_DIM
    assert swa_q_w == swa_q_heads * HEAD_DIM and swa_kv_w * SWA_GROUP == swa_q_w
    assert seq % SWA_WINDOW == 0
    t = batch * seq

    cos_full, sin_signed = _rope_tables(seq)
    pad_rows = (-batch) % 8
    c_pad = jnp.pad(c, ((0, pad_rows), (0, 0)))

    swa_w = swa_q_w + 2 * swa_kv_w
    n_rope = swa_q_w + swa_kv_w

    for l in range(depth):
        mod = _ada_mod(c_pad, w_ada[l], b_ada[l])[:batch]
        mod3 = mod.reshape(batch * N_MOD, 1, d)

        h = _prenorm(x, g_pre_mix[l], mod3, 0, 1).reshape(t, d)
        proj_swa, (w_sb_in_bf,) = _rope_proj(
            h, w_in[l][:, :swa_w].astype(BF16), cos_full, sin_signed, n_rope=n_rope, seq=seq,
            casts=((w_in[l], swa_w, d_in - swa_w),))
        proj_sb, (w_gate_bf,) = _mm(h, w_sb_in_bf, BF16, casts=(w_branch_gate[l],),
                                    tn_pref=SB_PROJ_TN, name="sb_proj")
        gates, (w_swa_bf, w_sb_bf, w_out_bf, w_ff_in_bf) = _mm(
            h, w_gate_bf, BF16, bias=b_branch_gate[l], epilogue="sigmoid",
            casts=(w_proj_swa[l], w_proj_sb[l], w_out[l], w_ff_in[l]), name="gate_proj")
        a_swa = _swa(proj_swa, attn_sinks[l], batch=batch, seq=seq, q_heads=swa_q_heads,
                     k_col=swa_q_w, v_col=n_rope)
        a_sb = _sb(proj_sb, batch=batch, seq=seq, heads=sb_heads,
                   q_col=0, k_col=sb_w, v_col=2 * sb_w)
        merged = _merge(a_swa, a_sb, w_swa_bf, w_sb_bf, gates)
        mixed, _ = _mm(merged, w_out_bf, F32, name="out_proj")
        act, x1, (w_ff_out_bf,) = _mlp_in(
            mixed, x.reshape(t, d), g_post_mix[l], g_pre_mlp[l], mod3, w_ff_in_bf, seq=seq,
            k_gate=2, k_shift=3, k_scale=4, casts=(w_ff_out[l],))
        f = _mm_ksplit(act, w_ff_out_bf, name="ff_out").reshape(batch, seq, d)
        x = _post(f, x1.reshape(batch, seq, d), g_post_mlp[l], mod3, 5)
    return x
```

```python
import functools

import jax
import jax.numpy as jnp
from jax import lax
from jax.experimental import pallas as pl
from jax.experimental.pallas import tpu as pltpu

F32 = jnp.float32
BF16 = jnp.bfloat16

HEAD_DIM = 128
LANES = 128
BF16_SUBLANES = 16
SWA_WINDOW = 128
SWA_GROUP = 4
ROPE_THETA = 10000.0
NORM_EPS = 1e-6
LOG2E = 1.4426950408889634
SB_UNDERFLOW_LOG2 = -152.0
N_MOD = 6
MIB = 1024 * 1024
VMEM_LIMIT_BYTES = 56 * MIB

MM_TILE = 1024
MM_K_TILE = 4096
SB_PROJ_TN = 768
NORM_ROWS = 512
SWA_STEP_ROWS = 4096
SB_BLOCK = 256


def _tile(dim, pref):
    t = min(dim, pref)
    while dim % t:
        t //= 2
    return t


def _cparams(semantics):
    return pltpu.CompilerParams(dimension_semantics=semantics, vmem_limit_bytes=VMEM_LIMIT_BYTES)


def _ada_kernel(c_ref, w_ref, b_ref, o_ref):
    c = c_ref[...]
    act = (c * jax.nn.sigmoid(c)).astype(BF16)
    o_ref[...] = jnp.dot(act, w_ref[...].astype(BF16), preferred_element_type=F32) + b_ref[...]


def _ada_mod(c_pad, w_ada, b_ada):
    rows, d = c_pad.shape
    n = w_ada.shape[1]
    tn = _tile(n, MM_TILE)
    return pl.pallas_call(
        _ada_kernel,
        out_shape=jax.ShapeDtypeStruct((rows, n), F32),
        grid=(n // tn,),
        in_specs=[
            pl.BlockSpec((rows, d), lambda j: (0, 0)),
            pl.BlockSpec((d, tn), lambda j: (0, j)),
            pl.BlockSpec((1, tn), lambda j: (0, j)),
        ],
        out_specs=pl.BlockSpec((rows, tn), lambda j: (0, j)),
        compiler_params=_cparams(("arbitrary",)),
        name="ada_mod",
    )(c_pad, w_ada, b_ada.reshape(1, n))


def _rms(x, g):
    y = x * lax.rsqrt(jnp.mean(x * x, axis=-1, keepdims=True) + NORM_EPS)
    return y * g


def _prenorm_kernel(x_ref, g_ref, scale_ref, shift_ref, h_ref):
    h = _rms(x_ref[0], g_ref[...]) * (1.0 + scale_ref[0]) + shift_ref[0]
    h_ref[0] = h.astype(h_ref.dtype)


def _mod_spec(d, k):
    return pl.BlockSpec((1, 1, d), lambda b, i: (b * N_MOD + k, 0, 0))


def _prenorm(x, g, mod3, k_shift, k_scale):
    b, s, d = x.shape
    ts = _tile(s, NORM_ROWS)
    row = pl.BlockSpec((1, ts, d), lambda bb, i: (bb, i, 0))
    return pl.pallas_call(
        _prenorm_kernel,
        out_shape=jax.ShapeDtypeStruct((b, s, d), BF16),
        grid=(b, s // ts),
        in_specs=[row, pl.BlockSpec((1, d), lambda bb, i: (0, 0)),
                  _mod_spec(d, k_scale), _mod_spec(d, k_shift)],
        out_specs=row,
        compiler_params=_cparams(("arbitrary", "arbitrary")),
        name="prenorm",
    )(x, g.reshape(1, d), mod3, mod3)


def _post_pre_math(y, x, gpost, gate, gpre, scale, shift):
    x1 = x + _rms(y, gate * gpost)
    h = _rms(x1, gpre * (1.0 + scale)) + shift
    return x1, h


def _first_h2_kernel(y_ref, x_ref, gpost_ref, gate_ref, gpre_ref, scale_ref, shift_ref, h_ref):
    _, h = _post_pre_math(y_ref[...], x_ref[...], gpost_ref[...], gate_ref[0], gpre_ref[...],
                          scale_ref[0], shift_ref[0])
    h_ref[...] = h.astype(h_ref.dtype)


def _first_h2(y, x, gpost, gpre, mod3, rows, k_gate, k_shift, k_scale):
    d = x.shape[1]
    ts = _tile(rows, NORM_ROWS)
    row = pl.BlockSpec((ts, d), lambda i: (i, 0))
    vec = pl.BlockSpec((1, d), lambda i: (0, 0))

    def mod(k):
        return pl.BlockSpec((1, 1, d), lambda i: (k, 0, 0))

    return pl.pallas_call(
        _first_h2_kernel,
        out_shape=jax.ShapeDtypeStruct((rows, d), BF16),
        grid=(rows // ts,),
        in_specs=[row, row, vec, mod(k_gate), vec, mod(k_scale), mod(k_shift)],
        out_specs=row,
        compiler_params=_cparams(("arbitrary",)),
        name="first_h2",
    )(y, x, gpost.reshape(1, d), mod3, gpre.reshape(1, d), mod3, mod3)


def _mlp_in_kernel(*refs, chunk, n_casts):
    (y_ref, x_ref, gpost_ref, gate_ref, gpre_ref, scale_ref, shift_ref, h_first_hbm,
     w_ref) = refs[:9]
    cast_in = refs[9:9 + n_casts]
    act_ref, x1_ref = refs[9 + n_casts:11 + n_casts]
    cast_out = refs[11 + n_casts:11 + 2 * n_casts]
    h_even, h_odd, sem = refs[11 + 2 * n_casts:]
    i = pl.program_id(0)
    j = pl.program_id(1)

    @pl.when(jnp.logical_and(i == 0, j == 0))
    def _():
        first = pltpu.make_async_copy(h_first_hbm, h_even, sem)
        first.start()
        first.wait()

    for src, dst in zip(cast_in, cast_out):
        dst[...] = src[...].astype(dst.dtype)

    def step(h_cur, h_next):
        x1, h = _post_pre_math(y_ref[...], x_ref[...], gpost_ref[...], gate_ref[0], gpre_ref[...],
                               scale_ref[0], shift_ref[0])
        x1_ref[...] = x1
        h_next[pl.ds(pl.multiple_of(j * chunk, chunk), chunk), :] = h.astype(h_next.dtype)
        acc = jnp.dot(h_cur[...], w_ref[...], preferred_element_type=F32)
        act_ref[...] = jnp.square(jnp.maximum(acc, 0.0)).astype(act_ref.dtype)

    @pl.when(i % 2 == 0)
    def _():
        step(h_even, h_odd)

    @pl.when(i % 2 == 1)
    def _():
        step(h_odd, h_even)


def _mlp_in(y, x, gpost, gpre, mod3, w, *, seq, k_gate, k_shift, k_scale, casts=()):
    t, d = x.shape
    n = w.shape[1]
    tm = _tile(seq, MM_TILE)
    tn = _tile(n, MM_TILE)
    ni, nj = t // tm, n // tn
    chunk = tm // nj
    assert tm % nj == 0 and chunk % BF16_SUBLANES == 0
    blocks_per_batch = seq // tm

    def nxt(i):
        return (i + 1) % ni

    rows = pl.BlockSpec((chunk, d), lambda i, j: (nxt(i) * nj + j, 0))
    vec = pl.BlockSpec((1, d), lambda i, j: (0, 0))

    def mod(k):
        return pl.BlockSpec((1, 1, d),
                            lambda i, j: ((nxt(i) // blocks_per_batch) * N_MOD + k, 0, 0))

    h_first = _first_h2(y, x, gpost, gpre, mod3, tm, k_gate, k_shift, k_scale)
    cast_in_specs, cast_out_specs, cast_args, cast_shapes = _cast_plan(casts, ni, nj)
    outs = pl.pallas_call(
        functools.partial(_mlp_in_kernel, chunk=chunk, n_casts=len(casts)),
        out_shape=[jax.ShapeDtypeStruct((t, n), BF16), jax.ShapeDtypeStruct((t, d), F32)]
        + cast_shapes,
        grid=(ni, nj),
        in_specs=[rows, rows, vec, mod(k_gate), vec, mod(k_scale), mod(k_shift),
                  pl.BlockSpec(memory_space=pl.ANY),
                  pl.BlockSpec((d, tn), lambda i, j: (0, j))] + cast_in_specs,
        out_specs=[pl.BlockSpec((tm, tn), lambda i, j: (i, j)), rows] + cast_out_specs,
        scratch_shapes=[pltpu.VMEM((tm, d), BF16), pltpu.VMEM((tm, d), BF16),
                        pltpu.SemaphoreType.DMA(())],
        compiler_params=_cparams(("arbitrary", "arbitrary")),
        name="ff_in",
    )(y, x, gpost.reshape(1, d), mod3, gpre.reshape(1, d), mod3, mod3, h_first, w, *cast_args)
    return outs[0], outs[1], [o.reshape(-1, o.shape[-1]) for o in outs[2:]]


def _post_kernel(y_ref, x_ref, gpost_ref, gate_ref, o_ref):
    o_ref[0] = x_ref[0] + gate_ref[0] * _rms(y_ref[0], gpost_ref[...])


def _post(y, x, gpost, mod3, k_gate):
    b, s, d = x.shape
    ts = _tile(s, NORM_ROWS)
    row = pl.BlockSpec((1, ts, d), lambda bb, i: (bb, i, 0))
    return pl.pallas_call(
        _post_kernel,
        out_shape=jax.ShapeDtypeStruct((b, s, d), F32),
        grid=(b, s // ts),
        in_specs=[row, row, pl.BlockSpec((1, d), lambda bb, i: (0, 0)), _mod_spec(d, k_gate)],
        out_specs=row,
        compiler_params=_cparams(("arbitrary", "arbitrary")),
        name="post_norm",
    )(y, x, gpost.reshape(1, d), mod3)


def _rope_proj_kernel(*refs, full_rope_tiles, partial_chunks, n_casts):
    a_ref, w_ref, cos_ref, sin_ref = refs[:4]
    cast_in = refs[4:4 + n_casts]
    o_ref = refs[4 + n_casts]
    cast_out = refs[5 + n_casts:]
    for src, dst in zip(cast_in, cast_out):
        dst[...] = src[...].astype(dst.dtype)
    j = pl.program_id(1)
    acc = jnp.dot(a_ref[...], w_ref[...], preferred_element_type=F32)
    cos, sin = cos_ref[...], sin_ref[...]
    head = j <= full_rope_tiles
    tail = j < full_rope_tiles
    tables = {True: (jnp.where(head, cos, 1.0), jnp.where(head, sin, 0.0)),
              False: (jnp.where(tail, cos, 1.0), jnp.where(tail, sin, 0.0))}
    for c in range(acc.shape[1] // LANES):
        cols = slice(c * LANES, (c + 1) * LANES)
        cos_c, sin_c = tables[c < partial_chunks]
        blk = acc[:, cols]
        o_ref[:, cols] = (blk * cos_c + pltpu.roll(blk, HEAD_DIM // 2, 1) * sin_c).astype(o_ref.dtype)


def _rope_proj(h, w, cos_full, sin_signed, *, n_rope, seq, casts=()):
    t, d = h.shape
    n = w.shape[1]
    tm = _tile(seq, MM_TILE)
    tn = _tile(n, MM_TILE)
    assert n_rope % LANES == 0 and tn % LANES == 0
    n_seq_tiles = seq // tm
    ni, nj = t // tm, n // tn
    cast_in_specs, cast_out_specs, cast_args, cast_shapes = _cast_plan(casts, ni, nj)
    kern = functools.partial(_rope_proj_kernel, full_rope_tiles=n_rope // tn,
                             partial_chunks=(n_rope % tn) // LANES, n_casts=len(casts))
    outs = pl.pallas_call(
        kern,
        out_shape=[jax.ShapeDtypeStruct((t, n), BF16)] + cast_shapes,
        grid=(ni, nj),
        in_specs=[
            pl.BlockSpec((tm, d), lambda i, j: (i, 0)),
            pl.BlockSpec((d, tn), lambda i, j: (0, j)),
            pl.BlockSpec((tm, HEAD_DIM), lambda i, j: (i % n_seq_tiles, 0)),
            pl.BlockSpec((tm, HEAD_DIM), lambda i, j: (i % n_seq_tiles, 0)),
        ] + cast_in_specs,
        out_specs=[pl.BlockSpec((tm, tn), lambda i, j: (i, j))] + cast_out_specs,
        compiler_params=_cparams(("arbitrary", "arbitrary")),
        name="swa_proj",
    )(h, w, cos_full, sin_signed, *cast_args)
    return outs[0], [o.reshape(-1, o.shape[-1]) for o in outs[1:]]


def _swa_kernel(sink_ref, q_ref, kp_ref, k_ref, vp_ref, v_ref, o_ref, p_ref, *, tq):
    i = pl.program_id(1)
    kvh = pl.program_id(2)
    w = SWA_WINDOW
    rows = SWA_GROUP * w
    qi = lax.broadcasted_iota(jnp.int32, (rows, 2 * w), 0) % w
    kj = lax.broadcasted_iota(jnp.int32, (rows, 2 * w), 1)
    band = jnp.logical_and(kj > qi, kj <= qi + w)
    first = jnp.logical_and(band, jnp.logical_or(kj >= w, i > 0))
    sink_fill = jnp.concatenate(
        [jnp.full((w, 2 * w), sink_ref[kvh * SWA_GROUP + g], F32) for g in range(SWA_GROUP)], axis=0)
    masked_fill = jnp.where(kj == 0, sink_fill * LOG2E, -jnp.inf)
    key_row = lax.broadcasted_iota(jnp.int32, (2 * w, HEAD_DIM), 0)
    ones = jnp.ones((2 * w, HEAD_DIM), BF16)
    scale = HEAD_DIM ** -0.5 * LOG2E
    n_sub = tq // w
    def probs_stage(s):
        q = jnp.concatenate(
            [q_ref[s * w:(s + 1) * w, g * HEAD_DIM:(g + 1) * HEAD_DIM] for g in range(SWA_GROUP)],
            axis=0)
        if s == 0:
            k = jnp.concatenate([kp_ref[...], k_ref[0:w, :]], axis=0)
            valid = first
        else:
            k = k_ref[(s - 1) * w:(s + 1) * w, :]
            valid = band
        sc = lax.dot_general(q, k, (((1,), (1,)), ((), ())), preferred_element_type=F32) * scale
        sc = jnp.where(valid, sc, masked_fill)
        p_ref[s] = jnp.exp2(sc - jnp.max(sc, axis=-1, keepdims=True)).astype(BF16)

    def values_stage(s):
        if s == 0:
            v = jnp.concatenate([vp_ref[...], v_ref[0:w, :]], axis=0)
        else:
            v = v_ref[(s - 1) * w:(s + 1) * w, :]
        v = jnp.where(key_row == 0, jnp.zeros_like(v), v)
        ov = jnp.dot(p_ref[s], jnp.concatenate([v, ones], axis=1), preferred_element_type=F32)
        o = ov[:, :HEAD_DIM] / ov[:, HEAD_DIM:]
        for g in range(SWA_GROUP):
            o_ref[s * w:(s + 1) * w, g * HEAD_DIM:(g + 1) * HEAD_DIM] = (
                o[g * w:(g + 1) * w, :].astype(o_ref.dtype))

    for s in range(n_sub):
        probs_stage(s)
        if s >= 1:
            values_stage(s - 1)
    values_stage(n_sub - 1)


def _swa(proj, sinks, *, batch, seq, q_heads, k_col, v_col):
    kv_heads = q_heads // SWA_GROUP
    w = SWA_WINDOW
    tq = _tile(seq, SWA_STEP_ROWS)
    nq = seq // tq
    sub = tq // w
    gw = SWA_GROUP * HEAD_DIM
    kb, vb = k_col // HEAD_DIM, v_col // HEAD_DIM

    def prev_rows(b, i):
        return b * (seq // w) + jnp.maximum(i * sub - 1, 0)

    grid_spec = pltpu.PrefetchScalarGridSpec(
        num_scalar_prefetch=1,
        grid=(batch, nq, kv_heads),
        in_specs=[
            pl.BlockSpec((tq, gw), lambda b, i, h, s: (b * nq + i, h)),
            pl.BlockSpec((w, HEAD_DIM), lambda b, i, h, s: (prev_rows(b, i), kb + h)),
            pl.BlockSpec((tq, HEAD_DIM), lambda b, i, h, s: (b * nq + i, kb + h)),
            pl.BlockSpec((w, HEAD_DIM), lambda b, i, h, s: (prev_rows(b, i), vb + h)),
            pl.BlockSpec((tq, HEAD_DIM), lambda b, i, h, s: (b * nq + i, vb + h)),
        ],
        out_specs=pl.BlockSpec((tq, gw), lambda b, i, h, s: (b * nq + i, h)),
        scratch_shapes=[pltpu.VMEM((sub, SWA_GROUP * w, 2 * w), BF16)],
    )
    return pl.pallas_call(
        functools.partial(_swa_kernel, tq=tq),
        out_shape=jax.ShapeDtypeStruct((batch * seq, q_heads * HEAD_DIM), BF16),
        grid_spec=grid_spec,
        compiler_params=_cparams(("arbitrary", "arbitrary", "arbitrary")),
        name="swa_attention",
    )(sinks, proj, proj, proj, proj, proj)


def _sb_kernel(q_ref, kc_ref, kp_ref, vc_ref, vp_ref, kv_hbm, o_ref, hilo0_ref, hilo1_ref, lb0_ref,
               lb1_ref, acc_ref, r_ref, kbuf_ref, vbuf_ref, sem, *, tq, hp, seq, k_col, v_col):
    hilo_refs = (hilo0_ref, hilo1_ref)
    lb_refs = (lb0_ref, lb1_ref)
    i = pl.program_id(2)
    row = lax.broadcasted_iota(jnp.int32, (tq, tq), 0)
    col = lax.broadcasted_iota(jnp.int32, (tq, tq), 1)
    strict_lower = col < row
    neg_later = jnp.where(row > col, -1.0, 0.0).astype(BF16)
    neg_later = jnp.concatenate([neg_later, neg_later], axis=0)
    z_scale = HEAD_DIM ** -0.5 * LOG2E

    def logits_stage(slot, k_ref, diag, heads=range(hp)):
        for h in heads:
            cols = slice(h * HEAD_DIM, (h + 1) * HEAD_DIM)
            q = q_ref[:, cols]
            k = k_ref[:, cols]
            z = lax.dot_general(q, k, (((1,), (1,)), ((), ())),
                                preferred_element_type=F32) * z_scale
            neg_abs = -jnp.abs(z)
            softplus = jnp.maximum(z, 0.0) + jnp.log(1.0 + jnp.exp2(neg_abs)) * LOG2E
            log_beta = z - softplus
            if diag:
                lb_refs[slot][h] = jnp.where(strict_lower, log_beta, -jnp.inf)
                softplus = jnp.where(strict_lower, softplus, 0.0)
                r_ref[h] = -jnp.sum(softplus, axis=-1, keepdims=True)
            else:
                r_sum = r_ref[h]
                lb_refs[slot][h] = log_beta + r_sum
                r_ref[h] = r_sum - jnp.sum(softplus, axis=-1, keepdims=True)
            hi = softplus.astype(BF16)
            lo = (softplus - hi.astype(F32)).astype(BF16)
            hilo_refs[slot][h] = jnp.concatenate([hi, lo], axis=1)

    def weights_stage(slot, v_ref, first, heads=range(hp)):
        for h in heads:
            cols = slice(h * HEAD_DIM, (h + 1) * HEAD_DIM)
            v = v_ref[:, cols]
            suffix = jnp.dot(hilo_refs[slot][h], neg_later, preferred_element_type=F32)
            att = jnp.exp2(lb_refs[slot][h] + suffix)
            pv = jnp.dot(att.astype(BF16), v, preferred_element_type=F32)
            if first:
                acc_ref[h] = pv
            else:
                acc_ref[h] += pv

    def any_row_alive():
        top = functools.reduce(
            jnp.maximum, [jnp.max(r_ref[h], axis=0, keepdims=True) for h in range(hp)])
        return top[0, 0] > SB_UNDERFLOW_LOG2

    def staggered(logits_stages, weights_stages):
        lag = 2
        for k, stage in enumerate(logits_stages):
            stage()
            if k >= lag:
                weights_stages[k - lag]()
        for stage in weights_stages[max(len(logits_stages) - lag, 0):]:
            stage()

    def per_head(stage, *args):
        return [functools.partial(stage, *args, heads=(h,)) for h in range(hp)]

    @pl.when(i == 0)
    def _():
        staggered(per_head(logits_stage, 0, kc_ref, True), per_head(weights_stage, 0, vc_ref, True))

    @pl.when(i > 0)
    def _():
        staggered(
            per_head(logits_stage, 0, kc_ref, True) + per_head(logits_stage, 1, kp_ref, False),
            per_head(weights_stage, 0, vc_ref, True) + per_head(weights_stage, 1, vp_ref, False))

        def cond(state):
            n, alive = state
            return jnp.logical_and(n <= i, alive)

        def body(state):
            n, _ = state
            row0 = pl.multiple_of(pl.program_id(0) * seq + (i - n) * tq, tq)
            col0 = pl.program_id(1) * (hp * HEAD_DIM)
            copies = [
                pltpu.make_async_copy(
                    kv_hbm.at[pl.ds(row0, tq), pl.ds(pl.multiple_of(c + col0, LANES), hp * HEAD_DIM)],
                    buf, sem.at[s])
                for s, (c, buf) in enumerate(((k_col, kbuf_ref), (v_col, vbuf_ref)))]
            for cp in copies:
                cp.start()
            for cp in copies:
                cp.wait()
            staggered(per_head(logits_stage, 1, kbuf_ref, False),
                      per_head(weights_stage, 1, vbuf_ref, False))
            return n + 1, any_row_alive()

        lax.while_loop(cond, body, (jnp.int32(2), any_row_alive()))

    for h in range(hp):
        o_ref[:, h * HEAD_DIM:(h + 1) * HEAD_DIM] = acc_ref[h].astype(o_ref.dtype)


def _sb(proj, *, batch, seq, heads, q_col, k_col, v_col):
    tq = _tile(seq, SB_BLOCK)
    nq = seq // tq
    hp = next(c for c in (16, 8, 4, 2, 1) if heads % c == 0)
    hw = hp * HEAD_DIM
    assert q_col % hw == 0 and k_col % hw == 0 and v_col % hw == 0
    qb, kb, vb = q_col // hw, k_col // hw, v_col // hw

    def cur(col):
        return pl.BlockSpec((tq, hw), lambda b, h, i: (b * nq + i, col + h))

    def prev(col):
        return pl.BlockSpec((tq, hw), lambda b, h, i: (b * nq + jnp.maximum(i - 1, 0), col + h))

    return pl.pallas_call(
        functools.partial(_sb_kernel, tq=tq, hp=hp, seq=seq, k_col=k_col, v_col=v_col),
        out_shape=jax.ShapeDtypeStruct((batch * seq, heads * HEAD_DIM), BF16),
        grid=(batch, heads // hp, nq),
        in_specs=[cur(qb), cur(kb), prev(kb), cur(vb), prev(vb),
                  pl.BlockSpec(memory_space=pl.ANY)],
        out_specs=pl.BlockSpec((tq, hw), lambda b, h, i: (b * nq + i, h)),
        scratch_shapes=[pltpu.VMEM((hp, tq, 2 * tq), BF16), pltpu.VMEM((hp, tq, 2 * tq), BF16),
                        pltpu.VMEM((hp, tq, tq), F32), pltpu.VMEM((hp, tq, tq), F32),
                        pltpu.VMEM((hp, tq, HEAD_DIM), F32), pltpu.VMEM((hp, tq, 1), F32),
                        pltpu.VMEM((tq, hw), BF16), pltpu.VMEM((tq, hw), BF16),
                        pltpu.SemaphoreType.DMA((2,))],
        compiler_params=_cparams(("arbitrary", "arbitrary", "arbitrary")),
        name="sb_attention",
    )(proj, proj, proj, proj, proj, proj)


def _merge_kernel(a1_ref, a2_ref, w1_ref, w2_ref, ga_ref, gb_ref, o_ref):
    y1 = jnp.dot(a1_ref[...], w1_ref[...], preferred_element_type=F32)
    y2 = jnp.dot(a2_ref[...], w2_ref[...], preferred_element_type=F32)
    o = ga_ref[...].astype(F32) * y1 + gb_ref[...].astype(F32) * y2
    o_ref[...] = o.astype(o_ref.dtype)


def _merge(a_swa, a_sb, w_swa, w_sb, gates):
    t, k1 = a_swa.shape
    k2 = a_sb.shape[1]
    d = w_swa.shape[1]
    tm = _tile(t, MM_TILE)
    tn = _tile(d, MM_TILE)
    ga, gb = 0, d // tn
    return pl.pallas_call(
        _merge_kernel,
        out_shape=jax.ShapeDtypeStruct((t, d), BF16),
        grid=(t // tm, d // tn),
        in_specs=[
            pl.BlockSpec((tm, k1), lambda i, j: (i, 0)),
            pl.BlockSpec((tm, k2), lambda i, j: (i, 0)),
            pl.BlockSpec((k1, tn), lambda i, j: (0, j)),
            pl.BlockSpec((k2, tn), lambda i, j: (0, j)),
            pl.BlockSpec((tm, tn), lambda i, j: (i, ga + j)),
            pl.BlockSpec((tm, tn), lambda i, j: (i, gb + j)),
        ],
        out_specs=pl.BlockSpec((tm, tn), lambda i, j: (i, j)),
        compiler_params=_cparams(("arbitrary", "arbitrary")),
        name="branch_merge",
    )(a_swa, a_sb, w_swa, w_sb, gates, gates)


def _mm_kernel(*refs, epilogue, has_bias, n_casts):
    a_ref, w_ref = refs[:2]
    b_ref = refs[2] if has_bias else None
    first_cast = 2 + has_bias
    cast_in = refs[first_cast:first_cast + n_casts]
    o_ref = refs[first_cast + n_casts]
    cast_out = refs[first_cast + n_casts + 1:]
    for src, dst in zip(cast_in, cast_out):
        dst[...] = src[...].astype(dst.dtype)
    acc = jnp.dot(a_ref[...], w_ref[...], preferred_element_type=F32)
    if has_bias:
        acc = acc + b_ref[...]
    if epilogue == "relu_sq":
        acc = jnp.square(jnp.maximum(acc, 0.0))
    elif epilogue == "sigmoid":
        acc = 0.5 * jnp.tanh(0.5 * acc) + 0.5
    o_ref[...] = acc.astype(o_ref.dtype)


def _cast_plan(casts, ni, nj):
    steps = ni * nj
    in_specs, out_specs, args, shapes = [], [], [], []
    for c in casts:
        c, col0, cols = c if isinstance(c, tuple) else (c, 0, c.shape[1])
        rows = c.shape[0]
        if col0 == 0 and cols == c.shape[1] and rows % (steps * BF16_SUBLANES) == 0:
            slab = rows // steps
            spec = pl.BlockSpec((1, slab, cols), lambda i, j: (i * nj + j, 0, 0))
            in_specs.append(spec)
            out_specs.append(spec)
            args.append(c.reshape(steps, slab, cols))
            shapes.append(jax.ShapeDtypeStruct((steps, slab, cols), BF16))
        else:
            slab = cols // steps
            assert cols % (steps * LANES) == 0 and col0 % slab == 0, (c.shape, col0, cols, steps)
            first = col0 // slab
            in_specs.append(pl.BlockSpec((rows, slab), lambda i, j: (0, first + i * nj + j)))
            out_specs.append(pl.BlockSpec((rows, slab), lambda i, j: (0, i * nj + j)))
            args.append(c)
            shapes.append(jax.ShapeDtypeStruct((rows, cols), BF16))
    return in_specs, out_specs, args, shapes


def _mm(a, w, out_dtype, *, bias=None, epilogue=None, casts=(), tn_pref=MM_TILE, name):
    t, k = a.shape
    n = w.shape[1]
    tm = _tile(t, MM_TILE)
    tn = tn_pref if n % tn_pref == 0 else _tile(n, MM_TILE)
    ni, nj = t // tm, n // tn
    in_specs = [pl.BlockSpec((tm, k), lambda i, j: (i, 0)),
                pl.BlockSpec((k, tn), lambda i, j: (0, j))]
    args = [a, w]
    if bias is not None:
        in_specs.append(pl.BlockSpec((1, tn), lambda i, j: (0, j)))
        args.append(bias.reshape(1, n))
    cast_in_specs, cast_out_specs, cast_args, cast_shapes = _cast_plan(casts, ni, nj)
    in_specs += cast_in_specs
    args += cast_args
    out_specs = [pl.BlockSpec((tm, tn), lambda i, j: (i, j))] + cast_out_specs
    out_shape = [jax.ShapeDtypeStruct((t, n), out_dtype)] + cast_shapes
    outs = pl.pallas_call(
        functools.partial(_mm_kernel, epilogue=epilogue, has_bias=bias is not None,
                          n_casts=len(casts)),
        out_shape=out_shape,
        grid=(ni, nj),
        in_specs=in_specs,
        out_specs=out_specs,
        compiler_params=_cparams(("arbitrary", "arbitrary")),
        name=name,
    )(*args)
    return outs[0], [o.reshape(-1, o.shape[-1]) for o in outs[1:]]


def _mm_ksplit_kernel(a_ref, w_ref, o_ref):
    @pl.when(pl.program_id(2) == 0)
    def _():
        o_ref[...] = jnp.zeros_like(o_ref)

    o_ref[...] += jnp.dot(a_ref[...], w_ref[...], preferred_element_type=F32)


def _mm_ksplit(a, w, *, name):
    t, k = a.shape
    n = w.shape[1]
    tm = _tile(t, MM_TILE)
    tn = _tile(n, MM_TILE)
    tk = _tile(k, MM_K_TILE)
    return pl.pallas_call(
        _mm_ksplit_kernel,
        out_shape=jax.ShapeDtypeStruct((t, n), F32),
        grid=(t // tm, n // tn, k // tk),
        in_specs=[pl.BlockSpec((tm, tk), lambda i, j, kk: (i, kk)),
                  pl.BlockSpec((tk, tn), lambda i, j, kk: (kk, j))],
        out_specs=pl.BlockSpec((tm, tn), lambda i, j, kk: (i, j)),
        compiler_params=_cparams(("arbitrary", "arbitrary", "arbitrary")),
        name=name,
    )(a, w)


def _rope_tables(seq):
    inv_freq = 1.0 / (ROPE_THETA ** (jnp.arange(0, HEAD_DIM, 2, dtype=F32) / HEAD_DIM))
    ang = jnp.arange(seq, dtype=F32)[:, None] * inv_freq[None, :]
    cos, sin = jnp.cos(ang), jnp.sin(ang)
    return jnp.concatenate([cos, cos], axis=-1), jnp.concatenate([-sin, sin], axis=-1)


def kernel(x, c, w_ada, b_ada, g_pre_mix, g_post_mix, g_pre_mlp, g_post_mlp, w_in, attn_sinks,
           w_branch_gate, b_branch_gate, w_proj_swa, w_proj_sb, w_out, w_ff_in, w_ff_out):
    batch, seq, d = x.shape
    depth = w_in.shape[0]
    d_in = w_in.shape[2]
    swa_q_heads = attn_sinks.shape[1]
    swa_q_w = w_proj_swa.shape[1]
    sb_w = w_proj_sb.shape[1]
    swa_kv_w = (d_in - swa_q_w - 3 * sb_w) // 2
    sb_heads = sb_w // HEAD_DIM
    assert swa_q_w == swa_q_heads * HEAD_DIM and swa_kv_w * SWA_GROUP == swa_q_w
    assert seq % SWA_WINDOW == 0
    t = batch * seq

    cos_full, sin_signed = _rope_tables(seq)
    pad_rows = (-batch) % 8
    c_pad = jnp.pad(c, ((0, pad_rows), (0, 0)))

    swa_w = swa_q_w + 2 * swa_kv_w
    n_rope = swa_q_w + swa_kv_w

    for l in range(depth):
        mod = _ada_mod(c_pad, w_ada[l], b_ada[l])[:batch]
        mod3 = mod.reshape(batch * N_MOD, 1, d)

        h = _prenorm(x, g_pre_mix[l], mod3, 0, 1).reshape(t, d)
        proj_swa, (w_sb_in_bf,) = _rope_proj(
            h, w_in[l][:, :swa_w].astype(BF16), cos_full, sin_signed, n_rope=n_rope, seq=seq,
            casts=((w_in[l], swa_w, d_in - swa_w),))
        proj_sb, (w_gate_bf,) = _mm(h, w_sb_in_bf, BF16, casts=(w_branch_gate[l],),
                                    tn_pref=SB_PROJ_TN, name="sb_proj")
        gates, (w_swa_bf, w_sb_bf, w_out_bf, w_ff_in_bf) = _mm(
            h, w_gate_bf, BF16, bias=b_branch_gate[l], epilogue="sigmoid",
            casts=(w_proj_swa[l], w_proj_sb[l], w_out[l], w_ff_in[l]), name="gate_proj")
        a_swa = _swa(proj_swa, attn_sinks[l], batch=batch, seq=seq, q_heads=swa_q_heads,
                     k_col=swa_q_w, v_col=n_rope)
        a_sb = _sb(proj_sb, batch=batch, seq=seq, heads=sb_heads,
                   q_col=0, k_col=sb_w, v_col=2 * sb_w)
        merged = _merge(a_swa, a_sb, w_swa_bf, w_sb_bf, gates)
        mixed, _ = _mm(merged, w_out_bf, F32, name="out_proj")
        act, x1, (w_ff_out_bf,) = _mlp_in(
            mixed, x.reshape(t, d), g_post_mix[l], g_pre_mlp[l], mod3, w_ff_in_bf, seq=seq,
            k_gate=2, k_shift=3, k_scale=4, casts=(w_ff_out[l],))
        f = _mm_ksplit(act, w_ff_out_bf, name="ff_out").reshape(batch, seq, d)
        x = _post(f, x1.reshape(batch, seq, d), g_post_mlp[l], mod3, 5)
    return x
```
